```python
import jax, jax.numpy as jnp
from jax import lax
import numpy as np

D_MODEL = 2048
BATCH = 16
SEQ = 2048
DEPTH = 2

N_A_LAYERS = DEPTH - DEPTH // 2
N_B_LAYERS = DEPTH // 2
PLE_DIM = 256
D_FF = 5632
HG_EXPAND = 128
HG_HEADS = D_MODEL // HG_EXPAND
HG_DK = HG_EXPAND
HG_DV = D_MODEL // HG_HEADS
HG_CHUNK = 32
DA_HEAD_DIM = 128
DA_HEADS = D_MODEL // (2 * DA_HEAD_DIM)
DA_VDIM = 2 * DA_HEAD_DIM
Q_BLOCK = 128
ROPE_THETA = 10000.0
NORM_EPS = 1e-6

kernel_name = "yoco_hgrn2_diffattn_macaron_ple"


def rms_norm(x, g):
    xf = x.astype(jnp.float32)
    y = xf * lax.rsqrt(jnp.mean(xf * xf, axis=-1, keepdims=True) + NORM_EPS)
    return (y * g.astype(jnp.float32)).astype(x.dtype)


def swiglu_ffn(x, w_gate_up, w_down):
    gate, up = jnp.split(x @ w_gate_up, 2, axis=-1)
    return (jax.nn.silu(gate) * up) @ w_down


def rope(x, positions):
    dh = x.shape[-1]
    half = dh // 2
    inv_freq = ROPE_THETA ** (-jnp.arange(0, dh, 2, dtype=jnp.float32) / dh)
    ang = positions.astype(jnp.float32)[:, None] * inv_freq[None, :]
    bshape = (1, x.shape[1]) + (1,) * (x.ndim - 3) + (half,)
    cos = jnp.cos(ang).reshape(bshape)
    sin = jnp.sin(ang).reshape(bshape)
    xf = x.astype(jnp.float32)
    x1, x2 = xf[..., :half], xf[..., half:]
    out = jnp.concatenate([x1 * cos - x2 * sin, x2 * cos + x1 * sin], axis=-1)
    return out.astype(x.dtype)


def hgrn2_mixer(x, w_in, lb, out_gain, w_out):
    B, S, _ = x.shape
    d_f = HG_HEADS * HG_DK
    d_i = HG_HEADS * HG_DV
    proj = x @ w_in
    q = proj[..., :d_f]
    fz = proj[..., d_f:2 * d_f].astype(jnp.float32)
    v_in = proj[..., 2 * d_f:2 * d_f + d_i]
    g = proj[..., 2 * d_f + d_i:]
    lbf = lb.astype(jnp.float32)
    log_f = jnp.logaddexp(jnp.log(lbf), jnp.log1p(-lbf) + jax.nn.log_sigmoid(fz))
    k = (1.0 - lbf) * jax.nn.sigmoid(-fz)

    nc = S // HG_CHUNK

    def to_chunks(t, d):
        return t.reshape(B, nc, HG_CHUNK, HG_HEADS, d).transpose(1, 0, 3, 2, 4)

    qc = to_chunks(q.astype(jnp.float32) * (HG_DK ** -0.5), HG_DK)
    kc = to_chunks(k, HG_DK)
    vc = to_chunks(v_in.astype(jnp.float32), HG_DV)
    bc = jnp.cumsum(to_chunks(log_f, HG_DK), axis=3)
    causal = jnp.tril(jnp.ones((HG_CHUNK, HG_CHUNK), dtype=bool))

    def step(state, inp):
        q_c, k_c, v_c, b_c = inp
        o_inter = jnp.einsum('bhck,bhkv->bhcv', q_c * jnp.exp(b_c), state)
        diff = b_c[:, :, :, None, :] - b_c[:, :, None, :, :]
        decay = jnp.exp(jnp.where(causal[None, None, :, :, None], diff, -jnp.inf))
        attn = jnp.einsum('bhtk,bhsk,bhtsk->bhts', q_c, k_c, decay)
        o_intra = jnp.einsum('bhts,bhsv->bhtv', attn, v_c)
        b_last = b_c[:, :, -1:, :]
        new_state = state * jnp.exp(b_last)[:, :, 0, :, None] + jnp.einsum(
            'bhsk,bhsv->bhkv', k_c * jnp.exp(b_last - b_c), v_c)
        return new_state, o_inter + o_intra

    state0 = jnp.zeros((B, HG_HEADS, HG_DK, HG_DV), jnp.float32)
    _, o = lax.scan(step, state0, (qc, kc, vc, bc))
    o = o.transpose(1, 0, 3, 2, 4).reshape(B, S, HG_HEADS, HG_DV)
    o = rms_norm(o, out_gain).reshape(B, S, d_i).astype(x.dtype)
    o = o * jax.nn.silu(g)
    return o @ w_out


def shared_kv(h, kv_norm, w_kv, positions):
    B, S, _ = h.shape
    kv = rms_norm(h, kv_norm) @ w_kv
    d_k = DA_HEADS * 2 * DA_HEAD_DIM
    k = kv[..., :d_k].reshape(B, S, DA_HEADS, 2, DA_HEAD_DIM)
    v = kv[..., d_k:].reshape(B, S, DA_HEADS, DA_VDIM)
    return rope(k, positions), v


def diff_attention(x, k, v, w_q, lam, subln, w_out, lambda_init, positions):
    B, S, _ = x.shape
    q = (x @ w_q).reshape(B, S, DA_HEADS, 2, DA_HEAD_DIM)
    q = rope(q, positions) * (DA_HEAD_DIM ** -0.5)
    lam32 = lam.astype(jnp.float32)
    lambda_full = (jnp.exp(jnp.sum(lam32[0] * lam32[1])) - jnp.exp(jnp.sum(lam32[2] * lam32[3]))
                   + lambda_init)
    nq = S // Q_BLOCK
    qb = q.reshape(B, nq, Q_BLOCK, DA_HEADS, 2, DA_HEAD_DIM).transpose(1, 0, 2, 3, 4, 5)
    q_pos = positions.reshape(nq, Q_BLOCK)

    def one_block(args):
        q_blk, qp = args
        s = jnp.einsum('bqhcd,bkhcd->bhcqk', q_blk, k).astype(jnp.float32)
        mask = positions[None, :] <= qp[:, None]
        s = jnp.where(mask[None, None, None], s, -jnp.inf)
        pm = jax.nn.softmax(s, axis=-1)
        w = pm[:, :, 0] - lambda_full * pm[:, :, 1]
        return jnp.einsum('bhqk,bkhv->bqhv', w.astype(v.dtype), v)

    o = lax.map(one_block, (qb, q_pos))
    o = o.transpose(1, 0, 2, 3, 4).reshape(B, S, DA_HEADS, DA_VDIM)
    o = rms_norm(o, subln) * (1.0 - lambda_init)
    return o.reshape(B, S, DA_HEADS * DA_VDIM) @ w_out


def setup_inputs(seed: int = 0) -> dict:
    key = jax.random.key(seed)
    ks = jax.random.split(key, 22)
    f32 = jnp.float32

    def nrm(k, shape, scale):
        return jax.random.normal(k, shape, f32) * scale

    def gain(k, shape):
        return 1.0 + 0.02 * jax.random.normal(k, shape, f32)

    d_f = HG_HEADS * HG_DK
    d_i = HG_HEADS * HG_DV
    return {
        "x": nrm(ks[0], (BATCH, SEQ, D_MODEL), 1.0),
        "p": nrm(ks[1], (DEPTH, BATCH, SEQ, PLE_DIM), 1.0),
        "ffn_norm": gain(ks[2], (DEPTH, 2, D_MODEL)),
        "ffn_w_gate_up": nrm(ks[3], (DEPTH, 2, D_MODEL, 2 * D_FF), D_MODEL ** -0.5),
        "ffn_w_down": nrm(ks[4], (DEPTH, 2, D_FF, D_MODEL), D_FF ** -0.5),
        "mix_norm": gain(ks[5], (DEPTH, D_MODEL)),
        "hgrn_w_in": nrm(ks[6], (N_A_LAYERS, D_MODEL, 2 * d_f + 2 * d_i), D_MODEL ** -0.5),
        "hgrn_lower_bounds": nrm(ks[7], (N_A_LAYERS + 1, d_f), 0.5),
        "hgrn_out_norm": gain(ks[8], (N_A_LAYERS, HG_DV)),
        "hgrn_w_out": nrm(ks[9], (N_A_LAYERS, d_i, D_MODEL), d_i ** -0.5),
        "kv_norm": gain(ks[10], (D_MODEL,)),
        "w_kv": nrm(ks[11], (D_MODEL, DA_HEADS * 2 * DA_HEAD_DIM + DA_HEADS * DA_VDIM), D_MODEL ** -0.5),
        "diff_w_q": nrm(ks[12], (N_B_LAYERS, D_MODEL, DA_HEADS * 2 * DA_HEAD_DIM), D_MODEL ** -0.5),
        "diff_lambda": nrm(ks[13], (N_B_LAYERS, 4, DA_HEAD_DIM), 0.1),
        "diff_subln": gain(ks[14], (N_B_LAYERS, DA_VDIM)),
        "diff_w_out": nrm(ks[15], (N_B_LAYERS, DA_HEADS * DA_VDIM, D_MODEL), (DA_HEADS * DA_VDIM) ** -0.5),
        "ple_norm": gain(ks[16], (DEPTH, D_MODEL)),
        "ple_w_gate": nrm(ks[17], (DEPTH, D_MODEL, D_MODEL), D_MODEL ** -0.5),
        "ple_w_proj": nrm(ks[18], (DEPTH, PLE_DIM, D_MODEL), PLE_DIM ** -0.5),
        "final_norm": gain(ks[19], (D_MODEL,)),
    }


def reference(x, p, ffn_norm, ffn_w_gate_up, ffn_w_down, mix_norm, hgrn_w_in, hgrn_lower_bounds,
              hgrn_out_norm, hgrn_w_out, kv_norm, w_kv, diff_w_q, diff_lambda, diff_subln, diff_w_out,
              ple_norm, ple_w_gate, ple_w_proj, final_norm):
    positions = jnp.arange(x.shape[1], dtype=jnp.int32)
    lb_all = jnp.cumsum(jax.nn.softmax(hgrn_lower_bounds.astype(jnp.float32), axis=0), axis=0)
    h = x
    k_shared = None
    v_shared = None
    for i in range(DEPTH):
        h = h + 0.5 * swiglu_ffn(rms_norm(h, ffn_norm[i, 0]), ffn_w_gate_up[i, 0], ffn_w_down[i, 0])
        hn = rms_norm(h, mix_norm[i])
        if i < N_A_LAYERS:
            mix = hgrn2_mixer(hn, hgrn_w_in[i], lb_all[i], hgrn_out_norm[i], hgrn_w_out[i])
        else:
            j = i - N_A_LAYERS
            lambda_init = 0.8 - 0.6 * math_exp(-0.3 * i)
            mix = diff_attention(hn, k_shared, v_shared, diff_w_q[j], diff_lambda[j], diff_subln[j],
                                 diff_w_out[j], lambda_init, positions)
        h = h + mix
        h = h + 0.5 * swiglu_ffn(rms_norm(h, ffn_norm[i, 1]), ffn_w_gate_up[i, 1], ffn_w_down[i, 1])
        gate = jax.nn.sigmoid(rms_norm(h, ple_norm[i]) @ ple_w_gate[i])
        h = h + gate * (p[i] @ ple_w_proj[i])
        if i == N_A_LAYERS - 1:
            k_shared, v_shared = shared_kv(h, kv_norm, w_kv, positions)
    return rms_norm(h, final_norm)


def math_exp(t):
    return float(np.exp(t))
```

```python
import functools
import math

import jax
import jax.numpy as jnp
from jax import lax
from jax.experimental import pallas as pl
from jax.experimental.pallas import tpu as pltpu

F32 = jnp.float32
BF16 = jnp.bfloat16

NORM_EPS = 1e-6
ROPE_THETA = 10000.0
HEAD = 128
LANES = 128
SUBLANES = 8
HG_CHUNK = 32
HG_SUB = 256
NEG = -1e30

VMEM_LIMIT = 56 * 1024 * 1024


def _params(sem):
    return pltpu.CompilerParams(dimension_semantics=sem, vmem_limit_bytes=VMEM_LIMIT)


def _rms(x, g):
    ms = jnp.mean(x * x, axis=-1, keepdims=True)
    return x * lax.rsqrt(ms + NORM_EPS) * g


def _sigmoid(x):
    return 1.0 / (1.0 + jnp.exp(-x))


def _ffn_body(x_ref, g_ref, wg_ref, wu_ref, wd_ref, o_ref, xn_ref):
    f = pl.program_id(1)
    last = pl.num_programs(1) - 1

    @pl.when(f == 0)
    def _():
        xn_ref[...] = _rms(x_ref[...], g_ref[...]).astype(BF16)

    xn = xn_ref[...]
    gate = jnp.dot(xn, wg_ref[...], preferred_element_type=F32)
    up = jnp.dot(xn, wu_ref[...], preferred_element_type=F32)
    act = (gate * _sigmoid(gate) * up).astype(BF16)
    part = jnp.dot(act, wd_ref[...], preferred_element_type=F32)

    @pl.when(f == 0)
    def _():
        o_ref[...] = part

    @pl.when(jnp.logical_and(f > 0, f < last))
    def _():
        o_ref[...] += part

    @pl.when(f == last)
    def _():
        o_ref[...] = x_ref[...] + 0.5 * (o_ref[...] + part)


def _ffn(x, gain, wgu, wd, li, lj, *, tm, tf):
    n, d = x.shape
    ff = wd.shape[2]
    nf = ff // tf
    assert n % tm == 0 and ff % tf == 0 and nf >= 2
    return pl.pallas_call(
        _ffn_body,
        grid=(n // tm, nf),
        in_specs=[
            pl.BlockSpec((tm, d), lambda i, f: (i, 0)),
            pl.BlockSpec((1, d), lambda i, f: (0, 0)),
            pl.BlockSpec((None, None, d, tf), lambda i, f: (li, lj, 0, f)),
            pl.BlockSpec((None, None, d, tf), lambda i, f: (li, lj, 0, f + nf)),
            pl.BlockSpec((None, None, tf, d), lambda i, f: (li, lj, f, 0)),
        ],
        out_specs=pl.BlockSpec((tm, d), lambda i, f: (i, 0)),
        out_shape=jax.ShapeDtypeStruct((n, d), F32),
        scratch_shapes=[pltpu.VMEM((tm, d), BF16)],
        compiler_params=_params(("parallel", "arbitrary")),
        name="ffn",
    )(x, gain, wgu, wgu, wd)


def _rope_store(y, cos, sin, o_ref, scale):
    for g in range(y.shape[1] // HEAD):
        yg = y[:, g * HEAD:(g + 1) * HEAD]
        r = yg * cos + pltpu.roll(yg, HEAD // 2, axis=1) * sin
        if scale != 1.0:
            r = r * scale
        o_ref[:, g * HEAD:(g + 1) * HEAD] = r.astype(o_ref.dtype)


def _norm_mm_body(x_ref, g_ref, w_ref, o_ref, xn_ref):
    @pl.when(pl.program_id(1) == 0)
    def _():
        xn_ref[...] = _rms(x_ref[...], g_ref[...]).astype(BF16)

    o_ref[...] = jnp.dot(xn_ref[...], w_ref[...], preferred_element_type=F32).astype(o_ref.dtype)


def _norm_mm_rope_body(x_ref, g_ref, w_ref, cos_ref, sin_ref, o_ref, xn_ref, *, rope_tiles, scale):
    j = pl.program_id(1)

    @pl.when(j == 0)
    def _():
        xn_ref[...] = _rms(x_ref[...], g_ref[...]).astype(BF16)

    y = jnp.dot(xn_ref[...], w_ref[...], preferred_element_type=F32)

    @pl.when(j < rope_tiles)
    def _():
        _rope_store(y, cos_ref[...], sin_ref[...], o_ref, scale)

    @pl.when(j >= rope_tiles)
    def _():
        o_ref[...] = y.astype(o_ref.dtype)


def _norm_mm(x, gain, w, w_index, *, tm, tn, out_dtype, rope=None, seq=None, rope_cols=0, scale=1.0):
    n, d = x.shape
    nout = w.shape[-1]
    assert n % tm == 0 and nout % tn == 0
    nlead = w.ndim - 2
    w_spec = pl.BlockSpec((None,) * nlead + (d, tn), lambda i, j: tuple(w_index) + (0, j))
    in_specs = [pl.BlockSpec((tm, d), lambda i, j: (i, 0)),
                pl.BlockSpec((1, d), lambda i, j: (0, 0)),
                w_spec]
    args = [x, gain, w]
    if rope is None:
        body = _norm_mm_body
    else:
        assert seq % tm == 0 and rope_cols % tn == 0
        nseq = seq // tm
        tab_spec = pl.BlockSpec((tm, HEAD), lambda i, j: (i % nseq, 0))
        in_specs += [tab_spec, tab_spec]
        args += list(rope)
        body = functools.partial(_norm_mm_rope_body, rope_tiles=rope_cols // tn, scale=scale)
    return pl.pallas_call(
        body,
        grid=(n // tm, nout // tn),
        in_specs=in_specs,
        out_specs=pl.BlockSpec((tm, tn), lambda i, j: (i, j)),
        out_shape=jax.ShapeDtypeStruct((n, nout), out_dtype),
        scratch_shapes=[pltpu.VMEM((tm, d), BF16)],
        compiler_params=_params(("parallel", "arbitrary")),
        name="norm_mm",
    )(*args)


def _mm_resid_body(x_ref, w_ref, r_ref, o_ref):
    o_ref[...] = r_ref[...] + jnp.dot(x_ref[...], w_ref[...], preferred_element_type=F32)


def _mm_resid(x, w, w_index, resid, *, tm, tn):
    n, k = x.shape
    nout = w.shape[-1]
    assert n % tm == 0 and nout % tn == 0
    nlead = w.ndim - 2
    return pl.pallas_call(
        _mm_resid_body,
        grid=(n // tm, nout // tn),
        in_specs=[pl.BlockSpec((tm, k), lambda i, j: (i, 0)),
                  pl.BlockSpec((None,) * nlead + (k, tn), lambda i, j: tuple(w_index) + (0, j)),
                  pl.BlockSpec((tm, tn), lambda i, j: (i, j))],
        out_specs=pl.BlockSpec((tm, tn), lambda i, j: (i, j)),
        out_shape=jax.ShapeDtypeStruct((n, nout), F32),
        compiler_params=_params(("parallel", "arbitrary")),
        name="mm_resid",
    )(x, w, resid)


def _ple_body(x_ref, g_ref, wg_ref, p_ref, wp_ref, fg_ref, o_ref, *, tn, final):
    x = x_ref[...]
    xn = _rms(x, g_ref[...]).astype(BF16)
    pb = p_ref[...].astype(BF16)
    d = x.shape[1]
    for c in range(d // tn):
        sl = slice(c * tn, (c + 1) * tn)
        gate = jnp.dot(xn, wg_ref[:, sl], preferred_element_type=F32)
        proj = jnp.dot(pb, wp_ref[:, sl], preferred_element_type=F32)
        o_ref[:, sl] = x_ref[:, sl] + _sigmoid(gate) * proj
    if final:
        o_ref[...] = _rms(o_ref[...], fg_ref[...])


def _ple(x, gain, wg, p, wp, layer, final_gain, *, tm, tn, final):
    n, d = x.shape
    pd = p.shape[-1]
    rows = n // tm
    assert n % tm == 0 and d % tn == 0
    return pl.pallas_call(
        functools.partial(_ple_body, tn=tn, final=final),
        grid=(rows,),
        in_specs=[pl.BlockSpec((tm, d), lambda i: (i, 0)),
                  pl.BlockSpec((1, d), lambda i: (0, 0)),
                  pl.BlockSpec((None, d, d), lambda i: (layer, 0, 0)),
                  pl.BlockSpec((None, tm, pd), lambda i: (layer, i, 0)),
                  pl.BlockSpec((None, pd, d), lambda i: (layer, 0, 0)),
                  pl.BlockSpec((1, d), lambda i: (0, 0))],
        out_specs=pl.BlockSpec((tm, d), lambda i: (i, 0)),
        out_shape=jax.ShapeDtypeStruct((n, d), F32),
        compiler_params=_params(("parallel",)),
        name="ple",
    )(x, gain, wg, p, wp, final_gain)


def _rope_table_body(cos_ref, sin_ref):
    shape = cos_ref.shape
    pos = lax.broadcasted_iota(jnp.int32, shape, 0).astype(F32)
    lane = lax.broadcasted_iota(jnp.int32, shape, 1)
    half = HEAD // 2
    k = jnp.bitwise_and(lane, half - 1).astype(F32)
    inv_freq = jnp.exp(k * (-math.log(ROPE_THETA) / half))
    ang = pos * inv_freq
    cos_ref[...] = jnp.cos(ang)
    s = jnp.sin(ang)
    sin_ref[...] = jnp.where(lane < half, -s, s)


def _rope_tables(seq):
    return pl.pallas_call(
        _rope_table_body,
        out_shape=(jax.ShapeDtypeStruct((seq, HEAD), F32), jax.ShapeDtypeStruct((seq, HEAD), F32)),
        name="rope_tables",
    )()


def _split3(x):
    hi = x.astype(BF16)
    r1 = x - hi.astype(F32)
    mid = r1.astype(BF16)
    lo = (r1 - mid.astype(F32)).astype(BF16)
    return hi, mid, lo


def _hgrn_sub(q, z, v, lb, state):
    rows = q.shape[0]
    nch = rows // HG_CHUNK
    nblk = rows // SUBLANES
    bpc = HG_CHUNK // SUBLANES

    t = jnp.exp(-jnp.abs(z))
    r = 1.0 / (1.0 + t)
    pos = z >= 0
    sig = jnp.where(pos, r, t * r)
    nsig = jnp.where(pos, t * r, r)
    logf = jnp.log(lb + (1.0 - lb) * sig)
    kk = (1.0 - lb) * nsig
    qs = q * (HEAD ** -0.5)

    ri = lax.broadcasted_iota(jnp.int32, (rows, rows), 0)
    ci = lax.broadcasted_iota(jnp.int32, (rows, rows), 1)
    same_chunk = (ri // HG_CHUNK) == (ci // HG_CHUNK)
    tril = jnp.where(same_chunk, jnp.where(ci <= ri, 1.0, 0.0), 0.0).astype(BF16)
    hi, mid, lo = _split3(logf)
    b = (jnp.dot(tril, hi, preferred_element_type=F32)
         + jnp.dot(tril, mid, preferred_element_type=F32)
         + jnp.dot(tril, lo, preferred_element_type=F32))

    vb = v.astype(BF16)
    b3 = b.reshape(nch, HG_CHUNK, HEAD)
    b_last = b3[:, HG_CHUNK - 1:HG_CHUNK, :]
    q_in = (qs * jnp.exp(b)).astype(BF16)
    k_end = (kk.reshape(nch, HG_CHUNK, HEAD) * jnp.exp(b_last - b3)).reshape(rows, HEAD).astype(BF16)
    decay_cols = jnp.exp(b_last).reshape(nch, HEAD).T

    tloc = lax.broadcasted_iota(jnp.int32, (nch, HG_CHUNK, HEAD), 1)
    qs3 = qs.reshape(nch, HG_CHUNK, HEAD)
    b8 = b.reshape(nblk, SUBLANES, HEAD)
    kk8 = kk.reshape(nblk, SUBLANES, HEAD)
    k_blk = (kk8 * jnp.exp(b8[:, SUBLANES - 1:SUBLANES, :] - b8)).reshape(nch, HG_CHUNK, HEAD)
    q_groups, k_groups = [], []
    for jb in range(bpc - 1):
        ref = b3[:, jb * SUBLANES + SUBLANES - 1:jb * SUBLANES + SUBLANES, :]
        first = (jb + 1) * SUBLANES
        qg = qs3 * jnp.exp(jnp.where(tloc >= first, b3 - ref, NEG))
        kg = jnp.where((tloc >= jb * SUBLANES) & (tloc < first), k_blk, 0.0)
        q_groups.append(qg.reshape(rows, HEAD).astype(BF16))
        k_groups.append(kg.reshape(rows, HEAD).astype(BF16))
    q_cat = jnp.concatenate(q_groups, axis=1)
    k_cat = jnp.concatenate(k_groups, axis=1)
    a_off = lax.dot_general(q_cat, k_cat, (((1,), (1,)), ((), ())), preferred_element_type=F32)
    off_mask = same_chunk & ((ri // SUBLANES) > (ci // SUBLANES))
    a_off = jnp.where(off_mask, a_off, 0.0).astype(BF16)
    o = jnp.dot(a_off, vb, preferred_element_type=F32)

    sub = lax.broadcasted_iota(jnp.int32, (nblk, SUBLANES, HEAD), 1)
    q8 = qs.reshape(nblk, SUBLANES, HEAD)
    v8 = v.reshape(nblk, SUBLANES, HEAD)
    od = jnp.zeros((nblk, SUBLANES, HEAD), F32)
    for j in range(SUBLANES):
        bj = b8[:, j:j + 1, :]
        qj = q8[:, j:j + 1, :]
        e = jnp.exp(jnp.where(sub <= j, bj - b8, NEG))
        a = jnp.sum(e * kk8 * qj, axis=2, keepdims=True)
        oj = jnp.sum(a * v8, axis=1, keepdims=True)
        od = jnp.where(sub == j, oj, od)
    o = o + od.reshape(rows, HEAD)

    outs = []
    for c in range(nch):
        sl = slice(c * HG_CHUNK, (c + 1) * HG_CHUNK)
        outs.append(jnp.dot(q_in[sl], state.astype(BF16), preferred_element_type=F32))
        upd = lax.dot_general(k_end[sl], vb[sl], (((0,), (0,)), ((), ())), preferred_element_type=F32)
        state = state * decay_cols[:, c:c + 1] + upd
    o = o + jnp.concatenate(outs, axis=0)
    return o, state


def _hgrn_body(q_ref, z_ref, v_ref, g_ref, lbp_ref, gain_ref, o_ref, state_ref, *, layer):
    @pl.when(pl.program_id(2) == 0)
    def _():
        state_ref[...] = jnp.zeros_like(state_ref)

    lbp = lbp_ref[...]
    e = jnp.exp(lbp - jnp.max(lbp, axis=0, keepdims=True))
    lb = jnp.sum(e[:layer + 1], axis=0, keepdims=True) / jnp.sum(e, axis=0, keepdims=True)

    rows = q_ref.shape[0]
    state = state_ref[...]
    for s in range(rows // HG_SUB):
        sl = slice(s * HG_SUB, (s + 1) * HG_SUB)
        o, state = _hgrn_sub(q_ref[sl, :], z_ref[sl, :], v_ref[sl, :], lb, state)
        on = _rms(o, gain_ref[...])
        g = g_ref[sl, :]
        o_ref[sl, :] = (on * (g * _sigmoid(g))).astype(o_ref.dtype)
    state_ref[...] = state


def _hgrn(proj, lbp, gain, layer, *, batch, seq, heads, tr):
    n = proj.shape[0]
    assert seq % tr == 0 and tr % HG_SUB == 0
    ns = seq // tr
    row = lambda b, h, s: b * ns + s
    sec = lambda k: pl.BlockSpec((tr, HEAD), lambda b, h, s: (row(b, h, s), k * heads + h))
    return pl.pallas_call(
        functools.partial(_hgrn_body, layer=layer),
        grid=(batch, heads, ns),
        in_specs=[sec(0), sec(1), sec(2), sec(3),
                  pl.BlockSpec((lbp.shape[0], HEAD), lambda b, h, s: (0, h)),
                  pl.BlockSpec((1, HEAD), lambda b, h, s: (0, 0))],
        out_specs=pl.BlockSpec((tr, HEAD), lambda b, h, s: (row(b, h, s), h)),
        out_shape=jax.ShapeDtypeStruct((n, heads * HEAD), BF16),
        scratch_shapes=[pltpu.VMEM((HEAD, HEAD), F32)],
        compiler_params=_params(("parallel", "parallel", "arbitrary")),
        name="hgrn",
    )(proj, proj, proj, proj, lbp, gain)


def _attn_body(lam_ref, sub_ref, q_ref, k_ref, v_ref, o_ref, acc_ref, m_ref, l_ref, *, tq, lambda_init):
    qi = pl.program_id(2)
    q = q_ref[...]

    m_ref[...] = jnp.full(m_ref.shape, NEG, F32)
    l_ref[...] = jnp.zeros(l_ref.shape, F32)
    acc_ref[...] = jnp.zeros(acc_ref.shape, F32)

    def tile(j, masked):
        start = pl.multiple_of(j * tq, tq)
        kt = k_ref[pl.ds(start, tq), :]
        vt = v_ref[pl.ds(start, tq), :]
        for c in range(2):
            cs = slice(c * HEAD, (c + 1) * HEAD)
            s = lax.dot_general(q[:, cs], kt[:, cs], (((1,), (1,)), ((), ())), preferred_element_type=F32)
            if masked:
                ri = lax.broadcasted_iota(jnp.int32, s.shape, 0)
                ci = lax.broadcasted_iota(jnp.int32, s.shape, 1)
                s = jnp.where(ci <= ri, s, NEG)
            m_old = m_ref[c]
            m_new = jnp.maximum(m_old, jnp.max(s, axis=-1, keepdims=True))
            alpha = jnp.exp(m_old - m_new)
            p = jnp.exp(s - m_new)
            l_ref[c] = alpha * l_ref[c] + jnp.sum(p, axis=-1, keepdims=True)
            acc_ref[c] = alpha * acc_ref[c] + jnp.dot(p.astype(BF16), vt, preferred_element_type=F32)
            m_ref[c] = m_new

    def body(j, carry):
        tile(j, False)
        return carry

    lax.fori_loop(0, qi, body, 0)
    tile(qi, True)

    lam = lam_ref[...]
    lam_full = (jnp.exp(jnp.sum(lam[0:1] * lam[1:2], axis=-1, keepdims=True))
                - jnp.exp(jnp.sum(lam[2:3] * lam[3:4], axis=-1, keepdims=True)) + lambda_init)
    o = acc_ref[0] * (1.0 / l_ref[0]) - lam_full * (acc_ref[1] * (1.0 / l_ref[1]))
    o_ref[...] = (_rms(o, sub_ref[...]) * (1.0 - lambda_init)).astype(o_ref.dtype)


def _attention(q, kv, lam, subln, *, batch, seq, heads, tq, lambda_init):
    n = q.shape[0]
    vd = 2 * HEAD
    assert seq % tq == 0
    nq = seq // tq
    return pl.pallas_call(
        functools.partial(_attn_body, tq=tq, lambda_init=lambda_init),
        grid=(batch, heads, nq),
        in_specs=[pl.BlockSpec((4, HEAD), lambda b, h, i: (0, 0)),
                  pl.BlockSpec((1, vd), lambda b, h, i: (0, 0)),
                  pl.BlockSpec((tq, vd), lambda b, h, i: (b * nq + i, h)),
                  pl.BlockSpec((seq, vd), lambda b, h, i: (b, h)),
                  pl.BlockSpec((seq, vd), lambda b, h, i: (b, heads + h))],
        out_specs=pl.BlockSpec((tq, vd), lambda b, h, i: (b * nq + i, h)),
        out_shape=jax.ShapeDtypeStruct((n, heads * vd), BF16),
        scratch_shapes=[pltpu.VMEM((2, tq, vd), F32), pltpu.VMEM((2, tq, 1), F32), pltpu.VMEM((2, tq, 1), F32)],
        compiler_params=_params(("parallel", "parallel", "arbitrary")),
        name="diff_attn",
    )(lam, subln, q, kv, kv)


def kernel(x, p, ffn_norm, ffn_w_gate_up, ffn_w_down, mix_norm, hgrn_w_in, hgrn_lower_bounds, hgrn_out_norm, hgrn_w_out, kv_norm, w_kv, diff_w_q, diff_lambda, diff_subln, diff_w_out, ple_norm, ple_w_gate, ple_w_proj, final_norm):
    batch, seq, d = x.shape
    depth = p.shape[0]
    n_a = hgrn_w_in.shape[0]
    n = batch * seq
    hg_heads = d // HEAD
    da_heads = d // (2 * HEAD)
    k_cols = da_heads * 2 * HEAD

    tm = min(512, seq)
    tq = min(512, seq)

    wgu = ffn_w_gate_up.astype(BF16)
    wdn = ffn_w_down.astype(BF16)
    w_in = hgrn_w_in.astype(BF16)
    w_ho = hgrn_w_out.astype(BF16)
    w_kvb = w_kv.astype(BF16)
    w_q = diff_w_q.astype(BF16)
    w_do = diff_w_out.astype(BF16)
    w_pg = ple_w_gate.astype(BF16)
    w_pp = ple_w_proj.astype(BF16)
    p2 = p.reshape(depth, n, p.shape[-1])

    rope = _rope_tables(seq)
    h = x.reshape(n, d)
    kv = None
    for i in range(depth):
        h = _ffn(h, ffn_norm[i, 0].reshape(1, d), wgu, wdn, i, 0, tm=tm, tf=512)
        if i < n_a:
            proj = _norm_mm(h, mix_norm[i].reshape(1, d), w_in, (i,), tm=tm, tn=1024, out_dtype=F32)
            og = _hgrn(proj, hgrn_lower_bounds, hgrn_out_norm[i].reshape(1, HEAD), i,
                       batch=batch, seq=seq, heads=hg_heads, tr=min(512, seq))
            h = _mm_resid(og, w_ho, (i,), h, tm=tm, tn=1024)
        else:
            j = i - n_a
            lambda_init = 0.8 - 0.6 * math.exp(-0.3 * i)
            q = _norm_mm(h, mix_norm[i].reshape(1, d), w_q, (j,), tm=tm, tn=1024, out_dtype=BF16,
                         rope=rope, seq=seq, rope_cols=k_cols, scale=HEAD ** -0.5)
            oa = _attention(q, kv, diff_lambda[j], diff_subln[j].reshape(1, 2 * HEAD),
                            batch=batch, seq=seq, heads=da_heads, tq=tq, lambda_init=lambda_init)
            h = _mm_resid(oa, w_do, (j,), h, tm=tm, tn=1024)
        h = _ffn(h, ffn_norm[i, 1].reshape(1, d), wgu, wdn, i, 1, tm=tm, tf=512)
        last = i == depth - 1
        h = _ple(h, ple_norm[i].reshape(1, d), w_pg, p2, w_pp, i, final_norm.reshape(1, d),
                 tm=tm, tn=512, final=last)
        if i == n_a - 1:
            kv = _norm_mm(h, kv_norm.reshape(1, d), w_kvb, (), tm=tm, tn=1024, out_dtype=BF16,
                          rope=rope, seq=seq, rope_cols=k_cols)
    return h.reshape(batch, seq, d)
```

```python
import functools
import math

import jax
import jax.numpy as jnp
from jax import lax
from jax.experimental import pallas as pl
from jax.experimental.pallas import tpu as pltpu

F32 = jnp.float32
BF16 = jnp.bfloat16

NORM_EPS = 1e-6
ROPE_THETA = 10000.0
HEAD = 128
LANES = 128
SUBLANES = 8
HG_CHUNK = 32
HG_SUB = 256
NEG = -1e30

VMEM_LIMIT = 56 * 1024 * 1024


def _params(sem):
    return pltpu.CompilerParams(dimension_semantics=sem, vmem_limit_bytes=VMEM_LIMIT)


def _rms(x, g):
    ms = jnp.mean(x * x, axis=-1, keepdims=True)
    return x * lax.rsqrt(ms + NORM_EPS) * g


def _sigmoid(x):
    return 1.0 / (1.0 + jnp.exp(-x))


def _ffn_body(x_ref, g_ref, wg_ref, wu_ref, wd_ref, o_ref, xn_ref, act_ref):
    f = pl.program_id(1)
    nf = pl.num_programs(1) - 1

    @pl.when(f == 0)
    def _():
        x = x_ref[...]
        xn_ref[...] = _rms(x, g_ref[...]).astype(BF16)
        act_ref[...] = jnp.zeros_like(act_ref)
        o_ref[...] = x

    @pl.when(f < nf)
    def _():
        part = jnp.dot(act_ref[...], wd_ref[...], preferred_element_type=F32)
        xn = xn_ref[...]
        gate = jnp.dot(xn, wg_ref[...], preferred_element_type=F32)
        up = jnp.dot(xn, wu_ref[...], preferred_element_type=F32)
        o_ref[...] += 0.5 * part
        act_ref[...] = (gate * _sigmoid(gate) * up).astype(BF16)

    @pl.when(f == nf)
    def _():
        o_ref[...] += 0.5 * jnp.dot(act_ref[...], wd_ref[...], preferred_element_type=F32)


def _ffn(x, gain, wgu, wd, li, lj, *, tm, tf):
    n, d = x.shape
    ff = wd.shape[2]
    nf = ff // tf
    assert n % tm == 0 and ff % tf == 0
    return pl.pallas_call(
        _ffn_body,
        grid=(n // tm, nf + 1),
        in_specs=[
            pl.BlockSpec((tm, d), lambda i, f: (i, 0)),
            pl.BlockSpec((1, d), lambda i, f: (0, 0)),
            pl.BlockSpec((None, None, d, tf), lambda i, f: (li, lj, 0, jnp.minimum(f, nf - 1))),
            pl.BlockSpec((None, None, d, tf), lambda i, f: (li, lj, 0, jnp.minimum(f, nf - 1) + nf)),
            pl.BlockSpec((None, None, tf, d), lambda i, f: (li, lj, jnp.maximum(f - 1, 0), 0)),
        ],
        out_specs=pl.BlockSpec((tm, d), lambda i, f: (i, 0)),
        out_shape=jax.ShapeDtypeStruct((n, d), F32),
        scratch_shapes=[pltpu.VMEM((tm, d), BF16), pltpu.VMEM((tm, tf), BF16)],
        compiler_params=_params(("parallel", "arbitrary")),
        name="ffn",
    )(x, gain, wgu, wgu, wd)


def _rope_store(y, cos, sin, o_ref, scale):
    for g in range(y.shape[1] // HEAD):
        yg = y[:, g * HEAD:(g + 1) * HEAD]
        r = yg * cos + pltpu.roll(yg, HEAD // 2, axis=1) * sin
        if scale != 1.0:
            r = r * scale
        o_ref[:, g * HEAD:(g + 1) * HEAD] = r.astype(o_ref.dtype)


def _norm_mm_body(x_ref, g_ref, w_ref, o_ref, xn_ref):
    @pl.when(pl.program_id(1) == 0)
    def _():
        xn_ref[...] = _rms(x_ref[...], g_ref[...]).astype(BF16)

    o_ref[...] = jnp.dot(xn_ref[...], w_ref[...], preferred_element_type=F32).astype(o_ref.dtype)


def _norm_mm_rope_body(x_ref, g_ref, w_ref, cos_ref, sin_ref, o_ref, xn_ref, *, rope_tiles, scale):
    j = pl.program_id(1)

    @pl.when(j == 0)
    def _():
        xn_ref[...] = _rms(x_ref[...], g_ref[...]).astype(BF16)

    y = jnp.dot(xn_ref[...], w_ref[...], preferred_element_type=F32)

    @pl.when(j < rope_tiles)
    def _():
        _rope_store(y, cos_ref[...], sin_ref[...], o_ref, scale)

    @pl.when(j >= rope_tiles)
    def _():
        o_ref[...] = y.astype(o_ref.dtype)


def _norm_mm(x, gain, w, w_index, *, tm, tn, out_dtype, rope=None, seq=None, rope_cols=0, scale=1.0):
    n, d = x.shape
    nout = w.shape[-1]
    assert n % tm == 0 and nout % tn == 0
    nlead = w.ndim - 2
    w_spec = pl.BlockSpec((None,) * nlead + (d, tn), lambda i, j: tuple(w_index) + (0, j))
    in_specs = [pl.BlockSpec((tm, d), lambda i, j: (i, 0)),
                pl.BlockSpec((1, d), lambda i, j: (0, 0)),
                w_spec]
    args = [x, gain, w]
    if rope is None:
        body = _norm_mm_body
    else:
        assert seq % tm == 0 and rope_cols % tn == 0
        nseq = seq // tm
        tab_spec = pl.BlockSpec((tm, HEAD), lambda i, j: (i % nseq, 0))
        in_specs += [tab_spec, tab_spec]
        args += list(rope)
        body = functools.partial(_norm_mm_rope_body, rope_tiles=rope_cols // tn, scale=scale)
    return pl.pallas_call(
        body,
        grid=(n // tm, nout // tn),
        in_specs=in_specs,
        out_specs=pl.BlockSpec((tm, tn), lambda i, j: (i, j)),
        out_shape=jax.ShapeDtypeStruct((n, nout), out_dtype),
        scratch_shapes=[pltpu.VMEM((tm, d), BF16)],
        compiler_params=_params(("parallel", "arbitrary")),
        name="norm_mm",
    )(*args)


def _mm_resid_body(x_ref, w_ref, r_ref, o_ref):
    o_ref[...] = r_ref[...] + jnp.dot(x_ref[...], w_ref[...], preferred_element_type=F32)


def _mm_resid(x, w, w_index, resid, *, tm, tn):
    n, k = x.shape
    nout = w.shape[-1]
    assert n % tm == 0 and nout % tn == 0
    nlead = w.ndim - 2
    return pl.pallas_call(
        _mm_resid_body,
        grid=(n // tm, nout // tn),
        in_specs=[pl.BlockSpec((tm, k), lambda i, j: (i, 0)),
                  pl.BlockSpec((None,) * nlead + (k, tn), lambda i, j: tuple(w_index) + (0, j)),
                  pl.BlockSpec((tm, tn), lambda i, j: (i, j))],
        out_specs=pl.BlockSpec((tm, tn), lambda i, j: (i, j)),
        out_shape=jax.ShapeDtypeStruct((n, nout), F32),
        compiler_params=_params(("parallel", "arbitrary")),
        name="mm_resid",
    )(x, w, resid)


def _ple_body(x_ref, g_ref, wg_ref, p_ref, wp_ref, fg_ref, o_ref, *, tn, final):
    x = x_ref[...]
    xn = _rms(x, g_ref[...]).astype(BF16)
    pb = p_ref[...].astype(BF16)
    d = x.shape[1]
    for c in range(d // tn):
        sl = slice(c * tn, (c + 1) * tn)
        gate = jnp.dot(xn, wg_ref[:, sl], preferred_element_type=F32)
        proj = jnp.dot(pb, wp_ref[:, sl], preferred_element_type=F32)
        o_ref[:, sl] = x_ref[:, sl] + _sigmoid(gate) * proj
    if final:
        o_ref[...] = _rms(o_ref[...], fg_ref[...])


def _ple(x, gain, wg, p, wp, layer, final_gain, *, tm, tn, final):
    n, d = x.shape
    pd = p.shape[-1]
    rows = n // tm
    assert n % tm == 0 and d % tn == 0
    return pl.pallas_call(
        functools.partial(_ple_body, tn=tn, final=final),
        grid=(rows,),
        in_specs=[pl.BlockSpec((tm, d), lambda i: (i, 0)),
                  pl.BlockSpec((1, d), lambda i: (0, 0)),
                  pl.BlockSpec((None, d, d), lambda i: (layer, 0, 0)),
                  pl.BlockSpec((None, tm, pd), lambda i: (layer, i, 0)),
                  pl.BlockSpec((None, pd, d), lambda i: (layer, 0, 0)),
                  pl.BlockSpec((1, d), lambda i: (0, 0))],
        out_specs=pl.BlockSpec((tm, d), lambda i: (i, 0)),
        out_shape=jax.ShapeDtypeStruct((n, d), F32),
        compiler_params=_params(("parallel",)),
        name="ple",
    )(x, gain, wg, p, wp, final_gain)


def _rope_table_body(cos_ref, sin_ref):
    shape = cos_ref.shape
    pos = lax.broadcasted_iota(jnp.int32, shape, 0).astype(F32)
    lane = lax.broadcasted_iota(jnp.int32, shape, 1)
    half = HEAD // 2
    k = jnp.bitwise_and(lane, half - 1).astype(F32)
    inv_freq = jnp.exp(k * (-math.log(ROPE_THETA) / half))
    ang = pos * inv_freq
    cos_ref[...] = jnp.cos(ang)
    s = jnp.sin(ang)
    sin_ref[...] = jnp.where(lane < half, -s, s)


def _rope_tables(seq):
    return pl.pallas_call(
        _rope_table_body,
        out_shape=(jax.ShapeDtypeStruct((seq, HEAD), F32), jax.ShapeDtypeStruct((seq, HEAD), F32)),
        name="rope_tables",
    )()


def _split3(x):
    hi = x.astype(BF16)
    r1 = x - hi.astype(F32)
    mid = r1.astype(BF16)
    lo = (r1 - mid.astype(F32)).astype(BF16)
    return hi, mid, lo


def _hgrn_sub(q, z, v, lb, state):
    rows = q.shape[0]
    nch = rows // HG_CHUNK
    nblk = rows // SUBLANES
    bpc = HG_CHUNK // SUBLANES

    t = jnp.exp(-jnp.abs(z))
    r = 1.0 / (1.0 + t)
    pos = z >= 0
    sig = jnp.where(pos, r, t * r)
    nsig = jnp.where(pos, t * r, r)
    logf = jnp.log(lb + (1.0 - lb) * sig)
    kk = (1.0 - lb) * nsig
    qs = q * (HEAD ** -0.5)

    ri = lax.broadcasted_iota(jnp.int32, (rows, rows), 0)
    ci = lax.broadcasted_iota(jnp.int32, (rows, rows), 1)
    same_chunk = (ri // HG_CHUNK) == (ci // HG_CHUNK)
    tril = jnp.where(same_chunk, jnp.where(ci <= ri, 1.0, 0.0), 0.0).astype(BF16)
    hi, mid, lo = _split3(logf)
    b = (jnp.dot(tril, hi, preferred_element_type=F32)
         + jnp.dot(tril, mid, preferred_element_type=F32)
         + jnp.dot(tril, lo, preferred_element_type=F32))

    vb = v.astype(BF16)
    b3 = b.reshape(nch, HG_CHUNK, HEAD)
    b_last = b3[:, HG_CHUNK - 1:HG_CHUNK, :]
    q_in = (qs * jnp.exp(b)).astype(BF16)
    k_end = (kk.reshape(nch, HG_CHUNK, HEAD) * jnp.exp(b_last - b3)).reshape(rows, HEAD).astype(BF16)
    decay_cols = jnp.exp(b_last).reshape(nch, HEAD).T

    tloc = lax.broadcasted_iota(jnp.int32, (nch, HG_CHUNK, HEAD), 1)
    qs3 = qs.reshape(nch, HG_CHUNK, HEAD)
    b8 = b.reshape(nblk, SUBLANES, HEAD)
    kk8 = kk.reshape(nblk, SUBLANES, HEAD)
    k_blk = (kk8 * jnp.exp(b8[:, SUBLANES - 1:SUBLANES, :] - b8)).reshape(nch, HG_CHUNK, HEAD)
    q_groups, k_groups = [], []
    for jb in range(bpc - 1):
        ref = b3[:, jb * SUBLANES + SUBLANES - 1:jb * SUBLANES + SUBLANES, :]
        first = (jb + 1) * SUBLANES
        qg = qs3 * jnp.exp(jnp.where(tloc >= first, b3 - ref, NEG))
        kg = jnp.where((tloc >= jb * SUBLANES) & (tloc < first), k_blk, 0.0)
        q_groups.append(qg.reshape(rows, HEAD).astype(BF16))
        k_groups.append(kg.reshape(rows, HEAD).astype(BF16))
    q_cat = jnp.concatenate(q_groups, axis=1)
    k_cat = jnp.concatenate(k_groups, axis=1)
    a_off = lax.dot_general(q_cat, k_cat, (((1,), (1,)), ((), ())), preferred_element_type=F32)
    off_mask = same_chunk & ((ri // SUBLANES) > (ci // SUBLANES))
    a_off = jnp.where(off_mask, a_off, 0.0).astype(BF16)
    o = jnp.dot(a_off, vb, preferred_element_type=F32)

    sub = lax.broadcasted_iota(jnp.int32, (nblk, SUBLANES, HEAD), 1)
    q8 = qs.reshape(nblk, SUBLANES, HEAD)
    v8 = v.reshape(nblk, SUBLANES, HEAD)
    od = jnp.zeros((nblk, SUBLANES, HEAD), F32)
    for j in range(SUBLANES):
        bj = b8[:, j:j + 1, :]
        qj = q8[:, j:j + 1, :]
        e = jnp.exp(jnp.where(sub <= j, bj - b8, NEG))
        a = jnp.sum(e * kk8 * qj, axis=2, keepdims=True)
        oj = jnp.sum(a * v8, axis=1, keepdims=True)
        od = jnp.where(sub == j, oj, od)
    o = o + od.reshape(rows, HEAD)

    outs = []
    for c in range(nch):
        sl = slice(c * HG_CHUNK, (c + 1) * HG_CHUNK)
        outs.append(jnp.dot(q_in[sl], state.astype(BF16), preferred_element_type=F32))
        upd = lax.dot_general(k_end[sl], vb[sl], (((0,), (0,)), ((), ())), preferred_element_type=F32)
        state = state * decay_cols[:, c:c + 1] + upd
    o = o + jnp.concatenate(outs, axis=0)
    return o, state


def _hgrn_body(q_ref, z_ref, v_ref, g_ref, lbp_ref, gain_ref, o_ref, state_ref, *, layer):
    @pl.when(pl.program_id(2) == 0)
    def _():
        state_ref[...] = jnp.zeros_like(state_ref)

    lbp = lbp_ref[...]
    e = jnp.exp(lbp - jnp.max(lbp, axis=0, keepdims=True))
    lb = jnp.sum(e[:layer + 1], axis=0, keepdims=True) / jnp.sum(e, axis=0, keepdims=True)

    rows = q_ref.shape[0]
    state = state_ref[...]
    for s in range(rows // HG_SUB):
        sl = slice(s * HG_SUB, (s + 1) * HG_SUB)
        o, state = _hgrn_sub(q_ref[sl, :], z_ref[sl, :], v_ref[sl, :], lb, state)
        on = _rms(o, gain_ref[...])
        g = g_ref[sl, :]
        o_ref[sl, :] = (on * (g * _sigmoid(g))).astype(o_ref.dtype)
    state_ref[...] = state


def _hgrn(proj, lbp, gain, layer, *, batch, seq, heads, tr):
    n = proj.shape[0]
    assert seq % tr == 0 and tr % HG_SUB == 0
    ns = seq // tr
    row = lambda b, h, s: b * ns + s
    sec = lambda k: pl.BlockSpec((tr, HEAD), lambda b, h, s: (row(b, h, s), k * heads + h))
    return pl.pallas_call(
        functools.partial(_hgrn_body, layer=layer),
        grid=(batch, heads, ns),
        in_specs=[sec(0), sec(1), sec(2), sec(3),
                  pl.BlockSpec((lbp.shape[0], HEAD), lambda b, h, s: (0, h)),
                  pl.BlockSpec((1, HEAD), lambda b, h, s: (0, 0))],
        out_specs=pl.BlockSpec((tr, HEAD), lambda b, h, s: (row(b, h, s), h)),
        out_shape=jax.ShapeDtypeStruct((n, heads * HEAD), BF16),
        scratch_shapes=[pltpu.VMEM((HEAD, HEAD), F32)],
        compiler_params=_params(("parallel", "parallel", "arbitrary")),
        name="hgrn",
    )(proj, proj, proj, proj, lbp, gain)


def _attn_body(lam_ref, sub_ref, q_ref, k_ref, v_ref, o_ref, acc_ref, m_ref, l_ref, *, tq, tk, lambda_init):
    qi = pl.program_id(2)
    q = q_ref[...]
    nrep = tk // LANES

    m_ref[...] = jnp.full(m_ref.shape, NEG, F32)
    l_ref[...] = jnp.zeros(l_ref.shape, F32)
    acc_ref[...] = jnp.zeros(acc_ref.shape, F32)

    def tile(j, masked):
        start = pl.multiple_of(j * tk, tk)
        kt = k_ref[pl.ds(start, tk), :]
        vt = v_ref[pl.ds(start, tk), :]
        for c in range(2):
            cs = slice(c * HEAD, (c + 1) * HEAD)
            s = lax.dot_general(q[:, cs], kt[:, cs], (((1,), (1,)), ((), ())), preferred_element_type=F32)
            if masked:
                ri = lax.broadcasted_iota(jnp.int32, s.shape, 0)
                ci = lax.broadcasted_iota(jnp.int32, s.shape, 1)
                s = jnp.where(ci <= ri, s, NEG)
            m_old = m_ref[c]
            m_new = jnp.maximum(m_old, jnp.max(s, axis=-1, keepdims=True))
            alpha = jnp.exp(m_old - m_new)
            p = jnp.exp(s - pltpu.repeat(m_new, nrep, axis=1))
            l_ref[c] = alpha * l_ref[c] + jnp.sum(p, axis=-1, keepdims=True)
            pv = jnp.dot(p.astype(BF16), vt, preferred_element_type=F32)
            acc_ref[c] = pltpu.repeat(alpha, 2, axis=1) * acc_ref[c] + pv
            m_ref[c] = m_new

    def body(j, carry):
        tile(j, False)
        return carry

    lax.fori_loop(0, qi, body, 0)
    tile(qi, True)

    lam = lam_ref[...]
    lam_full = (jnp.exp(jnp.sum(lam[0:1] * lam[1:2], axis=-1, keepdims=True))
                - jnp.exp(jnp.sum(lam[2:3] * lam[3:4], axis=-1, keepdims=True)) + lambda_init)
    inv0 = pltpu.repeat(1.0 / l_ref[0], 2, axis=1)
    inv1 = pltpu.repeat(1.0 / l_ref[1], 2, axis=1)
    o = acc_ref[0] * inv0 - lam_full * (acc_ref[1] * inv1)
    o_ref[...] = (_rms(o, sub_ref[...]) * (1.0 - lambda_init)).astype(o_ref.dtype)


def _attention(q, kv, lam, subln, *, batch, seq, heads, tq, lambda_init):
    n = q.shape[0]
    vd = 2 * HEAD
    assert seq % tq == 0
    nq = seq // tq
    return pl.pallas_call(
        functools.partial(_attn_body, tq=tq, tk=tq, lambda_init=lambda_init),
        grid=(batch, heads, nq),
        in_specs=[pl.BlockSpec((4, HEAD), lambda b, h, i: (0, 0)),
                  pl.BlockSpec((1, vd), lambda b, h, i: (0, 0)),
                  pl.BlockSpec((tq, vd), lambda b, h, i: (b * nq + i, h)),
                  pl.BlockSpec((seq, vd), lambda b, h, i: (b, h)),
                  pl.BlockSpec((seq, vd), lambda b, h, i: (b, heads + h))],
        out_specs=pl.BlockSpec((tq, vd), lambda b, h, i: (b * nq + i, h)),
        out_shape=jax.ShapeDtypeStruct((n, heads * vd), BF16),
        scratch_shapes=[pltpu.VMEM((2, tq, vd), F32), pltpu.VMEM((2, tq, LANES), F32),
                        pltpu.VMEM((2, tq, LANES), F32)],
        compiler_params=_params(("parallel", "parallel", "arbitrary")),
        name="diff_attn",
    )(lam, subln, q, kv, kv)


def kernel(x, p, ffn_norm, ffn_w_gate_up, ffn_w_down, mix_norm, hgrn_w_in, hgrn_lower_bounds, hgrn_out_norm, hgrn_w_out, kv_norm, w_kv, diff_w_q, diff_lambda, diff_subln, diff_w_out, ple_norm, ple_w_gate, ple_w_proj, final_norm):
    batch, seq, d = x.shape
    depth = p.shape[0]
    n_a = hgrn_w_in.shape[0]
    n = batch * seq
    hg_heads = d // HEAD
    da_heads = d // (2 * HEAD)
    k_cols = da_heads * 2 * HEAD

    tm = min(512, seq)
    tq = min(512, seq)

    wgu = ffn_w_gate_up.astype(BF16)
    wdn = ffn_w_down.astype(BF16)
    w_in = hgrn_w_in.astype(BF16)
    w_ho = hgrn_w_out.astype(BF16)
    w_kvb = w_kv.astype(BF16)
    w_q = diff_w_q.astype(BF16)
    w_do = diff_w_out.astype(BF16)
    w_pg = ple_w_gate.astype(BF16)
    w_pp = ple_w_proj.astype(BF16)
    p2 = p.reshape(depth, n, p.shape[-1])

    rope = _rope_tables(seq)
    h = x.reshape(n, d)
    kv = None
    for i in range(depth):
        h = _ffn(h, ffn_norm[i, 0].reshape(1, d), wgu, wdn, i, 0, tm=tm, tf=512)
        if i < n_a:
            proj = _norm_mm(h, mix_norm[i].reshape(1, d), w_in, (i,), tm=tm, tn=1024, out_dtype=F32)
            og = _hgrn(proj, hgrn_lower_bounds, hgrn_out_norm[i].reshape(1, HEAD), i,
                       batch=batch, seq=seq, heads=hg_heads, tr=min(512, seq))
            h = _mm_resid(og, w_ho, (i,), h, tm=tm, tn=1024)
        else:
            j = i - n_a
            lambda_init = 0.8 - 0.6 * math.exp(-0.3 * i)
            q = _norm_mm(h, mix_norm[i].reshape(1, d), w_q, (j,), tm=tm, tn=1024, out_dtype=BF16,
                         rope=rope, seq=seq, rope_cols=k_cols, scale=HEAD ** -0.5)
            oa = _attention(q, kv, diff_lambda[j], diff_subln[j].reshape(1, 2 * HEAD),
                            batch=batch, seq=seq, heads=da_heads, tq=tq, lambda_init=lambda_init)
            h = _mm_resid(oa, w_do, (j,), h, tm=tm, tn=1024)
        h = _ffn(h, ffn_norm[i, 1].reshape(1, d), wgu, wdn, i, 1, tm=tm, tf=512)
        last = i == depth - 1
        h = _ple(h, ple_norm[i].reshape(1, d), w_pg, p2, w_pp, i, final_norm.reshape(1, d),
                 tm=tm, tn=512, final=last)
        if i == n_a - 1:
            kv = _norm_mm(h, kv_norm.reshape(1, d), w_kvb, (), tm=tm, tn=1024, out_dtype=BF16,
                          rope=rope, seq=seq, rope_cols=k_cols)
    return h.reshape(batch, seq, d)
```

```python
import functools
import math

import jax
import jax.numpy as jnp
from jax import lax
from jax.experimental import pallas as pl
from jax.experimental.pallas import tpu as pltpu

F32 = jnp.float32
BF16 = jnp.bfloat16

NORM_EPS = 1e-6
ROPE_THETA = 10000.0
HEAD = 128
LANES = 128
SUBLANES = 8
HG_CHUNK = 32
HG_SUB = 256
NEG = -1e30

VMEM_LIMIT = 56 * 1024 * 1024


def _params(sem):
    return pltpu.CompilerParams(dimension_semantics=sem, vmem_limit_bytes=VMEM_LIMIT)


def _rms(x, g):
    ms = jnp.mean(x * x, axis=-1, keepdims=True)
    return x * lax.rsqrt(ms + NORM_EPS) * g


def _sigmoid(x):
    return 1.0 / (1.0 + jnp.exp(-x))


def _ffn_body(x_ref, g_ref, wg_ref, wu_ref, wd_ref, o_ref, xn_ref, act_ref):
    f = pl.program_id(1)
    nf = pl.num_programs(1) - 1

    @pl.when(f == 0)
    def _():
        x = x_ref[...]
        xn_ref[...] = _rms(x, g_ref[...]).astype(BF16)
        act_ref[...] = jnp.zeros_like(act_ref)
        o_ref[...] = x

    @pl.when(f < nf)
    def _():
        part = jnp.dot(act_ref[...], wd_ref[...], preferred_element_type=F32)
        xn = xn_ref[...]
        gate = jnp.dot(xn, wg_ref[...], preferred_element_type=F32)
        up = jnp.dot(xn, wu_ref[...], preferred_element_type=F32)
        o_ref[...] += 0.5 * part
        act_ref[...] = (gate * _sigmoid(gate) * up).astype(BF16)

    @pl.when(f == nf)
    def _():
        o_ref[...] += 0.5 * jnp.dot(act_ref[...], wd_ref[...], preferred_element_type=F32)


def _ffn(x, gain, wgu, wd, li, lj, *, tm, tf):
    n, d = x.shape
    ff = wd.shape[2]
    nf = ff // tf
    assert n % tm == 0 and ff % tf == 0
    return pl.pallas_call(
        _ffn_body,
        grid=(n // tm, nf + 1),
        in_specs=[
            pl.BlockSpec((tm, d), lambda i, f: (i, 0)),
            pl.BlockSpec((1, d), lambda i, f: (0, 0)),
            pl.BlockSpec((None, None, d, tf), lambda i, f: (li, lj, 0, jnp.minimum(f, nf - 1))),
            pl.BlockSpec((None, None, d, tf), lambda i, f: (li, lj, 0, jnp.minimum(f, nf - 1) + nf)),
            pl.BlockSpec((None, None, tf, d), lambda i, f: (li, lj, jnp.maximum(f - 1, 0), 0)),
        ],
        out_specs=pl.BlockSpec((tm, d), lambda i, f: (i, 0)),
        out_shape=jax.ShapeDtypeStruct((n, d), F32),
        scratch_shapes=[pltpu.VMEM((tm, d), BF16), pltpu.VMEM((tm, tf), BF16)],
        compiler_params=_params(("parallel", "arbitrary")),
        name="ffn",
    )(x, gain, wgu, wgu, wd)


def _rope_store(y, cos, sin, o_ref, scale):
    for g in range(y.shape[1] // HEAD):
        yg = y[:, g * HEAD:(g + 1) * HEAD]
        r = yg * cos + pltpu.roll(yg, HEAD // 2, axis=1) * sin
        if scale != 1.0:
            r = r * scale
        o_ref[:, g * HEAD:(g + 1) * HEAD] = r.astype(o_ref.dtype)


def _norm_mm_body(x_ref, g_ref, w_ref, o_ref, xn_ref):
    @pl.when(pl.program_id(1) == 0)
    def _():
        xn_ref[...] = _rms(x_ref[...], g_ref[...]).astype(BF16)

    o_ref[...] = jnp.dot(xn_ref[...], w_ref[...], preferred_element_type=F32).astype(o_ref.dtype)


def _norm_mm_rope_body(x_ref, g_ref, w_ref, cos_ref, sin_ref, o_ref, xn_ref, *, rope_tiles, scale):
    j = pl.program_id(1)

    @pl.when(j == 0)
    def _():
        xn_ref[...] = _rms(x_ref[...], g_ref[...]).astype(BF16)

    y = jnp.dot(xn_ref[...], w_ref[...], preferred_element_type=F32)

    @pl.when(j < rope_tiles)
    def _():
        _rope_store(y, cos_ref[...], sin_ref[...], o_ref, scale)

    @pl.when(j >= rope_tiles)
    def _():
        o_ref[...] = y.astype(o_ref.dtype)


def _norm_mm(x, gain, w, w_index, *, tm, tn, out_dtype, rope=None, seq=None, rope_cols=0, scale=1.0,
             ntiles=None, src_tile=lambda j: j):
    n, d = x.shape
    nout = (w.shape[-1] // tn if ntiles is None else ntiles) * tn
    assert n % tm == 0 and w.shape[-1] % tn == 0
    nlead = w.ndim - 2
    w_spec = pl.BlockSpec((None,) * nlead + (d, tn), lambda i, j: tuple(w_index) + (0, src_tile(j)))
    in_specs = [pl.BlockSpec((tm, d), lambda i, j: (i, 0)),
                pl.BlockSpec((1, d), lambda i, j: (0, 0)),
                w_spec]
    args = [x, gain, w]
    if rope is None:
        body = _norm_mm_body
    else:
        assert seq % tm == 0 and rope_cols % tn == 0
        nseq = seq // tm
        tab_spec = pl.BlockSpec((tm, HEAD), lambda i, j: (i % nseq, 0))
        in_specs += [tab_spec, tab_spec]
        args += list(rope)
        body = functools.partial(_norm_mm_rope_body, rope_tiles=rope_cols // tn, scale=scale)
    return pl.pallas_call(
        body,
        grid=(n // tm, nout // tn),
        in_specs=in_specs,
        out_specs=pl.BlockSpec((tm, tn), lambda i, j: (i, j)),
        out_shape=jax.ShapeDtypeStruct((n, nout), out_dtype),
        scratch_shapes=[pltpu.VMEM((tm, d), BF16)],
        compiler_params=_params(("parallel", "arbitrary")),
        name="norm_mm",
    )(*args)


def _mm_resid_body(x_ref, w_ref, r_ref, o_ref):
    o_ref[...] = r_ref[...] + jnp.dot(x_ref[...], w_ref[...], preferred_element_type=F32)


def _mm_resid(x, w, w_index, resid, *, tm, tn):
    n, k = x.shape
    nout = w.shape[-1]
    assert n % tm == 0 and nout % tn == 0
    nlead = w.ndim - 2
    return pl.pallas_call(
        _mm_resid_body,
        grid=(n // tm, nout // tn),
        in_specs=[pl.BlockSpec((tm, k), lambda i, j: (i, 0)),
                  pl.BlockSpec((None,) * nlead + (k, tn), lambda i, j: tuple(w_index) + (0, j)),
                  pl.BlockSpec((tm, tn), lambda i, j: (i, j))],
        out_specs=pl.BlockSpec((tm, tn), lambda i, j: (i, j)),
        out_shape=jax.ShapeDtypeStruct((n, nout), F32),
        compiler_params=_params(("parallel", "arbitrary")),
        name="mm_resid",
    )(x, w, resid)


def _ple_body(x_ref, g_ref, wg_ref, p_ref, wp_ref, fg_ref, o_ref, *, tn, final):
    x = x_ref[...]
    xn = _rms(x, g_ref[...]).astype(BF16)
    pb = p_ref[...].astype(BF16)
    d = x.shape[1]
    for c in range(d // tn):
        sl = slice(c * tn, (c + 1) * tn)
        gate = jnp.dot(xn, wg_ref[:, sl], preferred_element_type=F32)
        proj = jnp.dot(pb, wp_ref[:, sl], preferred_element_type=F32)
        o_ref[:, sl] = x_ref[:, sl] + _sigmoid(gate) * proj
    if final:
        o_ref[...] = _rms(o_ref[...], fg_ref[...])


def _ple(x, gain, wg, p, wp, layer, final_gain, *, tm, tn, final):
    n, d = x.shape
    pd = p.shape[-1]
    rows = n // tm
    assert n % tm == 0 and d % tn == 0
    return pl.pallas_call(
        functools.partial(_ple_body, tn=tn, final=final),
        grid=(rows,),
        in_specs=[pl.BlockSpec((tm, d), lambda i: (i, 0)),
                  pl.BlockSpec((1, d), lambda i: (0, 0)),
                  pl.BlockSpec((None, d, d), lambda i: (layer, 0, 0)),
                  pl.BlockSpec((None, tm, pd), lambda i: (layer, i, 0)),
                  pl.BlockSpec((None, pd, d), lambda i: (layer, 0, 0)),
                  pl.BlockSpec((1, d), lambda i: (0, 0))],
        out_specs=pl.BlockSpec((tm, d), lambda i: (i, 0)),
        out_shape=jax.ShapeDtypeStruct((n, d), F32),
        compiler_params=_params(("parallel",)),
        name="ple",
    )(x, gain, wg, p, wp, final_gain)


def _rope_table_body(cos_ref, sin_ref):
    shape = cos_ref.shape
    pos = lax.broadcasted_iota(jnp.int32, shape, 0).astype(F32)
    lane = lax.broadcasted_iota(jnp.int32, shape, 1)
    half = HEAD // 2
    k = jnp.bitwise_and(lane, half - 1).astype(F32)
    inv_freq = jnp.exp(k * (-math.log(ROPE_THETA) / half))
    ang = pos * inv_freq
    cos_ref[...] = jnp.cos(ang)
    s = jnp.sin(ang)
    sin_ref[...] = jnp.where(lane < half, -s, s)


def _rope_tables(seq):
    return pl.pallas_call(
        _rope_table_body,
        out_shape=(jax.ShapeDtypeStruct((seq, HEAD), F32), jax.ShapeDtypeStruct((seq, HEAD), F32)),
        name="rope_tables",
    )()


def _split3(x):
    hi = x.astype(BF16)
    r1 = x - hi.astype(F32)
    mid = r1.astype(BF16)
    lo = (r1 - mid.astype(F32)).astype(BF16)
    return hi, mid, lo


def _hgrn_sub(q, z, v, lb, state):
    rows = q.shape[0]
    nch = rows // HG_CHUNK
    nblk = rows // SUBLANES
    bpc = HG_CHUNK // SUBLANES

    t = jnp.exp(-jnp.abs(z))
    r = 1.0 / (1.0 + t)
    pos = z >= 0
    sig = jnp.where(pos, r, t * r)
    nsig = jnp.where(pos, t * r, r)
    logf = jnp.log(lb + (1.0 - lb) * sig)
    kk = (1.0 - lb) * nsig
    qs = q * (HEAD ** -0.5)

    ri = lax.broadcasted_iota(jnp.int32, (rows, rows), 0)
    ci = lax.broadcasted_iota(jnp.int32, (rows, rows), 1)
    same_chunk = (ri // HG_CHUNK) == (ci // HG_CHUNK)
    tril = jnp.where(same_chunk, jnp.where(ci <= ri, 1.0, 0.0), 0.0).astype(BF16)
    hi, mid, lo = _split3(logf)
    b = (jnp.dot(tril, hi, preferred_element_type=F32)
         + jnp.dot(tril, mid, preferred_element_type=F32)
         + jnp.dot(tril, lo, preferred_element_type=F32))

    vb = v.astype(BF16)
    b3 = b.reshape(nch, HG_CHUNK, HEAD)
    b_last = b3[:, HG_CHUNK - 1:HG_CHUNK, :]
    q_in = (qs * jnp.exp(b)).astype(BF16)
    k_end = (kk.reshape(nch, HG_CHUNK, HEAD) * jnp.exp(b_last - b3)).reshape(rows, HEAD).astype(BF16)
    decay_cols = jnp.exp(b_last).reshape(nch, HEAD).T

    tloc = lax.broadcasted_iota(jnp.int32, (nch, HG_CHUNK, HEAD), 1)
    qs3 = qs.reshape(nch, HG_CHUNK, HEAD)
    b8 = b.reshape(nblk, SUBLANES, HEAD)
    kk8 = kk.reshape(nblk, SUBLANES, HEAD)
    k_blk = (kk8 * jnp.exp(b8[:, SUBLANES - 1:SUBLANES, :] - b8)).reshape(nch, HG_CHUNK, HEAD)
    q_groups, k_groups = [], []
    for jb in range(bpc - 1):
        ref = b3[:, jb * SUBLANES + SUBLANES - 1:jb * SUBLANES + SUBLANES, :]
        first = (jb + 1) * SUBLANES
        qg = qs3 * jnp.exp(jnp.where(tloc >= first, b3 - ref, NEG))
        kg = jnp.where((tloc >= jb * SUBLANES) & (tloc < first), k_blk, 0.0)
        q_groups.append(qg.reshape(rows, HEAD).astype(BF16))
        k_groups.append(kg.reshape(rows, HEAD).astype(BF16))
    q_cat = jnp.concatenate(q_groups, axis=1)
    k_cat = jnp.concatenate(k_groups, axis=1)
    a_off = lax.dot_general(q_cat, k_cat, (((1,), (1,)), ((), ())), preferred_element_type=F32)
    off_mask = same_chunk & ((ri // SUBLANES) > (ci // SUBLANES))
    a_off = jnp.where(off_mask, a_off, 0.0).astype(BF16)
    o = jnp.dot(a_off, vb, preferred_element_type=F32)

    sub = lax.broadcasted_iota(jnp.int32, (nblk, SUBLANES, HEAD), 1)
    q8 = qs.reshape(nblk, SUBLANES, HEAD)
    v8 = v.reshape(nblk, SUBLANES, HEAD)
    od = jnp.zeros((nblk, SUBLANES, HEAD), F32)
    for j in range(SUBLANES):
        bj = b8[:, j:j + 1, :]
        qj = q8[:, j:j + 1, :]
        e = jnp.exp(jnp.where(sub <= j, bj - b8, NEG))
        a = jnp.sum(e * kk8 * qj, axis=2, keepdims=True)
        oj = jnp.sum(a * v8, axis=1, keepdims=True)
        od = jnp.where(sub == j, oj, od)
    o = o + od.reshape(rows, HEAD)

    outs = []
    for c in range(nch):
        sl = slice(c * HG_CHUNK, (c + 1) * HG_CHUNK)
        outs.append(jnp.dot(q_in[sl], state.astype(BF16), preferred_element_type=F32))
        upd = lax.dot_general(k_end[sl], vb[sl], (((0,), (0,)), ((), ())), preferred_element_type=F32)
        state = state * decay_cols[:, c:c + 1] + upd
    o = o + jnp.concatenate(outs, axis=0)
    return o, state


def _hgrn_body(q_ref, z_ref, v_ref, g_ref, lbp_ref, gain_ref, o_ref, state_ref, *, layer):
    @pl.when(pl.program_id(2) == 0)
    def _():
        state_ref[...] = jnp.zeros_like(state_ref)

    lbp = lbp_ref[...]
    e = jnp.exp(lbp - jnp.max(lbp, axis=0, keepdims=True))
    lb = jnp.sum(e[:layer + 1], axis=0, keepdims=True) / jnp.sum(e, axis=0, keepdims=True)

    rows = q_ref.shape[0]
    state = state_ref[...]
    for s in range(rows // HG_SUB):
        sl = slice(s * HG_SUB, (s + 1) * HG_SUB)
        o, state = _hgrn_sub(q_ref[sl, :].astype(F32), z_ref[sl, :], v_ref[sl, :].astype(F32), lb, state)
        on = _rms(o, gain_ref[...])
        g = g_ref[sl, :].astype(F32)
        o_ref[sl, :] = (on * (g * _sigmoid(g))).astype(o_ref.dtype)
    state_ref[...] = state


def _hgrn(qvg, z, lbp, gain, layer, *, batch, seq, heads, tr):
    n = z.shape[0]
    assert seq % tr == 0 and tr % HG_SUB == 0
    ns = seq // tr
    row = lambda b, h, s: b * ns + s
    sec = lambda k: pl.BlockSpec((tr, HEAD), lambda b, h, s: (row(b, h, s), k * heads + h))
    return pl.pallas_call(
        functools.partial(_hgrn_body, layer=layer),
        grid=(batch, heads, ns),
        in_specs=[sec(0), sec(0), sec(1), sec(2),
                  pl.BlockSpec((lbp.shape[0], HEAD), lambda b, h, s: (0, h)),
                  pl.BlockSpec((1, HEAD), lambda b, h, s: (0, 0))],
        out_specs=pl.BlockSpec((tr, HEAD), lambda b, h, s: (row(b, h, s), h)),
        out_shape=jax.ShapeDtypeStruct((n, heads * HEAD), BF16),
        scratch_shapes=[pltpu.VMEM((HEAD, HEAD), F32)],
        compiler_params=_params(("parallel", "parallel", "arbitrary")),
        name="hgrn",
    )(qvg, z, qvg, qvg, lbp, gain)


def _attn_body(lam_ref, sub_ref, q_ref, k_ref, v_ref, o_ref, acc_ref, m_ref, l_ref, *, tq, tk, lambda_init):
    qi = pl.program_id(2)
    q = q_ref[...]
    nrep = tk // LANES

    m_ref[...] = jnp.full(m_ref.shape, NEG, F32)
    l_ref[...] = jnp.zeros(l_ref.shape, F32)
    acc_ref[...] = jnp.zeros(acc_ref.shape, F32)

    def tile(j, masked):
        start = pl.multiple_of(j * tk, tk)
        kt = k_ref[pl.ds(start, tk), :]
        vt = v_ref[pl.ds(start, tk), :]
        for c in range(2):
            cs = slice(c * HEAD, (c + 1) * HEAD)
            s = lax.dot_general(q[:, cs], kt[:, cs], (((1,), (1,)), ((), ())), preferred_element_type=F32)
            if masked:
                ri = lax.broadcasted_iota(jnp.int32, s.shape, 0)
                ci = lax.broadcasted_iota(jnp.int32, s.shape, 1)
                s = jnp.where(ci <= ri, s, NEG)
            m_old = m_ref[c]
            m_new = jnp.maximum(m_old, jnp.max(s, axis=-1, keepdims=True))
            alpha = jnp.exp(m_old - m_new)
            p = jnp.exp(s - pltpu.repeat(m_new, nrep, axis=1))
            l_ref[c] = alpha * l_ref[c] + jnp.sum(p, axis=-1, keepdims=True)
            pv = jnp.dot(p.astype(BF16), vt, preferred_element_type=F32)
            acc_ref[c] = pltpu.repeat(alpha, 2, axis=1) * acc_ref[c] + pv
            m_ref[c] = m_new

    def body(j, carry):
        tile(j, False)
        return carry

    lax.fori_loop(0, qi, body, 0)
    tile(qi, True)

    lam = lam_ref[...]
    lam_full = (jnp.exp(jnp.sum(lam[0:1] * lam[1:2], axis=-1, keepdims=True))
                - jnp.exp(jnp.sum(lam[2:3] * lam[3:4], axis=-1, keepdims=True)) + lambda_init)
    inv0 = pltpu.repeat(1.0 / l_ref[0], 2, axis=1)
    inv1 = pltpu.repeat(1.0 / l_ref[1], 2, axis=1)
    o = acc_ref[0] * inv0 - lam_full * (acc_ref[1] * inv1)
    o_ref[...] = (_rms(o, sub_ref[...]) * (1.0 - lambda_init)).astype(o_ref.dtype)


def _attention(q, kv, lam, subln, *, batch, seq, heads, tq, lambda_init):
    n = q.shape[0]
    vd = 2 * HEAD
    assert seq % tq == 0
    nq = seq // tq
    return pl.pallas_call(
        functools.partial(_attn_body, tq=tq, tk=tq, lambda_init=lambda_init),
        grid=(batch, heads, nq),
        in_specs=[pl.BlockSpec((4, HEAD), lambda b, h, i: (0, 0)),
                  pl.BlockSpec((1, vd), lambda b, h, i: (0, 0)),
                  pl.BlockSpec((tq, vd), lambda b, h, i: (b * nq + i, h)),
                  pl.BlockSpec((seq, vd), lambda b, h, i: (b, h)),
                  pl.BlockSpec((seq, vd), lambda b, h, i: (b, heads + h))],
        out_specs=pl.BlockSpec((tq, vd), lambda b, h, i: (b * nq + i, h)),
        out_shape=jax.ShapeDtypeStruct((n, heads * vd), BF16),
        scratch_shapes=[pltpu.VMEM((2, tq, vd), F32), pltpu.VMEM((2, tq, LANES), F32),
                        pltpu.VMEM((2, tq, LANES), F32)],
        compiler_params=_params(("parallel", "parallel", "arbitrary")),
        name="diff_attn",
    )(lam, subln, q, kv, kv)


def kernel(x, p, ffn_norm, ffn_w_gate_up, ffn_w_down, mix_norm, hgrn_w_in, hgrn_lower_bounds, hgrn_out_norm, hgrn_w_out, kv_norm, w_kv, diff_w_q, diff_lambda, diff_subln, diff_w_out, ple_norm, ple_w_gate, ple_w_proj, final_norm):
    batch, seq, d = x.shape
    depth = p.shape[0]
    n_a = hgrn_w_in.shape[0]
    n = batch * seq
    hg_heads = d // HEAD
    da_heads = d // (2 * HEAD)
    k_cols = da_heads * 2 * HEAD

    tm = min(512, seq)
    tm_big = min(1024, seq)
    tq = min(512, seq)
    wide = hg_heads * HEAD

    wgu = ffn_w_gate_up.astype(BF16)
    wdn = ffn_w_down.astype(BF16)
    w_in = hgrn_w_in.astype(BF16)
    w_ho = hgrn_w_out.astype(BF16)
    w_kvb = w_kv.astype(BF16)
    w_q = diff_w_q.astype(BF16)
    w_do = diff_w_out.astype(BF16)
    w_pg = ple_w_gate.astype(BF16)
    w_pp = ple_w_proj.astype(BF16)
    p2 = p.reshape(depth, n, p.shape[-1])

    rope = _rope_tables(seq)
    h = x.reshape(n, d)
    kv = None
    for i in range(depth):
        h = _ffn(h, ffn_norm[i, 0].reshape(1, d), wgu, wdn, i, 0, tm=tm_big, tf=512)
        if i < n_a:
            gain = mix_norm[i].reshape(1, d)
            z = _norm_mm(h, gain, w_in, (i,), tm=tm, tn=wide, out_dtype=F32, ntiles=1, src_tile=lambda j: j + 1)
            per = wide // 1024
            qvg = _norm_mm(h, gain, w_in, (i,), tm=tm_big, tn=1024, out_dtype=BF16, ntiles=3 * per,
                           src_tile=lambda j: j + jnp.where(j >= per, per, 0))
            og = _hgrn(qvg, z, hgrn_lower_bounds, hgrn_out_norm[i].reshape(1, HEAD), i,
                       batch=batch, seq=seq, heads=hg_heads, tr=min(512, seq))
            h = _mm_resid(og, w_ho, (i,), h, tm=tm, tn=d)
        else:
            j = i - n_a
            lambda_init = 0.8 - 0.6 * math.exp(-0.3 * i)
            q = _norm_mm(h, mix_norm[i].reshape(1, d), w_q, (j,), tm=tm, tn=k_cols, out_dtype=BF16,
                         rope=rope, seq=seq, rope_cols=k_cols, scale=HEAD ** -0.5)
            oa = _attention(q, kv, diff_lambda[j], diff_subln[j].reshape(1, 2 * HEAD),
                            batch=batch, seq=seq, heads=da_heads, tq=tq, lambda_init=lambda_init)
            h = _mm_resid(oa, w_do, (j,), h, tm=tm, tn=d)
        h = _ffn(h, ffn_norm[i, 1].reshape(1, d), wgu, wdn, i, 1, tm=tm_big, tf=512)
        last = i == depth - 1
        h = _ple(h, ple_norm[i].reshape(1, d), w_pg, p2, w_pp, i, final_norm.reshape(1, d),
                 tm=tm, tn=512, final=last)
        if i == n_a - 1:
            kv = _norm_mm(h, kv_norm.reshape(1, d), w_kvb, (), tm=tm_big, tn=1024, out_dtype=BF16,
                          rope=rope, seq=seq, rope_cols=k_cols)
    return h.reshape(batch, seq, d)
```

```python
import functools
import math

import jax
import jax.numpy as jnp
from jax import lax
from jax.experimental import pallas as pl
from jax.experimental.pallas import tpu as pltpu

F32 = jnp.float32
BF16 = jnp.bfloat16

NORM_EPS = 1e-6
ROPE_THETA = 10000.0
HEAD = 128
LANES = 128
SUBLANES = 8
HG_CHUNK = 32
HG_SUB = 256
NEG = -1e30

VMEM_LIMIT = 60 * 1024 * 1024


def _params(sem):
    return pltpu.CompilerParams(dimension_semantics=sem, vmem_limit_bytes=VMEM_LIMIT)


def _rms(x, g):
    ms = jnp.mean(x * x, axis=-1, keepdims=True)
    return x * lax.rsqrt(ms + NORM_EPS) * g


def _sigmoid(x):
    return 1.0 / (1.0 + jnp.exp(-x))


def _ffn_body(x_ref, g_ref, wg_ref, wu_ref, wd_ref, wdl_ref, o_ref, xn_ref, act_ref):
    f = pl.program_id(1)
    last = pl.num_programs(1) - 1

    def gate_up():
        xn = xn_ref[...]
        gate = jnp.dot(xn, wg_ref[...], preferred_element_type=F32)
        up = jnp.dot(xn, wu_ref[...], preferred_element_type=F32)
        return (gate * _sigmoid(gate) * up).astype(BF16)

    @pl.when(f == 0)
    def _():
        x = x_ref[...]
        xn_ref[...] = _rms(x, g_ref[...]).astype(BF16)
        o_ref[...] = x
        act_ref[...] = gate_up()

    @pl.when(f > 0)
    def _():
        part = jnp.dot(act_ref[...], wd_ref[...], preferred_element_type=F32)
        act = gate_up()
        o_ref[...] += 0.5 * part
        act_ref[...] = act

    @pl.when(f == last)
    def _():
        o_ref[...] += 0.5 * jnp.dot(act_ref[...], wdl_ref[...], preferred_element_type=F32)


def _ffn(x, gain, wgu, wd, li, lj, *, tm, tf):
    n, d = x.shape
    ff = wd.shape[2]
    nf = ff // tf
    assert n % tm == 0 and ff % tf == 0 and nf >= 2
    return pl.pallas_call(
        _ffn_body,
        grid=(n // tm, nf),
        in_specs=[
            pl.BlockSpec((tm, d), lambda i, f: (i, 0)),
            pl.BlockSpec((1, d), lambda i, f: (0, 0)),
            pl.BlockSpec((None, None, d, tf), lambda i, f: (li, lj, 0, f)),
            pl.BlockSpec((None, None, d, tf), lambda i, f: (li, lj, 0, f + nf)),
            pl.BlockSpec((None, None, tf, d), lambda i, f: (li, lj, jnp.maximum(f - 1, 0), 0)),
            pl.BlockSpec((None, None, tf, d), lambda i, f: (li, lj, nf - 1, 0), pipeline_mode=pl.Buffered(1)),
        ],
        out_specs=pl.BlockSpec((tm, d), lambda i, f: (i, 0)),
        out_shape=jax.ShapeDtypeStruct((n, d), F32),
        scratch_shapes=[pltpu.VMEM((tm, d), BF16), pltpu.VMEM((tm, tf), BF16)],
        compiler_params=_params(("parallel", "arbitrary")),
        name="ffn",
    )(x, gain, wgu, wgu, wd, wd)


def _rope_store(y, cos, sin, o_ref, scale):
    for g in range(y.shape[1] // HEAD):
        yg = y[:, g * HEAD:(g + 1) * HEAD]
        r = yg * cos + pltpu.roll(yg, HEAD // 2, axis=1) * sin
        if scale != 1.0:
            r = r * scale
        o_ref[:, g * HEAD:(g + 1) * HEAD] = r.astype(o_ref.dtype)


def _norm_mm_body(x_ref, g_ref, w_ref, o_ref, xn_ref):
    @pl.when(pl.program_id(1) == 0)
    def _():
        xn_ref[...] = _rms(x_ref[...], g_ref[...]).astype(BF16)

    o_ref[...] = jnp.dot(xn_ref[...], w_ref[...], preferred_element_type=F32).astype(o_ref.dtype)


def _norm_mm_rope_body(x_ref, g_ref, w_ref, cos_ref, sin_ref, o_ref, xn_ref, *, rope_tiles, scale):
    j = pl.program_id(1)

    @pl.when(j == 0)
    def _():
        xn_ref[...] = _rms(x_ref[...], g_ref[...]).astype(BF16)

    y = jnp.dot(xn_ref[...], w_ref[...], preferred_element_type=F32)

    @pl.when(j < rope_tiles)
    def _():
        _rope_store(y, cos_ref[...], sin_ref[...], o_ref, scale)

    @pl.when(j >= rope_tiles)
    def _():
        o_ref[...] = y.astype(o_ref.dtype)


def _norm_mm(x, gain, w, w_index, *, tm, tn, out_dtype, rope=None, seq=None, rope_cols=0, scale=1.0,
             ntiles=None, src_tile=lambda j: j):
    n, d = x.shape
    nout = (w.shape[-1] // tn if ntiles is None else ntiles) * tn
    assert n % tm == 0 and w.shape[-1] % tn == 0
    nlead = w.ndim - 2
    w_spec = pl.BlockSpec((None,) * nlead + (d, tn), lambda i, j: tuple(w_index) + (0, src_tile(j)))
    in_specs = [pl.BlockSpec((tm, d), lambda i, j: (i, 0)),
                pl.BlockSpec((1, d), lambda i, j: (0, 0)),
                w_spec]
    args = [x, gain, w]
    if rope is None:
        body = _norm_mm_body
    else:
        assert seq % tm == 0 and rope_cols % tn == 0
        nseq = seq // tm
        tab_spec = pl.BlockSpec((tm, HEAD), lambda i, j: (i % nseq, 0))
        in_specs += [tab_spec, tab_spec]
        args += list(rope)
        body = functools.partial(_norm_mm_rope_body, rope_tiles=rope_cols // tn, scale=scale)
    return pl.pallas_call(
        body,
        grid=(n // tm, nout // tn),
        in_specs=in_specs,
        out_specs=pl.BlockSpec((tm, tn), lambda i, j: (i, j)),
        out_shape=jax.ShapeDtypeStruct((n, nout), out_dtype),
        scratch_shapes=[pltpu.VMEM((tm, d), BF16)],
        compiler_params=_params(("parallel", "arbitrary")),
        name="norm_mm",
    )(*args)


def _mm_resid_body(x_ref, w_ref, r_ref, o_ref):
    o_ref[...] = r_ref[...] + jnp.dot(x_ref[...], w_ref[...], preferred_element_type=F32)


def _mm_resid(x, w, w_index, resid, *, tm, tn):
    n, k = x.shape
    nout = w.shape[-1]
    assert n % tm == 0 and nout % tn == 0
    nlead = w.ndim - 2
    return pl.pallas_call(
        _mm_resid_body,
        grid=(n // tm, nout // tn),
        in_specs=[pl.BlockSpec((tm, k), lambda i, j: (i, 0)),
                  pl.BlockSpec((None,) * nlead + (k, tn), lambda i, j: tuple(w_index) + (0, j)),
                  pl.BlockSpec((tm, tn), lambda i, j: (i, j))],
        out_specs=pl.BlockSpec((tm, tn), lambda i, j: (i, j)),
        out_shape=jax.ShapeDtypeStruct((n, nout), F32),
        compiler_params=_params(("parallel", "arbitrary")),
        name="mm_resid",
    )(x, w, resid)


def _ple_body(x_ref, g_ref, wg_ref, p_ref, wp_ref, fg_ref, o_ref, *, tn, final):
    x = x_ref[...]
    xn = _rms(x, g_ref[...]).astype(BF16)
    pb = p_ref[...].astype(BF16)
    d = x.shape[1]
    for c in range(d // tn):
        sl = slice(c * tn, (c + 1) * tn)
        gate = jnp.dot(xn, wg_ref[:, sl], preferred_element_type=F32)
        proj = jnp.dot(pb, wp_ref[:, sl], preferred_element_type=F32)
        o_ref[:, sl] = x_ref[:, sl] + _sigmoid(gate) * proj
    if final:
        o_ref[...] = _rms(o_ref[...], fg_ref[...])


def _ple(x, gain, wg, p, wp, layer, final_gain, *, tm, tn, final):
    n, d = x.shape
    pd = p.shape[-1]
    rows = n // tm
    assert n % tm == 0 and d % tn == 0
    return pl.pallas_call(
        functools.partial(_ple_body, tn=tn, final=final),
        grid=(rows,),
        in_specs=[pl.BlockSpec((tm, d), lambda i: (i, 0)),
                  pl.BlockSpec((1, d), lambda i: (0, 0)),
                  pl.BlockSpec((None, d, d), lambda i: (layer, 0, 0)),
                  pl.BlockSpec((None, tm, pd), lambda i: (layer, i, 0)),
                  pl.BlockSpec((None, pd, d), lambda i: (layer, 0, 0)),
                  pl.BlockSpec((1, d), lambda i: (0, 0))],
        out_specs=pl.BlockSpec((tm, d), lambda i: (i, 0)),
        out_shape=jax.ShapeDtypeStruct((n, d), F32),
        compiler_params=_params(("parallel",)),
        name="ple",
    )(x, gain, wg, p, wp, final_gain)


def _rope_table_body(cos_ref, sin_ref):
    shape = cos_ref.shape
    pos = lax.broadcasted_iota(jnp.int32, shape, 0).astype(F32)
    lane = lax.broadcasted_iota(jnp.int32, shape, 1)
    half = HEAD // 2
    k = jnp.bitwise_and(lane, half - 1).astype(F32)
    inv_freq = jnp.exp(k * (-math.log(ROPE_THETA) / half))
    ang = pos * inv_freq
    cos_ref[...] = jnp.cos(ang)
    s = jnp.sin(ang)
    sin_ref[...] = jnp.where(lane < half, -s, s)


def _rope_tables(seq):
    return pl.pallas_call(
        _rope_table_body,
        out_shape=(jax.ShapeDtypeStruct((seq, HEAD), F32), jax.ShapeDtypeStruct((seq, HEAD), F32)),
        name="rope_tables",
    )()


def _split3(x):
    hi = x.astype(BF16)
    r1 = x - hi.astype(F32)
    mid = r1.astype(BF16)
    lo = (r1 - mid.astype(F32)).astype(BF16)
    return hi, mid, lo


def _hgrn_sub(q, z, v, lb, state):
    rows = q.shape[0]
    nch = rows // HG_CHUNK
    nblk = rows // SUBLANES
    bpc = HG_CHUNK // SUBLANES

    t = jnp.exp(-jnp.abs(z))
    r = 1.0 / (1.0 + t)
    pos = z >= 0
    sig = jnp.where(pos, r, t * r)
    nsig = jnp.where(pos, t * r, r)
    logf = jnp.log(lb + (1.0 - lb) * sig)
    kk = (1.0 - lb) * nsig
    qs = q * (HEAD ** -0.5)

    ri = lax.broadcasted_iota(jnp.int32, (rows, rows), 0)
    ci = lax.broadcasted_iota(jnp.int32, (rows, rows), 1)
    same_chunk = (ri // HG_CHUNK) == (ci // HG_CHUNK)
    tril = jnp.where(same_chunk, jnp.where(ci <= ri, 1.0, 0.0), 0.0).astype(BF16)
    hi, mid, lo = _split3(logf)
    b = (jnp.dot(tril, hi, preferred_element_type=F32)
         + jnp.dot(tril, mid, preferred_element_type=F32)
         + jnp.dot(tril, lo, preferred_element_type=F32))

    vb = v.astype(BF16)
    b3 = b.reshape(nch, HG_CHUNK, HEAD)
    b_last = b3[:, HG_CHUNK - 1:HG_CHUNK, :]
    q_in = (qs * jnp.exp(b)).astype(BF16)
    k_end = (kk.reshape(nch, HG_CHUNK, HEAD) * jnp.exp(b_last - b3)).reshape(rows, HEAD).astype(BF16)
    decay_cols = jnp.exp(b_last).reshape(nch, HEAD).T

    tloc = lax.broadcasted_iota(jnp.int32, (nch, HG_CHUNK, HEAD), 1)
    qs3 = qs.reshape(nch, HG_CHUNK, HEAD)
    b8 = b.reshape(nblk, SUBLANES, HEAD)
    kk8 = kk.reshape(nblk, SUBLANES, HEAD)
    k_blk = (kk8 * jnp.exp(b8[:, SUBLANES - 1:SUBLANES, :] - b8)).reshape(nch, HG_CHUNK, HEAD)
    q_groups, k_groups = [], []
    for jb in range(bpc - 1):
        ref = b3[:, jb * SUBLANES + SUBLANES - 1:jb * SUBLANES + SUBLANES, :]
        first = (jb + 1) * SUBLANES
        qg = qs3 * jnp.exp(jnp.where(tloc >= first, b3 - ref, NEG))
        kg = jnp.where((tloc >= jb * SUBLANES) & (tloc < first), k_blk, 0.0)
        q_groups.append(qg.reshape(rows, HEAD).astype(BF16))
        k_groups.append(kg.reshape(rows, HEAD).astype(BF16))
    q_cat = jnp.concatenate(q_groups, axis=1)
    k_cat = jnp.concatenate(k_groups, axis=1)
    a_off = lax.dot_general(q_cat, k_cat, (((1,), (1,)), ((), ())), preferred_element_type=F32)
    off_mask = same_chunk & ((ri // SUBLANES) > (ci // SUBLANES))
    a_off = jnp.where(off_mask, a_off, 0.0).astype(BF16)
    o = jnp.dot(a_off, vb, preferred_element_type=F32)

    sub = lax.broadcasted_iota(jnp.int32, (nblk, SUBLANES, HEAD), 1)
    q8 = qs.reshape(nblk, SUBLANES, HEAD)
    v8 = v.reshape(nblk, SUBLANES, HEAD)
    od = jnp.zeros((nblk, SUBLANES, HEAD), F32)
    for j in range(SUBLANES):
        bj = b8[:, j:j + 1, :]
        qj = q8[:, j:j + 1, :]
        e = jnp.exp(jnp.where(sub <= j, bj - b8, NEG))
        a = jnp.sum(e * kk8 * qj, axis=2, keepdims=True)
        oj = jnp.sum(a * v8, axis=1, keepdims=True)
        od = jnp.where(sub == j, oj, od)
    o = o + od.reshape(rows, HEAD)

    outs = []
    for c in range(nch):
        sl = slice(c * HG_CHUNK, (c + 1) * HG_CHUNK)
        outs.append(jnp.dot(q_in[sl], state.astype(BF16), preferred_element_type=F32))
        upd = lax.dot_general(k_end[sl], vb[sl], (((0,), (0,)), ((), ())), preferred_element_type=F32)
        state = state * decay_cols[:, c:c + 1] + upd
    o = o + jnp.concatenate(outs, axis=0)
    return o, state


def _hgrn_body(q_ref, z_ref, v_ref, g_ref, lbp_ref, gain_ref, o_ref, state_ref, *, layer):
    @pl.when(pl.program_id(2) == 0)
    def _():
        state_ref[...] = jnp.zeros_like(state_ref)

    lbp = lbp_ref[...]
    e = jnp.exp(lbp - jnp.max(lbp, axis=0, keepdims=True))
    lb = jnp.sum(e[:layer + 1], axis=0, keepdims=True) / jnp.sum(e, axis=0, keepdims=True)

    rows = q_ref.shape[0]
    state = state_ref[...]
    for s in range(rows // HG_SUB):
        sl = slice(s * HG_SUB, (s + 1) * HG_SUB)
        o, state = _hgrn_sub(q_ref[sl, :].astype(F32), z_ref[sl, :], v_ref[sl, :].astype(F32), lb, state)
        on = _rms(o, gain_ref[...])
        g = g_ref[sl, :].astype(F32)
        o_ref[sl, :] = (on * (g * _sigmoid(g))).astype(o_ref.dtype)
    state_ref[...] = state


def _hgrn(qvg, z, lbp, gain, layer, *, batch, seq, heads, tr):
    n = z.shape[0]
    assert seq % tr == 0 and tr % HG_SUB == 0
    ns = seq // tr
    row = lambda b, h, s: b * ns + s
    sec = lambda k: pl.BlockSpec((tr, HEAD), lambda b, h, s: (row(b, h, s), k * heads + h))
    return pl.pallas_call(
        functools.partial(_hgrn_body, layer=layer),
        grid=(batch, heads, ns),
        in_specs=[sec(0), sec(0), sec(1), sec(2),
                  pl.BlockSpec((lbp.shape[0], HEAD), lambda b, h, s: (0, h)),
                  pl.BlockSpec((1, HEAD), lambda b, h, s: (0, 0))],
        out_specs=pl.BlockSpec((tr, HEAD), lambda b, h, s: (row(b, h, s), h)),
        out_shape=jax.ShapeDtypeStruct((n, heads * HEAD), BF16),
        scratch_shapes=[pltpu.VMEM((HEAD, HEAD), F32)],
        compiler_params=_params(("parallel", "parallel", "arbitrary")),
        name="hgrn",
    )(qvg, z, qvg, qvg, lbp, gain)


def _attn_body(lam_ref, sub_ref, q_ref, k_ref, v_ref, o_ref, acc_ref, m_ref, l_ref, s0_ref, s1_ref, *, tq, tk,
               lambda_init):
    qi = pl.program_id(2)
    nrep = tk // LANES

    m_ref[...] = jnp.full(m_ref.shape, NEG, F32)
    l_ref[...] = jnp.zeros(l_ref.shape, F32)
    acc_ref[...] = jnp.zeros(acc_ref.shape, F32)

    def scores(j, s_ref):
        kt = k_ref[pl.ds(pl.multiple_of(j * tk, tk), tk), :]
        q = q_ref[...]
        for c in range(2):
            cs = slice(c * HEAD, (c + 1) * HEAD)
            s_ref[c] = lax.dot_general(q[:, cs], kt[:, cs], (((1,), (1,)), ((), ())), preferred_element_type=F32)

    def consume(j, s_ref, masked):
        vt = v_ref[pl.ds(pl.multiple_of(j * tk, tk), tk), :]
        for c in range(2):
            s = s_ref[c]
            if masked:
                ri = lax.broadcasted_iota(jnp.int32, s.shape, 0)
                ci = lax.broadcasted_iota(jnp.int32, s.shape, 1)
                s = jnp.where(ci <= ri, s, NEG)
            m_old = m_ref[c]
            m_new = jnp.maximum(m_old, jnp.max(s, axis=-1, keepdims=True))
            alpha = jnp.exp2(m_old - m_new)
            p = jnp.exp2(s - jnp.concatenate([m_new] * nrep, axis=1))
            l_ref[c] = alpha * l_ref[c] + jnp.sum(p, axis=-1, keepdims=True)
            pv = jnp.dot(p.astype(BF16), vt, preferred_element_type=F32)
            acc_ref[c] = jnp.concatenate([alpha, alpha], axis=1) * acc_ref[c] + pv
            m_ref[c] = m_new

    scores(0, s0_ref)

    def pair(i, carry):
        scores(2 * i + 1, s1_ref)
        consume(2 * i, s0_ref, False)
        scores(2 * i + 2, s0_ref)
        consume(2 * i + 1, s1_ref, False)
        return carry

    lax.fori_loop(0, qi // 2, pair, 0)

    @pl.when(qi % 2 == 1)
    def _():
        scores(qi, s1_ref)
        consume(qi - 1, s0_ref, False)
        consume(qi, s1_ref, True)

    @pl.when(qi % 2 == 0)
    def _():
        consume(qi, s0_ref, True)

    lam = lam_ref[...]
    lam_full = (jnp.exp(jnp.sum(lam[0:1] * lam[1:2], axis=-1, keepdims=True))
                - jnp.exp(jnp.sum(lam[2:3] * lam[3:4], axis=-1, keepdims=True)) + lambda_init)
    inv0 = jnp.concatenate([1.0 / l_ref[0]] * 2, axis=1)
    inv1 = jnp.concatenate([1.0 / l_ref[1]] * 2, axis=1)
    o = acc_ref[0] * inv0 - lam_full * (acc_ref[1] * inv1)
    o_ref[...] = (_rms(o, sub_ref[...]) * (1.0 - lambda_init)).astype(o_ref.dtype)


def _attention(q, kv, lam, subln, *, batch, seq, heads, tq, lambda_init):
    n = q.shape[0]
    vd = 2 * HEAD
    assert seq % tq == 0
    nq = seq // tq
    return pl.pallas_call(
        functools.partial(_attn_body, tq=tq, tk=tq, lambda_init=lambda_init),
        grid=(batch, heads, nq),
        in_specs=[pl.BlockSpec((4, HEAD), lambda b, h, i: (0, 0)),
                  pl.BlockSpec((1, vd), lambda b, h, i: (0, 0)),
                  pl.BlockSpec((tq, vd), lambda b, h, i: (b * nq + i, h)),
                  pl.BlockSpec((seq, vd), lambda b, h, i: (b, h)),
                  pl.BlockSpec((seq, vd), lambda b, h, i: (b, heads + h))],
        out_specs=pl.BlockSpec((tq, vd), lambda b, h, i: (b * nq + i, h)),
        out_shape=jax.ShapeDtypeStruct((n, heads * vd), BF16),
        scratch_shapes=[pltpu.VMEM((2, tq, vd), F32), pltpu.VMEM((2, tq, LANES), F32),
                        pltpu.VMEM((2, tq, LANES), F32), pltpu.VMEM((2, tq, tq), F32), pltpu.VMEM((2, tq, tq), F32)],
        compiler_params=_params(("parallel", "parallel", "arbitrary")),
        name="diff_attn",
    )(lam, subln, q, kv, kv)


def kernel(x, p, ffn_norm, ffn_w_gate_up, ffn_w_down, mix_norm, hgrn_w_in, hgrn_lower_bounds, hgrn_out_norm, hgrn_w_out, kv_norm, w_kv, diff_w_q, diff_lambda, diff_subln, diff_w_out, ple_norm, ple_w_gate, ple_w_proj, final_norm):
    batch, seq, d = x.shape
    depth = p.shape[0]
    n_a = hgrn_w_in.shape[0]
    n = batch * seq
    hg_heads = d // HEAD
    da_heads = d // (2 * HEAD)
    k_cols = da_heads * 2 * HEAD

    tm = min(512, seq)
    tm_big = min(1024, seq)
    tq = min(512, seq)
    wide = hg_heads * HEAD

    wgu = ffn_w_gate_up.astype(BF16)
    wdn = ffn_w_down.astype(BF16)
    w_in = hgrn_w_in.astype(BF16)
    w_ho = hgrn_w_out.astype(BF16)
    w_kvb = w_kv.astype(BF16)
    w_q = diff_w_q.astype(BF16)
    w_do = diff_w_out.astype(BF16)
    w_pg = ple_w_gate.astype(BF16)
    w_pp = ple_w_proj.astype(BF16)
    p2 = p.reshape(depth, n, p.shape[-1])

    rope = _rope_tables(seq)
    h = x.reshape(n, d)
    kv = None
    for i in range(depth):
        h = _ffn(h, ffn_norm[i, 0].reshape(1, d), wgu, wdn, i, 0, tm=tm_big, tf=512)
        if i < n_a:
            gain = mix_norm[i].reshape(1, d)
            z = _norm_mm(h, gain, w_in, (i,), tm=tm, tn=wide, out_dtype=F32, ntiles=1, src_tile=lambda j: j + 1)
            per = wide // 1024
            qvg = _norm_mm(h, gain, w_in, (i,), tm=tm_big, tn=1024, out_dtype=BF16, ntiles=3 * per,
                           src_tile=lambda j: j + jnp.where(j >= per, per, 0))
            og = _hgrn(qvg, z, hgrn_lower_bounds, hgrn_out_norm[i].reshape(1, HEAD), i,
                       batch=batch, seq=seq, heads=hg_heads, tr=min(512, seq))
            h = _mm_resid(og, w_ho, (i,), h, tm=tm, tn=d)
        else:
            j = i - n_a
            lambda_init = 0.8 - 0.6 * math.exp(-0.3 * i)
            q = _norm_mm(h, mix_norm[i].reshape(1, d), w_q, (j,), tm=tm, tn=k_cols, out_dtype=BF16,
                         rope=rope, seq=seq, rope_cols=k_cols, scale=HEAD ** -0.5 * math.log2(math.e))
            oa = _attention(q, kv, diff_lambda[j], diff_subln[j].reshape(1, 2 * HEAD),
                            batch=batch, seq=seq, heads=da_heads, tq=tq, lambda_init=lambda_init)
            h = _mm_resid(oa, w_do, (j,), h, tm=tm, tn=d)
        h = _ffn(h, ffn_norm[i, 1].reshape(1, d), wgu, wdn, i, 1, tm=tm_big, tf=512)
        last = i == depth - 1
        h = _ple(h, ple_norm[i].reshape(1, d), w_pg, p2, w_pp, i, final_norm.reshape(1, d),
                 tm=tm, tn=512, final=last)
        if i == n_a - 1:
            kv = _norm_mm(h, kv_norm.reshape(1, d), w_kvb, (), tm=tm_big, tn=1024, out_dtype=BF16,
                          rope=rope, seq=seq, rope_cols=k_cols)
    return h.reshape(batch, seq, d)
```

```python
import functools
import math

import jax
import jax.numpy as jnp
import numpy as np
from jax import lax
from jax.experimental import pallas as pl
from jax.experimental.pallas import tpu as pltpu

F32 = jnp.float32
BF16 = jnp.bfloat16

NORM_EPS = 1e-6
ROPE_THETA = 10000.0
HEAD = 128
LANES = 128
SUBLANES = 8
HG_CHUNK = 32
HG_SUB = 256
HG_UNROLL = 4
NEG = -1e30

VMEM_LIMIT = 60 * 1024 * 1024


def _params(sem):
    return pltpu.CompilerParams(dimension_semantics=sem, vmem_limit_bytes=VMEM_LIMIT)


def _rms(x, g):
    ms = jnp.mean(x * x, axis=-1, keepdims=True)
    return x * lax.rsqrt(ms + NORM_EPS) * g


def _sigmoid(x):
    return 1.0 / (1.0 + jnp.exp(-x))


def _ffn_body(x_ref, g_ref, wg_ref, wu_ref, wd_ref, wdl_ref, o_ref, xn_ref, act_ref):
    f = pl.program_id(1)
    last = pl.num_programs(1) - 1

    def gate_up():
        xn = xn_ref[...]
        gate = jnp.dot(xn, wg_ref[...], preferred_element_type=F32)
        up = jnp.dot(xn, wu_ref[...], preferred_element_type=F32)
        return (gate * _sigmoid(gate) * up).astype(BF16)

    @pl.when(f == 0)
    def _():
        x = x_ref[...]
        xn_ref[...] = _rms(x, g_ref[...]).astype(BF16)
        o_ref[...] = x
        act_ref[...] = gate_up()

    @pl.when(f > 0)
    def _():
        part = jnp.dot(act_ref[...], wd_ref[...], preferred_element_type=F32)
        act = gate_up()
        o_ref[...] += 0.5 * part
        act_ref[...] = act

    @pl.when(f == last)
    def _():
        o_ref[...] += 0.5 * jnp.dot(act_ref[...], wdl_ref[...], preferred_element_type=F32)


def _ffn(x, gain, wgu, wd, li, lj, *, tm, tf):
    n, d = x.shape
    ff = wd.shape[2]
    nf = ff // tf
    assert n % tm == 0 and ff % tf == 0 and nf >= 2
    return pl.pallas_call(
        _ffn_body,
        grid=(n // tm, nf),
        in_specs=[
            pl.BlockSpec((tm, d), lambda i, f: (i, 0)),
            pl.BlockSpec((1, d), lambda i, f: (0, 0)),
            pl.BlockSpec((None, None, d, tf), lambda i, f: (li, lj, 0, f)),
            pl.BlockSpec((None, None, d, tf), lambda i, f: (li, lj, 0, f + nf)),
            pl.BlockSpec((None, None, tf, d), lambda i, f: (li, lj, jnp.maximum(f - 1, 0), 0)),
            pl.BlockSpec((None, None, tf, d), lambda i, f: (li, lj, nf - 1, 0), pipeline_mode=pl.Buffered(1)),
        ],
        out_specs=pl.BlockSpec((tm, d), lambda i, f: (i, 0)),
        out_shape=jax.ShapeDtypeStruct((n, d), F32),
        scratch_shapes=[pltpu.VMEM((tm, d), BF16), pltpu.VMEM((tm, tf), BF16)],
        compiler_params=_params(("parallel", "arbitrary")),
        name="ffn",
    )(x, gain, wgu, wgu, wd, wd)


def _rope_store(y, cos, sin, o_ref, scale):
    for g in range(y.shape[1] // HEAD):
        yg = y[:, g * HEAD:(g + 1) * HEAD]
        r = yg * cos + pltpu.roll(yg, HEAD // 2, axis=1) * sin
        if scale != 1.0:
            r = r * scale
        o_ref[:, g * HEAD:(g + 1) * HEAD] = r.astype(o_ref.dtype)


def _norm_mm_body(x_ref, g_ref, w_ref, o_ref, xn_ref):
    @pl.when(pl.program_id(1) == 0)
    def _():
        xn_ref[...] = _rms(x_ref[...], g_ref[...]).astype(BF16)

    o_ref[...] = jnp.dot(xn_ref[...], w_ref[...], preferred_element_type=F32).astype(o_ref.dtype)


def _norm_mm_rope_body(x_ref, g_ref, w_ref, cos_ref, sin_ref, o_ref, xn_ref, *, rope_tiles, scale):
    j = pl.program_id(1)

    @pl.when(j == 0)
    def _():
        xn_ref[...] = _rms(x_ref[...], g_ref[...]).astype(BF16)

    y = jnp.dot(xn_ref[...], w_ref[...], preferred_element_type=F32)

    @pl.when(j < rope_tiles)
    def _():
        _rope_store(y, cos_ref[...], sin_ref[...], o_ref, scale)

    @pl.when(j >= rope_tiles)
    def _():
        o_ref[...] = y.astype(o_ref.dtype)


def _norm_mm(x, gain, w, w_index, *, tm, tn, out_dtype, rope=None, seq=None, rope_cols=0, scale=1.0,
             ntiles=None, src_tile=lambda j: j):
    n, d = x.shape
    nout = (w.shape[-1] // tn if ntiles is None else ntiles) * tn
    assert n % tm == 0 and w.shape[-1] % tn == 0
    nlead = w.ndim - 2
    w_spec = pl.BlockSpec((None,) * nlead + (d, tn), lambda i, j: tuple(w_index) + (0, src_tile(j)))
    in_specs = [pl.BlockSpec((tm, d), lambda i, j: (i, 0)),
                pl.BlockSpec((1, d), lambda i, j: (0, 0)),
                w_spec]
    args = [x, gain, w]
    if rope is None:
        body = _norm_mm_body
    else:
        assert seq % tm == 0 and rope_cols % tn == 0
        nseq = seq // tm
        tab_spec = pl.BlockSpec((tm, HEAD), lambda i, j: (i % nseq, 0))
        in_specs += [tab_spec, tab_spec]
        args += list(rope)
        body = functools.partial(_norm_mm_rope_body, rope_tiles=rope_cols // tn, scale=scale)
    return pl.pallas_call(
        body,
        grid=(n // tm, nout // tn),
        in_specs=in_specs,
        out_specs=pl.BlockSpec((tm, tn), lambda i, j: (i, j)),
        out_shape=jax.ShapeDtypeStruct((n, nout), out_dtype),
        scratch_shapes=[pltpu.VMEM((tm, d), BF16)],
        compiler_params=_params(("parallel", "arbitrary")),
        name="norm_mm",
    )(*args)


def _mm_resid_body(x_ref, w_ref, r_ref, o_ref):
    o_ref[...] = r_ref[...] + jnp.dot(x_ref[...], w_ref[...], preferred_element_type=F32)


def _mm_resid(x, w, w_index, resid, *, tm, tn):
    n, k = x.shape
    nout = w.shape[-1]
    assert n % tm == 0 and nout % tn == 0
    nlead = w.ndim - 2
    return pl.pallas_call(
        _mm_resid_body,
        grid=(n // tm, nout // tn),
        in_specs=[pl.BlockSpec((tm, k), lambda i, j: (i, 0)),
                  pl.BlockSpec((None,) * nlead + (k, tn), lambda i, j: tuple(w_index) + (0, j)),
                  pl.BlockSpec((tm, tn), lambda i, j: (i, j))],
        out_specs=pl.BlockSpec((tm, tn), lambda i, j: (i, j)),
        out_shape=jax.ShapeDtypeStruct((n, nout), F32),
        compiler_params=_params(("parallel", "arbitrary")),
        name="mm_resid",
    )(x, w, resid)


def _ple_body(x_ref, g_ref, wg_ref, p_ref, wp_ref, fg_ref, o_ref, *, tn, final):
    x = x_ref[...]
    xn = _rms(x, g_ref[...]).astype(BF16)
    pb = p_ref[...].astype(BF16)
    d = x.shape[1]
    for c in range(d // tn):
        sl = slice(c * tn, (c + 1) * tn)
        gate = jnp.dot(xn, wg_ref[:, sl], preferred_element_type=F32)
        proj = jnp.dot(pb, wp_ref[:, sl], preferred_element_type=F32)
        o_ref[:, sl] = x_ref[:, sl] + _sigmoid(gate) * proj
    if final:
        o_ref[...] = _rms(o_ref[...], fg_ref[...])


def _ple(x, gain, wg, p, wp, layer, final_gain, *, tm, tn, final):
    n, d = x.shape
    pd = p.shape[-1]
    rows = n // tm
    assert n % tm == 0 and d % tn == 0
    return pl.pallas_call(
        functools.partial(_ple_body, tn=tn, final=final),
        grid=(rows,),
        in_specs=[pl.BlockSpec((tm, d), lambda i: (i, 0)),
                  pl.BlockSpec((1, d), lambda i: (0, 0)),
                  pl.BlockSpec((None, d, d), lambda i: (layer, 0, 0)),
                  pl.BlockSpec((None, tm, pd), lambda i: (layer, i, 0)),
                  pl.BlockSpec((None, pd, d), lambda i: (layer, 0, 0)),
                  pl.BlockSpec((1, d), lambda i: (0, 0))],
        out_specs=pl.BlockSpec((tm, d), lambda i: (i, 0)),
        out_shape=jax.ShapeDtypeStruct((n, d), F32),
        compiler_params=_params(("parallel",)),
        name="ple",
    )(x, gain, wg, p, wp, final_gain)


def _rope_table_body(cos_ref, sin_ref):
    shape = cos_ref.shape
    pos = lax.broadcasted_iota(jnp.int32, shape, 0).astype(F32)
    lane = lax.broadcasted_iota(jnp.int32, shape, 1)
    half = HEAD // 2
    k = jnp.bitwise_and(lane, half - 1).astype(F32)
    inv_freq = jnp.exp(k * (-math.log(ROPE_THETA) / half))
    ang = pos * inv_freq
    cos_ref[...] = jnp.cos(ang)
    s = jnp.sin(ang)
    sin_ref[...] = jnp.where(lane < half, -s, s)


def _rope_tables(seq):
    return pl.pallas_call(
        _rope_table_body,
        out_shape=(jax.ShapeDtypeStruct((seq, HEAD), F32), jax.ShapeDtypeStruct((seq, HEAD), F32)),
        name="rope_tables",
    )()


def _split3(x):
    hi = x.astype(BF16)
    r1 = x - hi.astype(F32)
    mid = r1.astype(BF16)
    lo = (r1 - mid.astype(F32)).astype(BF16)
    return hi, mid, lo


def _hgrn_consts():
    r = np.arange(HG_SUB)
    chunk, blk = r // HG_CHUNK, r // SUBLANES
    nch = HG_SUB // HG_CHUNK
    same_chunk = chunk[:, None] == chunk[None, :]
    tril = same_chunk & (r[None, :] <= r[:, None])
    off = same_chunk & (blk[:, None] > blk[None, :])
    diag = blk[:, None] == blk[None, :]
    lane_blk = np.arange(SUBLANES * HEAD) // HEAD
    spread = lane_blk[:, None] == (r % SUBLANES)[None, :]
    bpc = HG_CHUNK // SUBLANES
    kmask = np.stack([np.broadcast_to((((r % HG_CHUNK) // SUBLANES) == jb)[:, None], (HG_SUB, HEAD))
                      for jb in range(bpc - 1)])
    cmask = np.stack([np.broadcast_to((chunk == c)[None, :], (HEAD, HG_SUB)) for c in range(nch)])
    eye = np.eye(HEAD)
    return (jnp.asarray(tril, BF16), jnp.asarray(off, F32), jnp.asarray(diag, F32),
            jnp.asarray(spread, BF16), jnp.asarray(kmask, F32), jnp.asarray(cmask, F32), jnp.asarray(eye, BF16))


def _hgrn_front(q, z, v, lb, tril, kmask):
    rows = q.shape[0]
    nch = rows // HG_CHUNK
    nblk = rows // SUBLANES
    bpc = HG_CHUNK // SUBLANES

    t = jnp.exp(-jnp.abs(z))
    r = 1.0 / (1.0 + t)
    pos = z >= 0
    sig = jnp.where(pos, r, t * r)
    nsig = jnp.where(pos, t * r, r)
    logf = jnp.log(lb + (1.0 - lb) * sig)
    kk = (1.0 - lb) * nsig
    qs = q * (HEAD ** -0.5)

    hi, mid, lo = _split3(logf)
    b = (jnp.dot(tril, hi, preferred_element_type=F32)
         + jnp.dot(tril, mid, preferred_element_type=F32)
         + jnp.dot(tril, lo, preferred_element_type=F32))

    vb = v.astype(BF16)
    b3 = b.reshape(nch, HG_CHUNK, HEAD)
    b_last = b3[:, HG_CHUNK - 1:HG_CHUNK, :]
    q_in = (qs * jnp.exp(b)).astype(BF16)
    k_end = (kk.reshape(nch, HG_CHUNK, HEAD) * jnp.exp(b_last - b3)).reshape(rows, HEAD).astype(BF16)
    decay_cols = jnp.exp(b_last).reshape(nch, HEAD).T

    tloc = lax.broadcasted_iota(jnp.int32, (nch, HG_CHUNK, HEAD), 1)
    qs3 = qs.reshape(nch, HG_CHUNK, HEAD)
    b8 = b.reshape(nblk, SUBLANES, HEAD)
    kk8 = kk.reshape(nblk, SUBLANES, HEAD)
    k_blk = (kk8 * jnp.exp(b8[:, SUBLANES - 1:SUBLANES, :] - b8)).reshape(rows, HEAD)
    q_groups, k_groups = [], []
    for jb in range(bpc - 1):
        ref = b3[:, jb * SUBLANES + SUBLANES - 1:jb * SUBLANES + SUBLANES, :]
        qg = qs3 * jnp.exp(jnp.where(tloc >= (jb + 1) * SUBLANES, b3 - ref, NEG))
        q_groups.append(qg.reshape(rows, HEAD).astype(BF16))
        k_groups.append((k_blk * kmask[jb]).astype(BF16))
    q_cat = jnp.concatenate(q_groups, axis=1)
    k_cat = jnp.concatenate(k_groups, axis=1)

    sub = lax.broadcasted_iota(jnp.int32, (nblk, SUBLANES, HEAD), 1)
    q8 = qs.reshape(nblk, SUBLANES, HEAD)
    prods = []
    for i in range(SUBLANES):
        e = jnp.exp(jnp.where(sub >= i, b8 - b8[:, i:i + 1, :], NEG))
        prods.append((e * kk8[:, i:i + 1, :] * q8).reshape(rows, HEAD).astype(BF16))
    p_cat = jnp.concatenate(prods, axis=1)
    return q_cat, k_cat, p_cat, q_in, k_end, decay_cols, vb


def _hgrn_back(front, state, off, diag, spread, cmask, eye):
    q_cat, k_cat, p_cat, q_in, k_end, decay_cols, vb = front
    nch = decay_cols.shape[1]
    a = lax.dot_general(q_cat, k_cat, (((1,), (1,)), ((), ())), preferred_element_type=F32) * off
    a = a + jnp.dot(p_cat, spread, preferred_element_type=F32) * diag
    o = jnp.dot(a.astype(BF16), vb, preferred_element_type=F32)

    k_end_t = lax.dot_general(eye, k_end, (((1,), (1,)), ((), ())), preferred_element_type=F32)
    k_stack = jnp.concatenate([(k_end_t * cmask[c]).astype(BF16) for c in range(nch)], axis=0)
    upd = jnp.dot(k_stack, vb, preferred_element_type=F32)

    outs = []
    for c in range(nch):
        outs.append(jnp.dot(q_in[c * HG_CHUNK:(c + 1) * HG_CHUNK], state.astype(BF16), preferred_element_type=F32))
        state = state * decay_cols[:, c:c + 1] + upd[c * HEAD:(c + 1) * HEAD]
    o = o + jnp.concatenate(outs, axis=0)
    return o, state


def _hgrn_body(q_ref, z_ref, v_ref, g_ref, lbp_ref, gain_ref, tril_ref, off_ref, diag_ref, spread_ref, kmask_ref,
               cmask_ref, eye_ref, o_ref, state_ref, *, layer):
    lbp = lbp_ref[...]
    e = jnp.exp(lbp - jnp.max(lbp, axis=0, keepdims=True))
    lb = jnp.sum(e[:layer + 1], axis=0, keepdims=True) / jnp.sum(e, axis=0, keepdims=True)

    state_ref[...] = jnp.zeros_like(state_ref)
    consts = (tril_ref[...], off_ref[...], diag_ref[...], spread_ref[...], kmask_ref[...], cmask_ref[...],
              eye_ref[...])

    def slabs(s, carry):
        state = state_ref[...]
        sls = [pl.ds(pl.multiple_of((s * HG_UNROLL + u) * HG_SUB, HG_SUB), HG_SUB) for u in range(HG_UNROLL)]
        fronts = [_hgrn_front(q_ref[sl, :].astype(F32), z_ref[sl, :], v_ref[sl, :].astype(F32), lb, consts[0],
                              consts[4]) for sl in sls]
        for sl, front in zip(sls, fronts):
            o, state = _hgrn_back(front, state, consts[1], consts[2], consts[3], consts[5], consts[6])
            on = _rms(o, gain_ref[...])
            g = g_ref[sl, :].astype(F32)
            o_ref[sl, :] = (on * (g * _sigmoid(g))).astype(o_ref.dtype)
        state_ref[...] = state
        return carry

    lax.fori_loop(0, q_ref.shape[0] // (HG_SUB * HG_UNROLL), slabs, 0)


def _hgrn(qvg, z, lbp, gain, layer, *, batch, seq, heads):
    n = z.shape[0]
    assert seq % (HG_SUB * HG_UNROLL) == 0
    consts = _hgrn_consts()
    sec = lambda k: pl.BlockSpec((seq, HEAD), lambda b, h: (b, k * heads + h))
    whole = lambda a: pl.BlockSpec(a.shape, lambda b, h: (0,) * a.ndim)
    return pl.pallas_call(
        functools.partial(_hgrn_body, layer=layer),
        grid=(batch, heads),
        in_specs=[sec(0), sec(0), sec(1), sec(2),
                  pl.BlockSpec((lbp.shape[0], HEAD), lambda b, h: (0, h)),
                  pl.BlockSpec((1, HEAD), lambda b, h: (0, 0))] + [whole(a) for a in consts],
        out_specs=pl.BlockSpec((seq, HEAD), lambda b, h: (b, h)),
        out_shape=jax.ShapeDtypeStruct((n, heads * HEAD), BF16),
        scratch_shapes=[pltpu.VMEM((HEAD, HEAD), F32)],
        compiler_params=_params(("parallel", "parallel")),
        name="hgrn",
    )(qvg, z, qvg, qvg, lbp, gain, *consts)


def _attn_body(lam_ref, sub_ref, q_ref, k_ref, v_ref, o_ref, acc_ref, m_ref, l_ref, s0_ref, s1_ref, *, tq, tk,
               lambda_init):
    qi = pl.program_id(2)
    nrep = tk // LANES

    m_ref[...] = jnp.full(m_ref.shape, NEG, F32)
    l_ref[...] = jnp.zeros(l_ref.shape, F32)
    acc_ref[...] = jnp.zeros(acc_ref.shape, F32)

    def scores(j, s_ref):
        kt = k_ref[pl.ds(pl.multiple_of(j * tk, tk), tk), :]
        q = q_ref[...]
        for c in range(2):
            cs = slice(c * HEAD, (c + 1) * HEAD)
            s_ref[c] = lax.dot_general(q[:, cs], kt[:, cs], (((1,), (1,)), ((), ())), preferred_element_type=F32)

    def consume(j, s_ref, masked):
        vt = v_ref[pl.ds(pl.multiple_of(j * tk, tk), tk), :]
        for c in range(2):
            s = s_ref[c]
            if masked:
                ri = lax.broadcasted_iota(jnp.int32, s.shape, 0)
                ci = lax.broadcasted_iota(jnp.int32, s.shape, 1)
                s = jnp.where(ci <= ri, s, NEG)
            m_old = m_ref[c]
            m_new = jnp.maximum(m_old, jnp.max(s, axis=-1, keepdims=True))
            alpha = jnp.exp2(m_old - m_new)
            p = jnp.exp2(s - jnp.concatenate([m_new] * nrep, axis=1))
            l_ref[c] = alpha * l_ref[c] + jnp.sum(p, axis=-1, keepdims=True)
            pv = jnp.dot(p.astype(BF16), vt, preferred_element_type=F32)
            acc_ref[c] = jnp.concatenate([alpha, alpha], axis=1) * acc_ref[c] + pv
            m_ref[c] = m_new

    scores(0, s0_ref)

    def pair(i, carry):
        scores(2 * i + 1, s1_ref)
        consume(2 * i, s0_ref, False)
        scores(2 * i + 2, s0_ref)
        consume(2 * i + 1, s1_ref, False)
        return carry

    lax.fori_loop(0, qi // 2, pair, 0)

    @pl.when(qi % 2 == 1)
    def _():
        scores(qi, s1_ref)
        consume(qi - 1, s0_ref, False)
        consume(qi, s1_ref, True)

    @pl.when(qi % 2 == 0)
    def _():
        consume(qi, s0_ref, True)

    lam = lam_ref[...]
    lam_full = (jnp.exp(jnp.sum(lam[0:1] * lam[1:2], axis=-1, keepdims=True))
                - jnp.exp(jnp.sum(lam[2:3] * lam[3:4], axis=-1, keepdims=True)) + lambda_init)
    inv0 = jnp.concatenate([1.0 / l_ref[0]] * 2, axis=1)
    inv1 = jnp.concatenate([1.0 / l_ref[1]] * 2, axis=1)
    o = acc_ref[0] * inv0 - lam_full * (acc_ref[1] * inv1)
    o_ref[...] = (_rms(o, sub_ref[...]) * (1.0 - lambda_init)).astype(o_ref.dtype)


def _attention(q, kv, lam, subln, *, batch, seq, heads, tq, lambda_init):
    n = q.shape[0]
    vd = 2 * HEAD
    assert seq % tq == 0
    nq = seq // tq
    return pl.pallas_call(
        functools.partial(_attn_body, tq=tq, tk=tq, lambda_init=lambda_init),
        grid=(batch, heads, nq),
        in_specs=[pl.BlockSpec((4, HEAD), lambda b, h, i: (0, 0)),
                  pl.BlockSpec((1, vd), lambda b, h, i: (0, 0)),
                  pl.BlockSpec((tq, vd), lambda b, h, i: (b * nq + i, h)),
                  pl.BlockSpec((seq, vd), lambda b, h, i: (b, h)),
                  pl.BlockSpec((seq, vd), lambda b, h, i: (b, heads + h))],
        out_specs=pl.BlockSpec((tq, vd), lambda b, h, i: (b * nq + i, h)),
        out_shape=jax.ShapeDtypeStruct((n, heads * vd), BF16),
        scratch_shapes=[pltpu.VMEM((2, tq, vd), F32), pltpu.VMEM((2, tq, LANES), F32),
                        pltpu.VMEM((2, tq, LANES), F32), pltpu.VMEM((2, tq, tq), F32), pltpu.VMEM((2, tq, tq), F32)],
        compiler_params=_params(("parallel", "parallel", "arbitrary")),
        name="diff_attn",
    )(lam, subln, q, kv, kv)


def kernel(x, p, ffn_norm, ffn_w_gate_up, ffn_w_down, mix_norm, hgrn_w_in, hgrn_lower_bounds, hgrn_out_norm, hgrn_w_out, kv_norm, w_kv, diff_w_q, diff_lambda, diff_subln, diff_w_out, ple_norm, ple_w_gate, ple_w_proj, final_norm):
    batch, seq, d = x.shape
    depth = p.shape[0]
    n_a = hgrn_w_in.shape[0]
    n = batch * seq
    hg_heads = d // HEAD
    da_heads = d // (2 * HEAD)
    k_cols = da_heads * 2 * HEAD

    tm = min(512, seq)
    tm_big = min(1024, seq)
    tq = min(512, seq)
    wide = hg_heads * HEAD

    wgu = ffn_w_gate_up.astype(BF16)
    wdn = ffn_w_down.astype(BF16)
    w_in = hgrn_w_in.astype(BF16)
    w_ho = hgrn_w_out.astype(BF16)
    w_kvb = w_kv.astype(BF16)
    w_q = diff_w_q.astype(BF16)
    w_do = diff_w_out.astype(BF16)
    w_pg = ple_w_gate.astype(BF16)
    w_pp = ple_w_proj.astype(BF16)
    p2 = p.reshape(depth, n, p.shape[-1])

    rope = _rope_tables(seq)
    h = x.reshape(n, d)
    kv = None
    for i in range(depth):
        h = _ffn(h, ffn_norm[i, 0].reshape(1, d), wgu, wdn, i, 0, tm=tm_big, tf=512)
        if i < n_a:
            gain = mix_norm[i].reshape(1, d)
            z = _norm_mm(h, gain, w_in, (i,), tm=tm, tn=wide, out_dtype=F32, ntiles=1, src_tile=lambda j: j + 1)
            per = wide // 1024
            qvg = _norm_mm(h, gain, w_in, (i,), tm=tm_big, tn=1024, out_dtype=BF16, ntiles=3 * per,
                           src_tile=lambda j: j + jnp.where(j >= per, per, 0))
            og = _hgrn(qvg, z, hgrn_lower_bounds, hgrn_out_norm[i].reshape(1, HEAD), i,
                       batch=batch, seq=seq, heads=hg_heads)
            h = _mm_resid(og, w_ho, (i,), h, tm=tm, tn=d)
        else:
            j = i - n_a
            lambda_init = 0.8 - 0.6 * math.exp(-0.3 * i)
            q = _norm_mm(h, mix_norm[i].reshape(1, d), w_q, (j,), tm=tm, tn=k_cols, out_dtype=BF16,
                         rope=rope, seq=seq, rope_cols=k_cols, scale=HEAD ** -0.5 * math.log2(math.e))
            oa = _attention(q, kv, diff_lambda[j], diff_subln[j].reshape(1, 2 * HEAD),
                            batch=batch, seq=seq, heads=da_heads, tq=tq, lambda_init=lambda_init)
            h = _mm_resid(oa, w_do, (j,), h, tm=tm, tn=d)
        h = _ffn(h, ffn_norm[i, 1].reshape(1, d), wgu, wdn, i, 1, tm=tm_big, tf=512)
        last = i == depth - 1
        h = _ple(h, ple_norm[i].reshape(1, d), w_pg, p2, w_pp, i, final_norm.reshape(1, d),
                 tm=tm, tn=512, final=last)
        if i == n_a - 1:
            kv = _norm_mm(h, kv_norm.reshape(1, d), w_kvb, (), tm=tm_big, tn=1024, out_dtype=BF16,
                          rope=rope, seq=seq, rope_cols=k_cols)
    return h.reshape(batch, seq, d)
```

```python
import functools
import math

import jax
import jax.numpy as jnp
import numpy as np
from jax import lax
from jax.experimental import pallas as pl
from jax.experimental.pallas import tpu as pltpu

F32 = jnp.float32
BF16 = jnp.bfloat16

NORM_EPS = 1e-6
ROPE_THETA = 10000.0
HEAD = 128
LANES = 128
SUBLANES = 8
HG_CHUNK = 32
HG_SUB = 256
HG_UNROLL = 4
NEG = -1e30

VMEM_LIMIT = 60 * 1024 * 1024


def _params(sem):
    return pltpu.CompilerParams(dimension_semantics=sem, vmem_limit_bytes=VMEM_LIMIT)


def _rms(x, g):
    ms = jnp.mean(x * x, axis=-1, keepdims=True)
    return x * lax.rsqrt(ms + NORM_EPS) * g


def _sigmoid(x):
    return 1.0 / (1.0 + jnp.exp(-x))


def _ffn_body(x_ref, g_ref, wg_ref, wu_ref, wd_ref, wdl_ref, o_ref, xn_ref, act_ref):
    f = pl.program_id(1)
    last = pl.num_programs(1) - 1

    def gate_up():
        xn = xn_ref[...]
        gate = jnp.dot(xn, wg_ref[...], preferred_element_type=F32)
        up = jnp.dot(xn, wu_ref[...], preferred_element_type=F32)
        return (gate * _sigmoid(gate) * up).astype(BF16)

    @pl.when(f == 0)
    def _():
        x = x_ref[...]
        xn_ref[...] = _rms(x, g_ref[...]).astype(BF16)
        o_ref[...] = x
        act_ref[...] = gate_up()

    @pl.when(f > 0)
    def _():
        part = jnp.dot(act_ref[...], wd_ref[...], preferred_element_type=F32)
        act = gate_up()
        o_ref[...] += 0.5 * part
        act_ref[...] = act

    @pl.when(f == last)
    def _():
        o_ref[...] += 0.5 * jnp.dot(act_ref[...], wdl_ref[...], preferred_element_type=F32)


def _ffn(x, gain, wgu, wd, li, lj, *, tm, tf):
    n, d = x.shape
    ff = wd.shape[2]
    nf = ff // tf
    assert n % tm == 0 and ff % tf == 0 and nf >= 2
    return pl.pallas_call(
        _ffn_body,
        grid=(n // tm, nf),
        in_specs=[
            pl.BlockSpec((tm, d), lambda i, f: (i, 0)),
            pl.BlockSpec((1, d), lambda i, f: (0, 0)),
            pl.BlockSpec((None, None, d, tf), lambda i, f: (li, lj, 0, f)),
            pl.BlockSpec((None, None, d, tf), lambda i, f: (li, lj, 0, f + nf)),
            pl.BlockSpec((None, None, tf, d), lambda i, f: (li, lj, jnp.maximum(f - 1, 0), 0)),
            pl.BlockSpec((None, None, tf, d), lambda i, f: (li, lj, nf - 1, 0), pipeline_mode=pl.Buffered(1)),
        ],
        out_specs=pl.BlockSpec((tm, d), lambda i, f: (i, 0)),
        out_shape=jax.ShapeDtypeStruct((n, d), F32),
        scratch_shapes=[pltpu.VMEM((tm, d), BF16), pltpu.VMEM((tm, tf), BF16)],
        compiler_params=_params(("parallel", "arbitrary")),
        name="ffn",
    )(x, gain, wgu, wgu, wd, wd)


def _rope_store(y, cos, sin, o_ref, scale):
    for g in range(y.shape[1] // HEAD):
        yg = y[:, g * HEAD:(g + 1) * HEAD]
        r = yg * cos + pltpu.roll(yg, HEAD // 2, axis=1) * sin
        if scale != 1.0:
            r = r * scale
        o_ref[:, g * HEAD:(g + 1) * HEAD] = r.astype(o_ref.dtype)


def _norm_mm_body(x_ref, g_ref, w_ref, o_ref, xn_ref):
    @pl.when(pl.program_id(1) == 0)
    def _():
        xn_ref[...] = _rms(x_ref[...], g_ref[...]).astype(BF16)

    o_ref[...] = jnp.dot(xn_ref[...], w_ref[...], preferred_element_type=F32).astype(o_ref.dtype)


def _norm_mm_rope_body(x_ref, g_ref, w_ref, cos_ref, sin_ref, o_ref, xn_ref, *, rope_tiles, scale):
    j = pl.program_id(1)

    @pl.when(j == 0)
    def _():
        xn_ref[...] = _rms(x_ref[...], g_ref[...]).astype(BF16)

    y = jnp.dot(xn_ref[...], w_ref[...], preferred_element_type=F32)

    @pl.when(j < rope_tiles)
    def _():
        _rope_store(y, cos_ref[...], sin_ref[...], o_ref, scale)

    @pl.when(j >= rope_tiles)
    def _():
        o_ref[...] = y.astype(o_ref.dtype)


def _norm_mm(x, gain, w, w_index, *, tm, tn, out_dtype, rope=None, seq=None, rope_cols=0, scale=1.0,
             ntiles=None, src_tile=lambda j: j):
    n, d = x.shape
    nout = (w.shape[-1] // tn if ntiles is None else ntiles) * tn
    assert n % tm == 0 and w.shape[-1] % tn == 0
    nlead = w.ndim - 2
    w_spec = pl.BlockSpec((None,) * nlead + (d, tn), lambda i, j: tuple(w_index) + (0, src_tile(j)))
    in_specs = [pl.BlockSpec((tm, d), lambda i, j: (i, 0)),
                pl.BlockSpec((1, d), lambda i, j: (0, 0)),
                w_spec]
    args = [x, gain, w]
    if rope is None:
        body = _norm_mm_body
    else:
        assert seq % tm == 0 and rope_cols % tn == 0
        nseq = seq // tm
        tab_spec = pl.BlockSpec((tm, HEAD), lambda i, j: (i % nseq, 0))
        in_specs += [tab_spec, tab_spec]
        args += list(rope)
        body = functools.partial(_norm_mm_rope_body, rope_tiles=rope_cols // tn, scale=scale)
    return pl.pallas_call(
        body,
        grid=(n // tm, nout // tn),
        in_specs=in_specs,
        out_specs=pl.BlockSpec((tm, tn), lambda i, j: (i, j)),
        out_shape=jax.ShapeDtypeStruct((n, nout), out_dtype),
        scratch_shapes=[pltpu.VMEM((tm, d), BF16)],
        compiler_params=_params(("parallel", "arbitrary")),
        name="norm_mm",
    )(*args)


def _mm_resid_body(x_ref, w_ref, r_ref, o_ref):
    o_ref[...] = r_ref[...] + jnp.dot(x_ref[...], w_ref[...], preferred_element_type=F32)


def _mm_resid(x, w, w_index, resid, *, tm, tn):
    n, k = x.shape
    nout = w.shape[-1]
    assert n % tm == 0 and nout % tn == 0
    nlead = w.ndim - 2
    return pl.pallas_call(
        _mm_resid_body,
        grid=(n // tm, nout // tn),
        in_specs=[pl.BlockSpec((tm, k), lambda i, j: (i, 0)),
                  pl.BlockSpec((None,) * nlead + (k, tn), lambda i, j: tuple(w_index) + (0, j)),
                  pl.BlockSpec((tm, tn), lambda i, j: (i, j))],
        out_specs=pl.BlockSpec((tm, tn), lambda i, j: (i, j)),
        out_shape=jax.ShapeDtypeStruct((n, nout), F32),
        compiler_params=_params(("parallel", "arbitrary")),
        name="mm_resid",
    )(x, w, resid)


def _ple_body(x_ref, g_ref, wg_ref, p_ref, wp_ref, fg_ref, o_ref, *, tn, final):
    x = x_ref[...]
    xn = _rms(x, g_ref[...]).astype(BF16)
    pb = p_ref[...].astype(BF16)
    d = x.shape[1]
    for c in range(d // tn):
        sl = slice(c * tn, (c + 1) * tn)
        gate = jnp.dot(xn, wg_ref[:, sl], preferred_element_type=F32)
        proj = jnp.dot(pb, wp_ref[:, sl], preferred_element_type=F32)
        o_ref[:, sl] = x_ref[:, sl] + _sigmoid(gate) * proj
    if final:
        o_ref[...] = _rms(o_ref[...], fg_ref[...])


def _ple(x, gain, wg, p, wp, layer, final_gain, *, tm, tn, final):
    n, d = x.shape
    pd = p.shape[-1]
    rows = n // tm
    assert n % tm == 0 and d % tn == 0
    return pl.pallas_call(
        functools.partial(_ple_body, tn=tn, final=final),
        grid=(rows,),
        in_specs=[pl.BlockSpec((tm, d), lambda i: (i, 0)),
                  pl.BlockSpec((1, d), lambda i: (0, 0)),
                  pl.BlockSpec((None, d, d), lambda i: (layer, 0, 0)),
                  pl.BlockSpec((None, tm, pd), lambda i: (layer, i, 0)),
                  pl.BlockSpec((None, pd, d), lambda i: (layer, 0, 0)),
                  pl.BlockSpec((1, d), lambda i: (0, 0))],
        out_specs=pl.BlockSpec((tm, d), lambda i: (i, 0)),
        out_shape=jax.ShapeDtypeStruct((n, d), F32),
        compiler_params=_params(("parallel",)),
        name="ple",
    )(x, gain, wg, p, wp, final_gain)


def _rope_table_body(cos_ref, sin_ref):
    shape = cos_ref.shape
    pos = lax.broadcasted_iota(jnp.int32, shape, 0).astype(F32)
    lane = lax.broadcasted_iota(jnp.int32, shape, 1)
    half = HEAD // 2
    k = jnp.bitwise_and(lane, half - 1).astype(F32)
    inv_freq = jnp.exp(k * (-math.log(ROPE_THETA) / half))
    ang = pos * inv_freq
    cos_ref[...] = jnp.cos(ang)
    s = jnp.sin(ang)
    sin_ref[...] = jnp.where(lane < half, -s, s)


def _rope_tables(seq):
    return pl.pallas_call(
        _rope_table_body,
        out_shape=(jax.ShapeDtypeStruct((seq, HEAD), F32), jax.ShapeDtypeStruct((seq, HEAD), F32)),
        name="rope_tables",
    )()


def _split3(x):
    hi = x.astype(BF16)
    r1 = x - hi.astype(F32)
    mid = r1.astype(BF16)
    lo = (r1 - mid.astype(F32)).astype(BF16)
    return hi, mid, lo


def _hgrn_consts():
    r = np.arange(HG_SUB)
    chunk, blk = r // HG_CHUNK, r // SUBLANES
    nch = HG_SUB // HG_CHUNK
    same_chunk = chunk[:, None] == chunk[None, :]
    tril = same_chunk & (r[None, :] <= r[:, None])
    off = same_chunk & (blk[:, None] > blk[None, :])
    diag = (blk[:, None] == blk[None, :]) & (r[None, :] <= r[:, None])
    lane_blk = np.arange(SUBLANES * HEAD) // HEAD
    spread = lane_blk[:, None] == (r % SUBLANES)[None, :]
    bpc = HG_CHUNK // SUBLANES
    kmask = np.stack([np.broadcast_to((((r % HG_CHUNK) // SUBLANES) == jb)[:, None], (HG_SUB, HEAD))
                      for jb in range(bpc - 1)])
    cmask = np.stack([np.broadcast_to((chunk == c)[None, :], (HEAD, HG_SUB)) for c in range(nch)])
    eye = np.eye(HEAD)
    return (jnp.asarray(tril, BF16), jnp.asarray(off, F32), jnp.asarray(diag, F32),
            jnp.asarray(spread, BF16), jnp.asarray(kmask, F32), jnp.asarray(cmask, F32), jnp.asarray(eye, BF16))


def _hgrn_front(q, z, v, lb, tril, kmask):
    rows = q.shape[0]
    nch = rows // HG_CHUNK
    nblk = rows // SUBLANES
    bpc = HG_CHUNK // SUBLANES

    log2e = math.log2(math.e)
    soft = jnp.log2(1.0 + jnp.exp2(jnp.abs(z) * -log2e))
    lg_lb = jnp.log2(lb)
    lg_1mlb = jnp.log2(1.0 - lb)
    lg_sig = lg_1mlb + (jnp.minimum(z, 0.0) * log2e - soft)
    lg_k = lg_1mlb - (jnp.maximum(z, 0.0) * log2e + soft)
    logf = jnp.maximum(lg_lb, lg_sig) + jnp.log2(1.0 + jnp.exp2(-jnp.abs(lg_lb - lg_sig)))
    qs = q * (HEAD ** -0.5)

    hi, mid, lo = _split3(logf)
    b = (jnp.dot(tril, hi, preferred_element_type=F32)
         + jnp.dot(tril, mid, preferred_element_type=F32)
         + jnp.dot(tril, lo, preferred_element_type=F32))

    vb = v.astype(BF16)
    c = b - lg_k
    b3 = b.reshape(nch, HG_CHUNK, HEAD)
    b_last = b3[:, HG_CHUNK - 1:HG_CHUNK, :]
    q_in = (qs * jnp.exp2(b)).astype(BF16)
    k_end = jnp.exp2(b_last - c.reshape(nch, HG_CHUNK, HEAD)).reshape(rows, HEAD).astype(BF16)
    decay_cols = jnp.exp2(b_last).reshape(nch, HEAD).T

    qs3 = qs.reshape(nch, HG_CHUNK, HEAD)
    b8 = b.reshape(nblk, SUBLANES, HEAD)
    c8 = c.reshape(nblk, SUBLANES, HEAD)
    k_blk = jnp.exp2(b8[:, SUBLANES - 1:SUBLANES, :] - c8).reshape(rows, HEAD)
    q_groups, k_groups = [], []
    for jb in range(bpc - 1):
        ref = b3[:, jb * SUBLANES + SUBLANES - 1:jb * SUBLANES + SUBLANES, :]
        qg = qs3 * jnp.exp2(jnp.minimum(b3 - ref, 0.0))
        q_groups.append(qg.reshape(rows, HEAD).astype(BF16))
        k_groups.append((k_blk * kmask[jb]).astype(BF16))
    q_cat = jnp.concatenate(q_groups, axis=1)
    k_cat = jnp.concatenate(k_groups, axis=1)

    q8 = qs.reshape(nblk, SUBLANES, HEAD)
    prods = []
    for i in range(SUBLANES):
        e = jnp.exp2(jnp.minimum(b8 - c8[:, i:i + 1, :], 0.0))
        prods.append((e * q8).reshape(rows, HEAD).astype(BF16))
    p_cat = jnp.concatenate(prods, axis=1)
    return q_cat, k_cat, p_cat, q_in, k_end, decay_cols, vb


def _hgrn_back(front, state, off, diag, spread, cmask, eye):
    q_cat, k_cat, p_cat, q_in, k_end, decay_cols, vb = front
    nch = decay_cols.shape[1]
    a = lax.dot_general(q_cat, k_cat, (((1,), (1,)), ((), ())), preferred_element_type=F32) * off
    a = a + jnp.dot(p_cat, spread, preferred_element_type=F32) * diag
    o = jnp.dot(a.astype(BF16), vb, preferred_element_type=F32)

    k_end_t = lax.dot_general(eye, k_end, (((1,), (1,)), ((), ())), preferred_element_type=F32)
    k_stack = jnp.concatenate([(k_end_t * cmask[c]).astype(BF16) for c in range(nch)], axis=0)
    upd = jnp.dot(k_stack, vb, preferred_element_type=F32)

    outs = []
    for c in range(nch):
        outs.append(jnp.dot(q_in[c * HG_CHUNK:(c + 1) * HG_CHUNK], state.astype(BF16), preferred_element_type=F32))
        state = state * decay_cols[:, c:c + 1] + upd[c * HEAD:(c + 1) * HEAD]
    o = o + jnp.concatenate(outs, axis=0)
    return o, state


def _hgrn_body(q_ref, z_ref, v_ref, g_ref, lbp_ref, gain_ref, tril_ref, off_ref, diag_ref, spread_ref, kmask_ref,
               cmask_ref, eye_ref, o_ref, state_ref, *, layer):
    lbp = lbp_ref[...]
    e = jnp.exp(lbp - jnp.max(lbp, axis=0, keepdims=True))
    lb = jnp.sum(e[:layer + 1], axis=0, keepdims=True) / jnp.sum(e, axis=0, keepdims=True)

    state_ref[...] = jnp.zeros_like(state_ref)
    consts = (tril_ref[...], off_ref[...], diag_ref[...], spread_ref[...], kmask_ref[...], cmask_ref[...],
              eye_ref[...])

    def slabs(s, carry):
        state = state_ref[...]
        sls = [pl.ds(pl.multiple_of((s * HG_UNROLL + u) * HG_SUB, HG_SUB), HG_SUB) for u in range(HG_UNROLL)]
        fronts = [_hgrn_front(q_ref[sl, :].astype(F32), z_ref[sl, :], v_ref[sl, :].astype(F32), lb, consts[0],
                              consts[4]) for sl in sls]
        for sl, front in zip(sls, fronts):
            o, state = _hgrn_back(front, state, consts[1], consts[2], consts[3], consts[5], consts[6])
            on = _rms(o, gain_ref[...])
            g = g_ref[sl, :].astype(F32)
            o_ref[sl, :] = (on * (g * _sigmoid(g))).astype(o_ref.dtype)
        state_ref[...] = state
        return carry

    lax.fori_loop(0, q_ref.shape[0] // (HG_SUB * HG_UNROLL), slabs, 0)


def _hgrn(qvg, z, lbp, gain, layer, *, batch, seq, heads):
    n = z.shape[0]
    assert seq % (HG_SUB * HG_UNROLL) == 0
    consts = _hgrn_consts()
    sec = lambda k: pl.BlockSpec((seq, HEAD), lambda b, h: (b, k * heads + h))
    whole = lambda a: pl.BlockSpec(a.shape, lambda b, h: (0,) * a.ndim)
    return pl.pallas_call(
        functools.partial(_hgrn_body, layer=layer),
        grid=(batch, heads),
        in_specs=[sec(0), sec(0), sec(1), sec(2),
                  pl.BlockSpec((lbp.shape[0], HEAD), lambda b, h: (0, h)),
                  pl.BlockSpec((1, HEAD), lambda b, h: (0, 0))] + [whole(a) for a in consts],
        out_specs=pl.BlockSpec((seq, HEAD), lambda b, h: (b, h)),
        out_shape=jax.ShapeDtypeStruct((n, heads * HEAD), BF16),
        scratch_shapes=[pltpu.VMEM((HEAD, HEAD), F32)],
        compiler_params=_params(("parallel", "parallel")),
        name="hgrn",
    )(qvg, z, qvg, qvg, lbp, gain, *consts)


def _attn_body(lam_ref, sub_ref, q_ref, k_ref, v_ref, o_ref, acc_ref, m_ref, l_ref, s0_ref, s1_ref, *, tq, tk,
               lambda_init):
    qi = pl.program_id(2)
    nrep = tk // LANES

    m_ref[...] = jnp.full(m_ref.shape, NEG, F32)
    l_ref[...] = jnp.zeros(l_ref.shape, F32)
    acc_ref[...] = jnp.zeros(acc_ref.shape, F32)

    def scores(j, s_ref):
        kt = k_ref[pl.ds(pl.multiple_of(j * tk, tk), tk), :]
        q = q_ref[...]
        for c in range(2):
            cs = slice(c * HEAD, (c + 1) * HEAD)
            s_ref[c] = lax.dot_general(q[:, cs], kt[:, cs], (((1,), (1,)), ((), ())), preferred_element_type=F32)

    def consume(j, s_ref, masked):
        vt = v_ref[pl.ds(pl.multiple_of(j * tk, tk), tk), :]
        for c in range(2):
            s = s_ref[c]
            if masked:
                ri = lax.broadcasted_iota(jnp.int32, s.shape, 0)
                ci = lax.broadcasted_iota(jnp.int32, s.shape, 1)
                s = jnp.where(ci <= ri, s, NEG)
            m_old = m_ref[c]
            m_new = jnp.maximum(m_old, jnp.max(s, axis=-1, keepdims=True))
            alpha = jnp.exp2(m_old - m_new)
            p = jnp.exp2(s - jnp.concatenate([m_new] * nrep, axis=1))
            l_ref[c] = alpha * l_ref[c] + jnp.sum(p, axis=-1, keepdims=True)
            pv = jnp.dot(p.astype(BF16), vt, preferred_element_type=F32)
            acc_ref[c] = jnp.concatenate([alpha, alpha], axis=1) * acc_ref[c] + pv
            m_ref[c] = m_new

    scores(0, s0_ref)

    def pair(i, carry):
        scores(2 * i + 1, s1_ref)
        consume(2 * i, s0_ref, False)
        scores(2 * i + 2, s0_ref)
        consume(2 * i + 1, s1_ref, False)
        return carry

    lax.fori_loop(0, qi // 2, pair, 0)

    @pl.when(qi % 2 == 1)
    def _():
        scores(qi, s1_ref)
        consume(qi - 1, s0_ref, False)
        consume(qi, s1_ref, True)

    @pl.when(qi % 2 == 0)
    def _():
        consume(qi, s0_ref, True)

    lam = lam_ref[...]
    lam_full = (jnp.exp(jnp.sum(lam[0:1] * lam[1:2], axis=-1, keepdims=True))
                - jnp.exp(jnp.sum(lam[2:3] * lam[3:4], axis=-1, keepdims=True)) + lambda_init)
    inv0 = jnp.concatenate([1.0 / l_ref[0]] * 2, axis=1)
    inv1 = jnp.concatenate([1.0 / l_ref[1]] * 2, axis=1)
    o = acc_ref[0] * inv0 - lam_full * (acc_ref[1] * inv1)
    o_ref[...] = (_rms(o, sub_ref[...]) * (1.0 - lambda_init)).astype(o_ref.dtype)


def _attention(q, kv, lam, subln, *, batch, seq, heads, tq, lambda_init):
    n = q.shape[0]
    vd = 2 * HEAD
    assert seq % tq == 0
    nq = seq // tq
    return pl.pallas_call(
        functools.partial(_attn_body, tq=tq, tk=tq, lambda_init=lambda_init),
        grid=(batch, heads, nq),
        in_specs=[pl.BlockSpec((4, HEAD), lambda b, h, i: (0, 0)),
                  pl.BlockSpec((1, vd), lambda b, h, i: (0, 0)),
                  pl.BlockSpec((tq, vd), lambda b, h, i: (b * nq + i, h)),
                  pl.BlockSpec((seq, vd), lambda b, h, i: (b, h)),
                  pl.BlockSpec((seq, vd), lambda b, h, i: (b, heads + h))],
        out_specs=pl.BlockSpec((tq, vd), lambda b, h, i: (b * nq + i, h)),
        out_shape=jax.ShapeDtypeStruct((n, heads * vd), BF16),
        scratch_shapes=[pltpu.VMEM((2, tq, vd), F32), pltpu.VMEM((2, tq, LANES), F32),
                        pltpu.VMEM((2, tq, LANES), F32), pltpu.VMEM((2, tq, tq), F32), pltpu.VMEM((2, tq, tq), F32)],
        compiler_params=_params(("parallel", "parallel", "arbitrary")),
        name="diff_attn",
    )(lam, subln, q, kv, kv)


def kernel(x, p, ffn_norm, ffn_w_gate_up, ffn_w_down, mix_norm, hgrn_w_in, hgrn_lower_bounds, hgrn_out_norm, hgrn_w_out, kv_norm, w_kv, diff_w_q, diff_lambda, diff_subln, diff_w_out, ple_norm, ple_w_gate, ple_w_proj, final_norm):
    batch, seq, d = x.shape
    depth = p.shape[0]
    n_a = hgrn_w_in.shape[0]
    n = batch * seq
    hg_heads = d // HEAD
    da_heads = d // (2 * HEAD)
    k_cols = da_heads * 2 * HEAD

    tm = min(512, seq)
    tm_big = min(1024, seq)
    tq = min(512, seq)
    wide = hg_heads * HEAD

    wgu = ffn_w_gate_up.astype(BF16)
    wdn = ffn_w_down.astype(BF16)
    w_in = hgrn_w_in.astype(BF16)
    w_ho = hgrn_w_out.astype(BF16)
    w_kvb = w_kv.astype(BF16)
    w_q = diff_w_q.astype(BF16)
    w_do = diff_w_out.astype(BF16)
    w_pg = ple_w_gate.astype(BF16)
    w_pp = ple_w_proj.astype(BF16)
    p2 = p.reshape(depth, n, p.shape[-1])

    rope = _rope_tables(seq)
    h = x.reshape(n, d)
    kv = None
    for i in range(depth):
        h = _ffn(h, ffn_norm[i, 0].reshape(1, d), wgu, wdn, i, 0, tm=tm_big, tf=512)
        if i < n_a:
            gain = mix_norm[i].reshape(1, d)
            z = _norm_mm(h, gain, w_in, (i,), tm=tm, tn=wide, out_dtype=F32, ntiles=1, src_tile=lambda j: j + 1)
            per = wide // 1024
            qvg = _norm_mm(h, gain, w_in, (i,), tm=tm_big, tn=1024, out_dtype=BF16, ntiles=3 * per,
                           src_tile=lambda j: j + jnp.where(j >= per, per, 0))
            og = _hgrn(qvg, z, hgrn_lower_bounds, hgrn_out_norm[i].reshape(1, HEAD), i,
                       batch=batch, seq=seq, heads=hg_heads)
            h = _mm_resid(og, w_ho, (i,), h, tm=tm, tn=d)
        else:
            j = i - n_a
            lambda_init = 0.8 - 0.6 * math.exp(-0.3 * i)
            q = _norm_mm(h, mix_norm[i].reshape(1, d), w_q, (j,), tm=tm, tn=k_cols, out_dtype=BF16,
                         rope=rope, seq=seq, rope_cols=k_cols, scale=HEAD ** -0.5 * math.log2(math.e))
            oa = _attention(q, kv, diff_lambda[j], diff_subln[j].reshape(1, 2 * HEAD),
                            batch=batch, seq=seq, heads=da_heads, tq=tq, lambda_init=lambda_init)
            h = _mm_resid(oa, w_do, (j,), h, tm=tm, tn=d)
        h = _ffn(h, ffn_norm[i, 1].reshape(1, d), wgu, wdn, i, 1, tm=tm_big, tf=512)
        last = i == depth - 1
        h = _ple(h, ple_norm[i].reshape(1, d), w_pg, p2, w_pp, i, final_norm.reshape(1, d),
                 tm=tm, tn=512, final=last)
        if i == n_a - 1:
            kv = _norm_mm(h, kv_norm.reshape(1, d), w_kvb, (), tm=tm_big, tn=k_cols, out_dtype=BF16,
                          rope=rope, seq=seq, rope_cols=k_cols)
    return h.reshape(batch, seq, d)
```

```python
import functools
import math

import jax
import jax.numpy as jnp
import numpy as np
from jax import lax
from jax.experimental import pallas as pl
from jax.experimental.pallas import tpu as pltpu

F32 = jnp.float32
BF16 = jnp.bfloat16

NORM_EPS = 1e-6
ROPE_THETA = 10000.0
HEAD = 128
LANES = 128
SUBLANES = 8
HG_CHUNK = 32
HG_SUB = 256
HG_UNROLL = 4
NEG = -1e30

VMEM_LIMIT = 60 * 1024 * 1024


def _params(sem):
    return pltpu.CompilerParams(dimension_semantics=sem, vmem_limit_bytes=VMEM_LIMIT)


def _rms(x, g):
    ms = jnp.mean(x * x, axis=-1, keepdims=True)
    return x * lax.rsqrt(ms + NORM_EPS) * g


def _sigmoid(x):
    return 1.0 / (1.0 + jnp.exp(-x))


def _ffn_body(x_ref, g_ref, wg_ref, wu_ref, wd_ref, wdl_ref, o_ref, xn_ref, act_ref):
    f = pl.program_id(1)
    last = pl.num_programs(1) - 1

    def gate_up():
        xn = xn_ref[...]
        gate = jnp.dot(xn, wg_ref[...], preferred_element_type=F32)
        up = jnp.dot(xn, wu_ref[...], preferred_element_type=F32)
        return (gate * _sigmoid(gate) * up).astype(BF16)

    @pl.when(f == 0)
    def _():
        x = x_ref[...]
        xn_ref[...] = _rms(x, g_ref[...]).astype(BF16)
        o_ref[...] = x
        act_ref[...] = gate_up()

    @pl.when(f > 0)
    def _():
        part = jnp.dot(act_ref[...], wd_ref[...], preferred_element_type=F32)
        act = gate_up()
        o_ref[...] += 0.5 * part
        act_ref[...] = act

    @pl.when(f == last)
    def _():
        o_ref[...] += 0.5 * jnp.dot(act_ref[...], wdl_ref[...], preferred_element_type=F32)


def _ffn(x, gain, wgu, wd, li, lj, *, tm, tf):
    n, d = x.shape
    ff = wd.shape[2]
    nf = ff // tf
    assert n % tm == 0 and ff % tf == 0 and nf >= 2
    return pl.pallas_call(
        _ffn_body,
        grid=(n // tm, nf),
        in_specs=[
            pl.BlockSpec((tm, d), lambda i, f: (i, 0)),
            pl.BlockSpec((1, d), lambda i, f: (0, 0)),
            pl.BlockSpec((None, None, d, tf), lambda i, f: (li, lj, 0, f)),
            pl.BlockSpec((None, None, d, tf), lambda i, f: (li, lj, 0, f + nf)),
            pl.BlockSpec((None, None, tf, d), lambda i, f: (li, lj, jnp.maximum(f - 1, 0), 0)),
            pl.BlockSpec((None, None, tf, d), lambda i, f: (li, lj, nf - 1, 0), pipeline_mode=pl.Buffered(1)),
        ],
        out_specs=pl.BlockSpec((tm, d), lambda i, f: (i, 0)),
        out_shape=jax.ShapeDtypeStruct((n, d), F32),
        scratch_shapes=[pltpu.VMEM((tm, d), BF16), pltpu.VMEM((tm, tf), BF16)],
        compiler_params=_params(("parallel", "arbitrary")),
        name="ffn",
    )(x, gain, wgu, wgu, wd, wd)


def _rope_store(y, cos, sin, o_ref, scale):
    for g in range(y.shape[1] // HEAD):
        yg = y[:, g * HEAD:(g + 1) * HEAD]
        r = yg * cos + pltpu.roll(yg, HEAD // 2, axis=1) * sin
        if scale != 1.0:
            r = r * scale
        o_ref[:, g * HEAD:(g + 1) * HEAD] = r.astype(o_ref.dtype)


def _norm_mm_body(x_ref, g_ref, w_ref, o_ref, xn_ref):
    @pl.when(pl.program_id(1) == 0)
    def _():
        xn_ref[...] = _rms(x_ref[...], g_ref[...]).astype(BF16)

    o_ref[...] = jnp.dot(xn_ref[...], w_ref[...], preferred_element_type=F32).astype(o_ref.dtype)


def _norm_mm_rope_body(x_ref, g_ref, w_ref, cos_ref, sin_ref, o_ref, xn_ref, *, rope_tiles, scale):
    j = pl.program_id(1)

    @pl.when(j == 0)
    def _():
        xn_ref[...] = _rms(x_ref[...], g_ref[...]).astype(BF16)

    y = jnp.dot(xn_ref[...], w_ref[...], preferred_element_type=F32)

    @pl.when(j < rope_tiles)
    def _():
        _rope_store(y, cos_ref[...], sin_ref[...], o_ref, scale)

    @pl.when(j >= rope_tiles)
    def _():
        o_ref[...] = y.astype(o_ref.dtype)


def _norm_mm(x, gain, w, w_index, *, tm, tn, out_dtype, rope=None, seq=None, rope_cols=0, scale=1.0,
             ntiles=None, src_tile=lambda j: j):
    n, d = x.shape
    nout = (w.shape[-1] // tn if ntiles is None else ntiles) * tn
    assert n % tm == 0 and w.shape[-1] % tn == 0
    nlead = w.ndim - 2
    w_spec = pl.BlockSpec((None,) * nlead + (d, tn), lambda i, j: tuple(w_index) + (0, src_tile(j)))
    in_specs = [pl.BlockSpec((tm, d), lambda i, j: (i, 0)),
                pl.BlockSpec((1, d), lambda i, j: (0, 0)),
                w_spec]
    args = [x, gain, w]
    if rope is None:
        body = _norm_mm_body
    else:
        assert seq % tm == 0 and rope_cols % tn == 0
        nseq = seq // tm
        tab_spec = pl.BlockSpec((tm, HEAD), lambda i, j: (i % nseq, 0))
        in_specs += [tab_spec, tab_spec]
        args += list(rope)
        body = functools.partial(_norm_mm_rope_body, rope_tiles=rope_cols // tn, scale=scale)
    return pl.pallas_call(
        body,
        grid=(n // tm, nout // tn),
        in_specs=in_specs,
        out_specs=pl.BlockSpec((tm, tn), lambda i, j: (i, j)),
        out_shape=jax.ShapeDtypeStruct((n, nout), out_dtype),
        scratch_shapes=[pltpu.VMEM((tm, d), BF16)],
        compiler_params=_params(("parallel", "arbitrary")),
        name="norm_mm",
    )(*args)


def _mm_resid_body(x_ref, w_ref, r_ref, o_ref):
    o_ref[...] = r_ref[...] + jnp.dot(x_ref[...], w_ref[...], preferred_element_type=F32)


def _mm_resid(x, w, w_index, resid, *, tm, tn):
    n, k = x.shape
    nout = w.shape[-1]
    assert n % tm == 0 and nout % tn == 0
    nlead = w.ndim - 2
    return pl.pallas_call(
        _mm_resid_body,
        grid=(n // tm, nout // tn),
        in_specs=[pl.BlockSpec((tm, k), lambda i, j: (i, 0)),
                  pl.BlockSpec((None,) * nlead + (k, tn), lambda i, j: tuple(w_index) + (0, j)),
                  pl.BlockSpec((tm, tn), lambda i, j: (i, j))],
        out_specs=pl.BlockSpec((tm, tn), lambda i, j: (i, j)),
        out_shape=jax.ShapeDtypeStruct((n, nout), F32),
        compiler_params=_params(("parallel", "arbitrary")),
        name="mm_resid",
    )(x, w, resid)


def _ple_body(x_ref, g_ref, wg_ref, p_ref, wp_ref, fg_ref, o_ref, *, tn, final):
    x = x_ref[...]
    xn = _rms(x, g_ref[...]).astype(BF16)
    pb = p_ref[...].astype(BF16)
    d = x.shape[1]
    for c in range(d // tn):
        sl = slice(c * tn, (c + 1) * tn)
        gate = jnp.dot(xn, wg_ref[:, sl], preferred_element_type=F32)
        proj = jnp.dot(pb, wp_ref[:, sl], preferred_element_type=F32)
        o_ref[:, sl] = x_ref[:, sl] + _sigmoid(gate) * proj
    if final:
        o_ref[...] = _rms(o_ref[...], fg_ref[...])


def _ple(x, gain, wg, p, wp, layer, final_gain, *, tm, tn, final):
    n, d = x.shape
    pd = p.shape[-1]
    rows = n // tm
    assert n % tm == 0 and d % tn == 0
    return pl.pallas_call(
        functools.partial(_ple_body, tn=tn, final=final),
        grid=(rows,),
        in_specs=[pl.BlockSpec((tm, d), lambda i: (i, 0)),
                  pl.BlockSpec((1, d), lambda i: (0, 0)),
                  pl.BlockSpec((None, d, d), lambda i: (layer, 0, 0)),
                  pl.BlockSpec((None, tm, pd), lambda i: (layer, i, 0)),
                  pl.BlockSpec((None, pd, d), lambda i: (layer, 0, 0)),
                  pl.BlockSpec((1, d), lambda i: (0, 0))],
        out_specs=pl.BlockSpec((tm, d), lambda i: (i, 0)),
        out_shape=jax.ShapeDtypeStruct((n, d), F32),
        compiler_params=_params(("parallel",)),
        name="ple",
    )(x, gain, wg, p, wp, final_gain)


def _rope_table_body(cos_ref, sin_ref):
    shape = cos_ref.shape
    pos = lax.broadcasted_iota(jnp.int32, shape, 0).astype(F32)
    lane = lax.broadcasted_iota(jnp.int32, shape, 1)
    half = HEAD // 2
    k = jnp.bitwise_and(lane, half - 1).astype(F32)
    inv_freq = jnp.exp(k * (-math.log(ROPE_THETA) / half))
    ang = pos * inv_freq
    cos_ref[...] = jnp.cos(ang)
    s = jnp.sin(ang)
    sin_ref[...] = jnp.where(lane < half, -s, s)


def _rope_tables(seq):
    return pl.pallas_call(
        _rope_table_body,
        out_shape=(jax.ShapeDtypeStruct((seq, HEAD), F32), jax.ShapeDtypeStruct((seq, HEAD), F32)),
        name="rope_tables",
    )()


def _split3(x):
    hi = x.astype(BF16)
    r1 = x - hi.astype(F32)
    mid = r1.astype(BF16)
    lo = (r1 - mid.astype(F32)).astype(BF16)
    return hi, mid, lo


def _hgrn_consts():
    r = np.arange(HG_SUB)
    chunk, blk = r // HG_CHUNK, r // SUBLANES
    nch = HG_SUB // HG_CHUNK
    same_chunk = chunk[:, None] == chunk[None, :]
    tril = same_chunk & (r[None, :] <= r[:, None])
    off = same_chunk & (blk[:, None] > blk[None, :])
    diag = (blk[:, None] == blk[None, :]) & (r[None, :] <= r[:, None])
    lane_blk = np.arange(SUBLANES * HEAD) // HEAD
    spread = lane_blk[:, None] == (r % SUBLANES)[None, :]
    bpc = HG_CHUNK // SUBLANES
    kmask = np.stack([np.broadcast_to((((r % HG_CHUNK) // SUBLANES) == jb)[:, None], (HG_SUB, HEAD))
                      for jb in range(bpc - 1)])
    cmask = np.stack([np.broadcast_to((chunk == c)[None, :], (HEAD, HG_SUB)) for c in range(nch)])
    eye = np.eye(HEAD)
    return (jnp.asarray(tril, BF16), jnp.asarray(off, F32), jnp.asarray(diag, F32),
            jnp.asarray(spread, BF16), jnp.asarray(kmask, F32), jnp.asarray(cmask, F32), jnp.asarray(eye, BF16))


def _hgrn_front(q, z, v, lb, tril, kmask):
    rows = q.shape[0]
    nch = rows // HG_CHUNK
    nblk = rows // SUBLANES
    bpc = HG_CHUNK // SUBLANES

    log2e = math.log2(math.e)
    soft = jnp.log2(1.0 + jnp.exp2(jnp.abs(z) * -log2e))
    lg_lb = jnp.log2(lb)
    lg_1mlb = jnp.log2(1.0 - lb)
    lg_sig = lg_1mlb + (jnp.minimum(z, 0.0) * log2e - soft)
    lg_k = lg_1mlb - (jnp.maximum(z, 0.0) * log2e + soft)
    logf = jnp.maximum(lg_lb, lg_sig) + jnp.log2(1.0 + jnp.exp2(-jnp.abs(lg_lb - lg_sig)))
    qs = q * (HEAD ** -0.5)

    hi, mid, lo = _split3(logf)
    b = (jnp.dot(tril, hi, preferred_element_type=F32)
         + jnp.dot(tril, mid, preferred_element_type=F32)
         + jnp.dot(tril, lo, preferred_element_type=F32))

    vb = v.astype(BF16)
    c = b - lg_k
    b3 = b.reshape(nch, HG_CHUNK, HEAD)
    b_last = b3[:, HG_CHUNK - 1:HG_CHUNK, :]
    q_in = (qs * jnp.exp2(b)).astype(BF16)
    k_end = jnp.exp2(b_last - c.reshape(nch, HG_CHUNK, HEAD)).reshape(rows, HEAD).astype(BF16)
    decay_cols = jnp.exp2(b_last).reshape(nch, HEAD).T

    qs3 = qs.reshape(nch, HG_CHUNK, HEAD)
    b8 = b.reshape(nblk, SUBLANES, HEAD)
    c8 = c.reshape(nblk, SUBLANES, HEAD)
    k_blk = jnp.exp2(b8[:, SUBLANES - 1:SUBLANES, :] - c8).reshape(rows, HEAD)
    q_groups, k_groups = [], []
    for jb in range(bpc - 1):
        ref = b3[:, jb * SUBLANES + SUBLANES - 1:jb * SUBLANES + SUBLANES, :]
        qg = qs3 * jnp.exp2(jnp.minimum(b3 - ref, 0.0))
        q_groups.append(qg.reshape(rows, HEAD).astype(BF16))
        k_groups.append((k_blk * kmask[jb]).astype(BF16))
    q_cat = jnp.concatenate(q_groups, axis=1)
    k_cat = jnp.concatenate(k_groups, axis=1)

    q8 = qs.reshape(nblk, SUBLANES, HEAD)
    prods = []
    for i in range(SUBLANES):
        e = jnp.exp2(jnp.minimum(b8 - c8[:, i:i + 1, :], 0.0))
        prods.append((e * q8).reshape(rows, HEAD).astype(BF16))
    p_cat = jnp.concatenate(prods, axis=1)
    return q_cat, k_cat, p_cat, q_in, k_end, decay_cols, vb


def _hgrn_back(front, state, off, diag, spread, cmask, eye):
    q_cat, k_cat, p_cat, q_in, k_end, decay_cols, vb = front
    nch = decay_cols.shape[1]
    a = lax.dot_general(q_cat, k_cat, (((1,), (1,)), ((), ())), preferred_element_type=F32) * off
    a = a + jnp.dot(p_cat, spread, preferred_element_type=F32) * diag
    o = jnp.dot(a.astype(BF16), vb, preferred_element_type=F32)

    k_end_t = lax.dot_general(eye, k_end, (((1,), (1,)), ((), ())), preferred_element_type=F32)
    k_stack = jnp.concatenate([(k_end_t * cmask[c]).astype(BF16) for c in range(nch)], axis=0)
    upd = jnp.dot(k_stack, vb, preferred_element_type=F32)

    outs = []
    for c in range(nch):
        outs.append(jnp.dot(q_in[c * HG_CHUNK:(c + 1) * HG_CHUNK], state.astype(BF16), preferred_element_type=F32))
        state = state * decay_cols[:, c:c + 1] + upd[c * HEAD:(c + 1) * HEAD]
    o = o + jnp.concatenate(outs, axis=0)
    return o, state


def _hgrn_body(q_ref, z_ref, v_ref, g_ref, lbp_ref, gain_ref, tril_ref, off_ref, diag_ref, spread_ref, kmask_ref,
               cmask_ref, eye_ref, o_ref, state_ref, *, layer):
    lbp = lbp_ref[...]
    e = jnp.exp(lbp - jnp.max(lbp, axis=0, keepdims=True))
    lb = jnp.sum(e[:layer + 1], axis=0, keepdims=True) / jnp.sum(e, axis=0, keepdims=True)

    state_ref[...] = jnp.zeros_like(state_ref)
    consts = (tril_ref[...], off_ref[...], diag_ref[...], spread_ref[...], kmask_ref[...], cmask_ref[...],
              eye_ref[...])

    def slabs(s, carry):
        state = state_ref[...]
        sls = [pl.ds(pl.multiple_of((s * HG_UNROLL + u) * HG_SUB, HG_SUB), HG_SUB) for u in range(HG_UNROLL)]
        fronts = [_hgrn_front(q_ref[sl, :].astype(F32), z_ref[sl, :], v_ref[sl, :].astype(F32), lb, consts[0],
                              consts[4]) for sl in sls]
        for sl, front in zip(sls, fronts):
            o, state = _hgrn_back(front, state, consts[1], consts[2], consts[3], consts[5], consts[6])
            on = _rms(o, gain_ref[...])
            g = g_ref[sl, :].astype(F32)
            o_ref[sl, :] = (on * (g * _sigmoid(g))).astype(o_ref.dtype)
        state_ref[...] = state
        return carry

    lax.fori_loop(0, q_ref.shape[0] // (HG_SUB * HG_UNROLL), slabs, 0)


def _hgrn(qvg, z, lbp, gain, layer, *, batch, seq, heads):
    n = z.shape[0]
    assert seq % (HG_SUB * HG_UNROLL) == 0
    consts = _hgrn_consts()
    sec = lambda k: pl.BlockSpec((seq, HEAD), lambda b, h: (b, k * heads + h))
    whole = lambda a: pl.BlockSpec(a.shape, lambda b, h: (0,) * a.ndim)
    return pl.pallas_call(
        functools.partial(_hgrn_body, layer=layer),
        grid=(batch, heads),
        in_specs=[sec(0), sec(0), sec(1), sec(2),
                  pl.BlockSpec((lbp.shape[0], HEAD), lambda b, h: (0, h)),
                  pl.BlockSpec((1, HEAD), lambda b, h: (0, 0))] + [whole(a) for a in consts],
        out_specs=pl.BlockSpec((seq, HEAD), lambda b, h: (b, h)),
        out_shape=jax.ShapeDtypeStruct((n, heads * HEAD), BF16),
        scratch_shapes=[pltpu.VMEM((HEAD, HEAD), F32)],
        compiler_params=_params(("parallel", "parallel")),
        name="hgrn",
    )(qvg, z, qvg, qvg, lbp, gain, *consts)


def _attn_body(lam_ref, sub_ref, q_ref, k_ref, v_ref, o_ref, acc_ref, m_ref, l_ref, s0_ref, s1_ref, *, tq, tk,
               lambda_init):
    nrep = tk // LANES
    lam = lam_ref[...]
    lam_full = (jnp.exp(jnp.sum(lam[0:1] * lam[1:2], axis=-1, keepdims=True))
                - jnp.exp(jnp.sum(lam[2:3] * lam[3:4], axis=-1, keepdims=True)) + lambda_init)

    def q_tile(qi, carry):
        _attn_q_tile(qi, lam_full, sub_ref, q_ref, k_ref, v_ref, o_ref, acc_ref, m_ref, l_ref, s0_ref, s1_ref,
                     tq=tq, tk=tk, nrep=nrep, lambda_init=lambda_init)
        return carry

    lax.fori_loop(0, q_ref.shape[0] // tq, q_tile, 0)


def _attn_q_tile(qi, lam_full, sub_ref, q_ref, k_ref, v_ref, o_ref, acc_ref, m_ref, l_ref, s0_ref, s1_ref, *, tq, tk,
                 nrep, lambda_init):
    rows = pl.ds(pl.multiple_of(qi * tq, tq), tq)

    m_ref[...] = jnp.full(m_ref.shape, NEG, F32)
    l_ref[...] = jnp.zeros(l_ref.shape, F32)
    acc_ref[...] = jnp.zeros(acc_ref.shape, F32)

    def scores(j, s_ref):
        kt = k_ref[pl.ds(pl.multiple_of(j * tk, tk), tk), :]
        q = q_ref[rows, :]
        for c in range(2):
            cs = slice(c * HEAD, (c + 1) * HEAD)
            s_ref[c] = lax.dot_general(q[:, cs], kt[:, cs], (((1,), (1,)), ((), ())), preferred_element_type=F32)

    def consume(j, s_ref, masked):
        vt = v_ref[pl.ds(pl.multiple_of(j * tk, tk), tk), :]
        for c in range(2):
            s = s_ref[c]
            if masked:
                ri = lax.broadcasted_iota(jnp.int32, s.shape, 0)
                ci = lax.broadcasted_iota(jnp.int32, s.shape, 1)
                s = jnp.where(ci <= ri, s, NEG)
            m_old = m_ref[c]
            m_new = jnp.maximum(m_old, jnp.max(s, axis=-1, keepdims=True))
            alpha = jnp.exp2(m_old - m_new)
            p = jnp.exp2(s - jnp.concatenate([m_new] * nrep, axis=1))
            l_ref[c] = alpha * l_ref[c] + jnp.sum(p, axis=-1, keepdims=True)
            pv = jnp.dot(p.astype(BF16), vt, preferred_element_type=F32)
            acc_ref[c] = jnp.concatenate([alpha, alpha], axis=1) * acc_ref[c] + pv
            m_ref[c] = m_new

    scores(0, s0_ref)

    def pair(i, carry):
        scores(2 * i + 1, s1_ref)
        consume(2 * i, s0_ref, False)
        scores(2 * i + 2, s0_ref)
        consume(2 * i + 1, s1_ref, False)
        return carry

    lax.fori_loop(0, qi // 2, pair, 0)

    @pl.when(qi % 2 == 1)
    def _():
        scores(qi, s1_ref)
        consume(qi - 1, s0_ref, False)
        consume(qi, s1_ref, True)

    @pl.when(qi % 2 == 0)
    def _():
        consume(qi, s0_ref, True)

    inv0 = jnp.concatenate([1.0 / l_ref[0]] * 2, axis=1)
    inv1 = jnp.concatenate([1.0 / l_ref[1]] * 2, axis=1)
    o = acc_ref[0] * inv0 - lam_full * (acc_ref[1] * inv1)
    o_ref[rows, :] = (_rms(o, sub_ref[...]) * (1.0 - lambda_init)).astype(o_ref.dtype)


def _attention(q, kv, lam, subln, *, batch, seq, heads, tq, lambda_init):
    n = q.shape[0]
    vd = 2 * HEAD
    assert seq % tq == 0
    return pl.pallas_call(
        functools.partial(_attn_body, tq=tq, tk=tq, lambda_init=lambda_init),
        grid=(batch, heads),
        in_specs=[pl.BlockSpec((4, HEAD), lambda b, h: (0, 0)),
                  pl.BlockSpec((1, vd), lambda b, h: (0, 0)),
                  pl.BlockSpec((seq, vd), lambda b, h: (b, h)),
                  pl.BlockSpec((seq, vd), lambda b, h: (b, h)),
                  pl.BlockSpec((seq, vd), lambda b, h: (b, heads + h))],
        out_specs=pl.BlockSpec((seq, vd), lambda b, h: (b, h)),
        out_shape=jax.ShapeDtypeStruct((n, heads * vd), BF16),
        scratch_shapes=[pltpu.VMEM((2, tq, vd), F32), pltpu.VMEM((2, tq, LANES), F32),
                        pltpu.VMEM((2, tq, LANES), F32), pltpu.VMEM((2, tq, tq), F32), pltpu.VMEM((2, tq, tq), F32)],
        compiler_params=_params(("parallel", "parallel")),
        name="diff_attn",
    )(lam, subln, q, kv, kv)


def kernel(x, p, ffn_norm, ffn_w_gate_up, ffn_w_down, mix_norm, hgrn_w_in, hgrn_lower_bounds, hgrn_out_norm, hgrn_w_out, kv_norm, w_kv, diff_w_q, diff_lambda, diff_subln, diff_w_out, ple_norm, ple_w_gate, ple_w_proj, final_norm):
    batch, seq, d = x.shape
    depth = p.shape[0]
    n_a = hgrn_w_in.shape[0]
    n = batch * seq
    hg_heads = d // HEAD
    da_heads = d // (2 * HEAD)
    k_cols = da_heads * 2 * HEAD

    tm = min(512, seq)
    tm_big = min(1024, seq)
    tq = min(512, seq)
    wide = hg_heads * HEAD

    wgu = ffn_w_gate_up.astype(BF16)
    wdn = ffn_w_down.astype(BF16)
    w_in = hgrn_w_in.astype(BF16)
    w_ho = hgrn_w_out.astype(BF16)
    w_kvb = w_kv.astype(BF16)
    w_q = diff_w_q.astype(BF16)
    w_do = diff_w_out.astype(BF16)
    w_pg = ple_w_gate.astype(BF16)
    w_pp = ple_w_proj.astype(BF16)
    p2 = p.reshape(depth, n, p.shape[-1])

    rope = _rope_tables(seq)
    h = x.reshape(n, d)
    kv = None
    for i in range(depth):
        h = _ffn(h, ffn_norm[i, 0].reshape(1, d), wgu, wdn, i, 0, tm=tm_big, tf=512)
        if i < n_a:
            gain = mix_norm[i].reshape(1, d)
            z = _norm_mm(h, gain, w_in, (i,), tm=tm, tn=wide, out_dtype=F32, ntiles=1, src_tile=lambda j: j + 1)
            per = wide // 1024
            qvg = _norm_mm(h, gain, w_in, (i,), tm=tm_big, tn=1024, out_dtype=BF16, ntiles=3 * per,
                           src_tile=lambda j: j + jnp.where(j >= per, per, 0))
            og = _hgrn(qvg, z, hgrn_lower_bounds, hgrn_out_norm[i].reshape(1, HEAD), i,
                       batch=batch, seq=seq, heads=hg_heads)
            h = _mm_resid(og, w_ho, (i,), h, tm=tm, tn=d)
        else:
            j = i - n_a
            lambda_init = 0.8 - 0.6 * math.exp(-0.3 * i)
            q = _norm_mm(h, mix_norm[i].reshape(1, d), w_q, (j,), tm=tm, tn=k_cols, out_dtype=BF16,
                         rope=rope, seq=seq, rope_cols=k_cols, scale=HEAD ** -0.5 * math.log2(math.e))
            oa = _attention(q, kv, diff_lambda[j], diff_subln[j].reshape(1, 2 * HEAD),
                            batch=batch, seq=seq, heads=da_heads, tq=tq, lambda_init=lambda_init)
            h = _mm_resid(oa, w_do, (j,), h, tm=tm, tn=d)
        h = _ffn(h, ffn_norm[i, 1].reshape(1, d), wgu, wdn, i, 1, tm=tm_big, tf=512)
        last = i == depth - 1
        h = _ple(h, ple_norm[i].reshape(1, d), w_pg, p2, w_pp, i, final_norm.reshape(1, d),
                 tm=tm, tn=512, final=last)
        if i == n_a - 1:
            kv = _norm_mm(h, kv_norm.reshape(1, d), w_kvb, (), tm=tm_big, tn=k_cols, out_dtype=BF16,
                          rope=rope, seq=seq, rope_cols=k_cols)
    return h.reshape(batch, seq, d)
```

```python
import functools
import math

import jax
import jax.numpy as jnp
import numpy as np
from jax import lax
from jax.experimental import pallas as pl
from jax.experimental.pallas import tpu as pltpu

F32 = jnp.float32
BF16 = jnp.bfloat16

NORM_EPS = 1e-6
ROPE_THETA = 10000.0
HEAD = 128
LANES = 128
SUBLANES = 8
HG_CHUNK = 32
HG_SUB = 256
HG_UNROLL = 4
NEG = -1e30

VMEM_LIMIT = 60 * 1024 * 1024


def _params(sem):
    return pltpu.CompilerParams(dimension_semantics=sem, vmem_limit_bytes=VMEM_LIMIT)


def _rms(x, g):
    ms = jnp.mean(x * x, axis=-1, keepdims=True)
    return x * lax.rsqrt(ms + NORM_EPS) * g


def _sigmoid(x):
    return 1.0 / (1.0 + jnp.exp(-x))


def _ffn_body(x_ref, g_ref, wgu_ref, wd_ref, wdl_ref, o_ref, xn_ref, act_ref, *, tf):
    f = pl.program_id(1)
    last = pl.num_programs(1) - 1

    def gate_up():
        xn = xn_ref[...]
        gate = jnp.dot(xn, wgu_ref[:, :tf], preferred_element_type=F32)
        up = jnp.dot(xn, wgu_ref[:, tf:], preferred_element_type=F32)
        return (gate * _sigmoid(gate) * up).astype(BF16)

    @pl.when(f == 0)
    def _():
        x = x_ref[...]
        xn_ref[...] = _rms(x, g_ref[...]).astype(BF16)
        o_ref[...] = x
        act_ref[...] = gate_up()

    @pl.when(f > 0)
    def _():
        part = jnp.dot(act_ref[...], wd_ref[...], preferred_element_type=F32)
        act = gate_up()
        o_ref[...] += 0.5 * part
        act_ref[...] = act

    @pl.when(f == last)
    def _():
        o_ref[...] += 0.5 * jnp.dot(act_ref[...], wdl_ref[...], preferred_element_type=F32)


def _ffn(x, gain, wgu, wd, li, lj, *, tm):
    n, d = x.shape
    nf, tf = wgu.shape[2], wgu.shape[4] // 2
    assert n % tm == 0 and wd.shape[2] == nf * tf and nf >= 2
    return pl.pallas_call(
        functools.partial(_ffn_body, tf=tf),
        grid=(n // tm, nf),
        in_specs=[
            pl.BlockSpec((tm, d), lambda i, f: (i, 0)),
            pl.BlockSpec((1, d), lambda i, f: (0, 0)),
            pl.BlockSpec((None, None, None, d, 2 * tf), lambda i, f: (li, lj, f, 0, 0)),
            pl.BlockSpec((None, None, tf, d), lambda i, f: (li, lj, jnp.maximum(f - 1, 0), 0)),
            pl.BlockSpec((None, None, tf, d), lambda i, f: (li, lj, nf - 1, 0), pipeline_mode=pl.Buffered(1)),
        ],
        out_specs=pl.BlockSpec((tm, d), lambda i, f: (i, 0)),
        out_shape=jax.ShapeDtypeStruct((n, d), F32),
        scratch_shapes=[pltpu.VMEM((tm, d), BF16), pltpu.VMEM((tm, tf), BF16)],
        compiler_params=_params(("parallel", "arbitrary")),
        name="ffn",
    )(x, gain, wgu, wd, wd)


def _rope_store(y, cos, sin, o_ref, scale):
    for g in range(y.shape[1] // HEAD):
        yg = y[:, g * HEAD:(g + 1) * HEAD]
        r = yg * cos + pltpu.roll(yg, HEAD // 2, axis=1) * sin
        if scale != 1.0:
            r = r * scale
        o_ref[:, g * HEAD:(g + 1) * HEAD] = r.astype(o_ref.dtype)


def _norm_mm_body(x_ref, g_ref, w_ref, o_ref, xn_ref):
    @pl.when(pl.program_id(1) == 0)
    def _():
        xn_ref[...] = _rms(x_ref[...], g_ref[...]).astype(BF16)

    o_ref[...] = jnp.dot(xn_ref[...], w_ref[...], preferred_element_type=F32).astype(o_ref.dtype)


def _norm_mm_rope_body(x_ref, g_ref, w_ref, cos_ref, sin_ref, o_ref, xn_ref, *, rope_tiles, scale):
    j = pl.program_id(1)

    @pl.when(j == 0)
    def _():
        xn_ref[...] = _rms(x_ref[...], g_ref[...]).astype(BF16)

    y = jnp.dot(xn_ref[...], w_ref[...], preferred_element_type=F32)

    @pl.when(j < rope_tiles)
    def _():
        _rope_store(y, cos_ref[...], sin_ref[...], o_ref, scale)

    @pl.when(j >= rope_tiles)
    def _():
        o_ref[...] = y.astype(o_ref.dtype)


def _norm_mm(x, gain, w, w_index, *, tm, tn, out_dtype, rope=None, seq=None, rope_cols=0, scale=1.0,
             ntiles=None, src_tile=lambda j: j):
    n, d = x.shape
    nout = (w.shape[-1] // tn if ntiles is None else ntiles) * tn
    assert n % tm == 0 and w.shape[-1] % tn == 0
    nlead = w.ndim - 2
    w_spec = pl.BlockSpec((None,) * nlead + (d, tn), lambda i, j: tuple(w_index) + (0, src_tile(j)))
    in_specs = [pl.BlockSpec((tm, d), lambda i, j: (i, 0)),
                pl.BlockSpec((1, d), lambda i, j: (0, 0)),
                w_spec]
    args = [x, gain, w]
    if rope is None:
        body = _norm_mm_body
    else:
        assert seq % tm == 0 and rope_cols % tn == 0
        nseq = seq // tm
        tab_spec = pl.BlockSpec((tm, HEAD), lambda i, j: (i % nseq, 0))
        in_specs += [tab_spec, tab_spec]
        args += list(rope)
        body = functools.partial(_norm_mm_rope_body, rope_tiles=rope_cols // tn, scale=scale)
    return pl.pallas_call(
        body,
        grid=(n // tm, nout // tn),
        in_specs=in_specs,
        out_specs=pl.BlockSpec((tm, tn), lambda i, j: (i, j)),
        out_shape=jax.ShapeDtypeStruct((n, nout), out_dtype),
        scratch_shapes=[pltpu.VMEM((tm, d), BF16)],
        compiler_params=_params(("parallel", "arbitrary")),
        name="norm_mm",
    )(*args)


def _mm_resid_body(x_ref, w_ref, r_ref, o_ref):
    o_ref[...] = r_ref[...] + jnp.dot(x_ref[...], w_ref[...], preferred_element_type=F32)


def _mm_resid(x, w, w_index, resid, *, tm, tn):
    n, k = x.shape
    nout = w.shape[-1]
    assert n % tm == 0 and nout % tn == 0
    nlead = w.ndim - 2
    return pl.pallas_call(
        _mm_resid_body,
        grid=(n // tm, nout // tn),
        in_specs=[pl.BlockSpec((tm, k), lambda i, j: (i, 0)),
                  pl.BlockSpec((None,) * nlead + (k, tn), lambda i, j: tuple(w_index) + (0, j)),
                  pl.BlockSpec((tm, tn), lambda i, j: (i, j))],
        out_specs=pl.BlockSpec((tm, tn), lambda i, j: (i, j)),
        out_shape=jax.ShapeDtypeStruct((n, nout), F32),
        compiler_params=_params(("parallel", "arbitrary")),
        name="mm_resid",
    )(x, w, resid)


def _ple_body(x_ref, g_ref, wg_ref, p_ref, wp_ref, fg_ref, o_ref, *, tn, final):
    x = x_ref[...]
    xn = _rms(x, g_ref[...]).astype(BF16)
    pb = p_ref[...].astype(BF16)
    d = x.shape[1]
    for c in range(d // tn):
        sl = slice(c * tn, (c + 1) * tn)
        gate = jnp.dot(xn, wg_ref[:, sl], preferred_element_type=F32)
        proj = jnp.dot(pb, wp_ref[:, sl], preferred_element_type=F32)
        o_ref[:, sl] = x_ref[:, sl] + _sigmoid(gate) * proj
    if final:
        o_ref[...] = _rms(o_ref[...], fg_ref[...])


def _ple(x, gain, wg, p, wp, layer, final_gain, *, tm, tn, final):
    n, d = x.shape
    pd = p.shape[-1]
    rows = n // tm
    assert n % tm == 0 and d % tn == 0
    return pl.pallas_call(
        functools.partial(_ple_body, tn=tn, final=final),
        grid=(rows,),
        in_specs=[pl.BlockSpec((tm, d), lambda i: (i, 0)),
                  pl.BlockSpec((1, d), lambda i: (0, 0)),
                  pl.BlockSpec((None, d, d), lambda i: (layer, 0, 0)),
                  pl.BlockSpec((None, tm, pd), lambda i: (layer, i, 0)),
                  pl.BlockSpec((None, pd, d), lambda i: (layer, 0, 0)),
                  pl.BlockSpec((1, d), lambda i: (0, 0))],
        out_specs=pl.BlockSpec((tm, d), lambda i: (i, 0)),
        out_shape=jax.ShapeDtypeStruct((n, d), F32),
        compiler_params=_params(("parallel",)),
        name="ple",
    )(x, gain, wg, p, wp, final_gain)


def _rope_table_body(cos_ref, sin_ref):
    shape = cos_ref.shape
    pos = lax.broadcasted_iota(jnp.int32, shape, 0).astype(F32)
    lane = lax.broadcasted_iota(jnp.int32, shape, 1)
    half = HEAD // 2
    k = jnp.bitwise_and(lane, half - 1).astype(F32)
    inv_freq = jnp.exp(k * (-math.log(ROPE_THETA) / half))
    ang = pos * inv_freq
    cos_ref[...] = jnp.cos(ang)
    s = jnp.sin(ang)
    sin_ref[...] = jnp.where(lane < half, -s, s)


def _rope_tables(seq):
    return pl.pallas_call(
        _rope_table_body,
        out_shape=(jax.ShapeDtypeStruct((seq, HEAD), F32), jax.ShapeDtypeStruct((seq, HEAD), F32)),
        name="rope_tables",
    )()


def _split3(x):
    hi = x.astype(BF16)
    r1 = x - hi.astype(F32)
    mid = r1.astype(BF16)
    lo = (r1 - mid.astype(F32)).astype(BF16)
    return hi, mid, lo


def _hgrn_consts():
    r = np.arange(HG_SUB)
    chunk, blk = r // HG_CHUNK, r // SUBLANES
    nch = HG_SUB // HG_CHUNK
    same_chunk = chunk[:, None] == chunk[None, :]
    tril = same_chunk & (r[None, :] <= r[:, None])
    off = same_chunk & (blk[:, None] > blk[None, :])
    diag = (blk[:, None] == blk[None, :]) & (r[None, :] <= r[:, None])
    lane_blk = np.arange(SUBLANES * HEAD) // HEAD
    spread = lane_blk[:, None] == (r % SUBLANES)[None, :]
    bpc = HG_CHUNK // SUBLANES
    kmask = np.stack([np.broadcast_to((((r % HG_CHUNK) // SUBLANES) == jb)[:, None], (HG_SUB, HEAD))
                      for jb in range(bpc - 1)])
    cmask = np.stack([np.broadcast_to((chunk == c)[None, :], (HEAD, HG_SUB)) for c in range(nch)])
    eye = np.eye(HEAD)
    return (jnp.asarray(tril, BF16), jnp.asarray(off, F32), jnp.asarray(diag, F32),
            jnp.asarray(spread, BF16), jnp.asarray(kmask, F32), jnp.asarray(cmask, F32), jnp.asarray(eye, BF16))


def _hgrn_front(q, z, v, lb, tril, kmask):
    rows = q.shape[0]
    nch = rows // HG_CHUNK
    nblk = rows // SUBLANES
    bpc = HG_CHUNK // SUBLANES

    log2e = math.log2(math.e)
    soft = jnp.log2(1.0 + jnp.exp2(jnp.abs(z) * -log2e))
    lg_lb = jnp.log2(lb)
    lg_1mlb = jnp.log2(1.0 - lb)
    lg_sig = lg_1mlb + (jnp.minimum(z, 0.0) * log2e - soft)
    lg_k = lg_1mlb - (jnp.maximum(z, 0.0) * log2e + soft)
    logf = jnp.maximum(lg_lb, lg_sig) + jnp.log2(1.0 + jnp.exp2(-jnp.abs(lg_lb - lg_sig)))
    qs = q * (HEAD ** -0.5)

    hi, mid, lo = _split3(logf)
    b = (jnp.dot(tril, hi, preferred_element_type=F32)
         + jnp.dot(tril, mid, preferred_element_type=F32)
         + jnp.dot(tril, lo, preferred_element_type=F32))

    vb = v.astype(BF16)
    c = b - lg_k
    b3 = b.reshape(nch, HG_CHUNK, HEAD)
    b_last = b3[:, HG_CHUNK - 1:HG_CHUNK, :]
    q_in = (qs * jnp.exp2(b)).astype(BF16)
    k_end = jnp.exp2(b_last - c.reshape(nch, HG_CHUNK, HEAD)).reshape(rows, HEAD).astype(BF16)
    decay_cols = jnp.exp2(b_last).reshape(nch, HEAD).T

    qs3 = qs.reshape(nch, HG_CHUNK, HEAD)
    b8 = b.reshape(nblk, SUBLANES, HEAD)
    c8 = c.reshape(nblk, SUBLANES, HEAD)
    k_blk = jnp.exp2(b8[:, SUBLANES - 1:SUBLANES, :] - c8).reshape(rows, HEAD)
    q_groups, k_groups = [], []
    for jb in range(bpc - 1):
        ref = b3[:, jb * SUBLANES + SUBLANES - 1:jb * SUBLANES + SUBLANES, :]
        qg = qs3 * jnp.exp2(jnp.minimum(b3 - ref, 0.0))
        q_groups.append(qg.reshape(rows, HEAD).astype(BF16))
        k_groups.append((k_blk * kmask[jb]).astype(BF16))
    q_cat = jnp.concatenate(q_groups, axis=1)
    k_cat = jnp.concatenate(k_groups, axis=1)

    q8 = qs.reshape(nblk, SUBLANES, HEAD)
    prods = []
    for i in range(SUBLANES):
        e = jnp.exp2(jnp.minimum(b8 - c8[:, i:i + 1, :], 0.0))
        prods.append((e * q8).reshape(rows, HEAD).astype(BF16))
    p_cat = jnp.concatenate(prods, axis=1)
    return q_cat, k_cat, p_cat, q_in, k_end, decay_cols, vb


def _hgrn_back(front, state, off, diag, spread, cmask, eye):
    q_cat, k_cat, p_cat, q_in, k_end, decay_cols, vb = front
    nch = decay_cols.shape[1]
    a = lax.dot_general(q_cat, k_cat, (((1,), (1,)), ((), ())), preferred_element_type=F32) * off
    a = a + jnp.dot(p_cat, spread, preferred_element_type=F32) * diag
    o = jnp.dot(a.astype(BF16), vb, preferred_element_type=F32)

    k_end_t = lax.dot_general(eye, k_end, (((1,), (1,)), ((), ())), preferred_element_type=F32)
    k_stack = jnp.concatenate([(k_end_t * cmask[c]).astype(BF16) for c in range(nch)], axis=0)
    upd = jnp.dot(k_stack, vb, preferred_element_type=F32)

    outs = []
    for c in range(nch):
        outs.append(jnp.dot(q_in[c * HG_CHUNK:(c + 1) * HG_CHUNK], state.astype(BF16), preferred_element_type=F32))
        state = state * decay_cols[:, c:c + 1] + upd[c * HEAD:(c + 1) * HEAD]
    o = o + jnp.concatenate(outs, axis=0)
    return o, state


def _hgrn_body(q_ref, z_ref, v_ref, g_ref, lbp_ref, gain_ref, tril_ref, off_ref, diag_ref, spread_ref, kmask_ref,
               cmask_ref, eye_ref, o_ref, state_ref, *, layer):
    lbp = lbp_ref[...]
    e = jnp.exp(lbp - jnp.max(lbp, axis=0, keepdims=True))
    lb = jnp.sum(e[:layer + 1], axis=0, keepdims=True) / jnp.sum(e, axis=0, keepdims=True)

    state_ref[...] = jnp.zeros_like(state_ref)
    consts = (tril_ref[...], off_ref[...], diag_ref[...], spread_ref[...], kmask_ref[...], cmask_ref[...],
              eye_ref[...])

    def slabs(s, carry):
        state = state_ref[...]
        sls = [pl.ds(pl.multiple_of((s * HG_UNROLL + u) * HG_SUB, HG_SUB), HG_SUB) for u in range(HG_UNROLL)]
        fronts = [_hgrn_front(q_ref[sl, :].astype(F32), z_ref[sl, :], v_ref[sl, :].astype(F32), lb, consts[0],
                              consts[4]) for sl in sls]
        for sl, front in zip(sls, fronts):
            o, state = _hgrn_back(front, state, consts[1], consts[2], consts[3], consts[5], consts[6])
            on = _rms(o, gain_ref[...])
            g = g_ref[sl, :].astype(F32)
            o_ref[sl, :] = (on * (g * _sigmoid(g))).astype(o_ref.dtype)
        state_ref[...] = state
        return carry

    lax.fori_loop(0, q_ref.shape[0] // (HG_SUB * HG_UNROLL), slabs, 0)


def _hgrn(qvg, z, lbp, gain, layer, *, batch, seq, heads):
    n = z.shape[0]
    assert seq % (HG_SUB * HG_UNROLL) == 0
    consts = _hgrn_consts()
    sec = lambda k: pl.BlockSpec((seq, HEAD), lambda b, h: (b, k * heads + h))
    whole = lambda a: pl.BlockSpec(a.shape, lambda b, h: (0,) * a.ndim)
    return pl.pallas_call(
        functools.partial(_hgrn_body, layer=layer),
        grid=(batch, heads),
        in_specs=[sec(0), sec(0), sec(1), sec(2),
                  pl.BlockSpec((lbp.shape[0], HEAD), lambda b, h: (0, h)),
                  pl.BlockSpec((1, HEAD), lambda b, h: (0, 0))] + [whole(a) for a in consts],
        out_specs=pl.BlockSpec((seq, HEAD), lambda b, h: (b, h)),
        out_shape=jax.ShapeDtypeStruct((n, heads * HEAD), BF16),
        scratch_shapes=[pltpu.VMEM((HEAD, HEAD), F32)],
        compiler_params=_params(("parallel", "parallel")),
        name="hgrn",
    )(qvg, z, qvg, qvg, lbp, gain, *consts)


def _attn_body(lam_ref, sub_ref, q_ref, k_ref, v_ref, o_ref, acc_ref, m_ref, l_ref, s0_ref, s1_ref, *, tq, tk,
               lambda_init):
    nrep = tk // LANES
    lam = lam_ref[...]
    lam_full = (jnp.exp(jnp.sum(lam[0:1] * lam[1:2], axis=-1, keepdims=True))
                - jnp.exp(jnp.sum(lam[2:3] * lam[3:4], axis=-1, keepdims=True)) + lambda_init)

    def q_tile(qi, carry):
        _attn_q_tile(qi, lam_full, sub_ref, q_ref, k_ref, v_ref, o_ref, acc_ref, m_ref, l_ref, s0_ref, s1_ref,
                     tq=tq, tk=tk, nrep=nrep, lambda_init=lambda_init)
        return carry

    lax.fori_loop(0, q_ref.shape[0] // tq, q_tile, 0)


def _attn_q_tile(qi, lam_full, sub_ref, q_ref, k_ref, v_ref, o_ref, acc_ref, m_ref, l_ref, s0_ref, s1_ref, *, tq, tk,
                 nrep, lambda_init):
    rows = pl.ds(pl.multiple_of(qi * tq, tq), tq)

    m_ref[...] = jnp.full(m_ref.shape, NEG, F32)
    l_ref[...] = jnp.zeros(l_ref.shape, F32)
    acc_ref[...] = jnp.zeros(acc_ref.shape, F32)

    def scores(j, s_ref):
        kt = k_ref[pl.ds(pl.multiple_of(j * tk, tk), tk), :]
        q = q_ref[rows, :]
        for c in range(2):
            cs = slice(c * HEAD, (c + 1) * HEAD)
            s_ref[c] = lax.dot_general(q[:, cs], kt[:, cs], (((1,), (1,)), ((), ())), preferred_element_type=F32)

    def consume(j, s_ref, masked):
        vt = v_ref[pl.ds(pl.multiple_of(j * tk, tk), tk), :]
        for c in range(2):
            s = s_ref[c]
            if masked:
                ri = lax.broadcasted_iota(jnp.int32, s.shape, 0)
                ci = lax.broadcasted_iota(jnp.int32, s.shape, 1)
                s = jnp.where(ci <= ri, s, NEG)
            m_old = m_ref[c]
            m_new = jnp.maximum(m_old, jnp.max(s, axis=-1, keepdims=True))
            alpha = jnp.exp2(m_old - m_new)
            p = jnp.exp2(s - jnp.concatenate([m_new] * nrep, axis=1))
            l_ref[c] = alpha * l_ref[c] + jnp.sum(p, axis=-1, keepdims=True)
            pv = jnp.dot(p.astype(BF16), vt, preferred_element_type=F32)
            acc_ref[c] = jnp.concatenate([alpha, alpha], axis=1) * acc_ref[c] + pv
            m_ref[c] = m_new

    scores(0, s0_ref)

    def pair(i, carry):
        scores(2 * i + 1, s1_ref)
        consume(2 * i, s0_ref, False)
        scores(2 * i + 2, s0_ref)
        consume(2 * i + 1, s1_ref, False)
        return carry

    lax.fori_loop(0, qi // 2, pair, 0)

    @pl.when(qi % 2 == 1)
    def _():
        scores(qi, s1_ref)
        consume(qi - 1, s0_ref, False)
        consume(qi, s1_ref, True)

    @pl.when(qi % 2 == 0)
    def _():
        consume(qi, s0_ref, True)

    inv0 = jnp.concatenate([1.0 / l_ref[0]] * 2, axis=1)
    inv1 = jnp.concatenate([1.0 / l_ref[1]] * 2, axis=1)
    o = acc_ref[0] * inv0 - lam_full * (acc_ref[1] * inv1)
    o_ref[rows, :] = (_rms(o, sub_ref[...]) * (1.0 - lambda_init)).astype(o_ref.dtype)


def _attention(q, kv, lam, subln, *, batch, seq, heads, tq, lambda_init):
    n = q.shape[0]
    vd = 2 * HEAD
    assert seq % tq == 0
    return pl.pallas_call(
        functools.partial(_attn_body, tq=tq, tk=tq, lambda_init=lambda_init),
        grid=(batch, heads),
        in_specs=[pl.BlockSpec((4, HEAD), lambda b, h: (0, 0)),
                  pl.BlockSpec((1, vd), lambda b, h: (0, 0)),
                  pl.BlockSpec((seq, vd), lambda b, h: (b, h)),
                  pl.BlockSpec((seq, vd), lambda b, h: (b, h)),
                  pl.BlockSpec((seq, vd), lambda b, h: (b, heads + h))],
        out_specs=pl.BlockSpec((seq, vd), lambda b, h: (b, h)),
        out_shape=jax.ShapeDtypeStruct((n, heads * vd), BF16),
        scratch_shapes=[pltpu.VMEM((2, tq, vd), F32), pltpu.VMEM((2, tq, LANES), F32),
                        pltpu.VMEM((2, tq, LANES), F32), pltpu.VMEM((2, tq, tq), F32), pltpu.VMEM((2, tq, tq), F32)],
        compiler_params=_params(("parallel", "parallel")),
        name="diff_attn",
    )(lam, subln, q, kv, kv)


def kernel(x, p, ffn_norm, ffn_w_gate_up, ffn_w_down, mix_norm, hgrn_w_in, hgrn_lower_bounds, hgrn_out_norm, hgrn_w_out, kv_norm, w_kv, diff_w_q, diff_lambda, diff_subln, diff_w_out, ple_norm, ple_w_gate, ple_w_proj, final_norm):
    batch, seq, d = x.shape
    depth = p.shape[0]
    n_a = hgrn_w_in.shape[0]
    n = batch * seq
    hg_heads = d // HEAD
    da_heads = d // (2 * HEAD)
    k_cols = da_heads * 2 * HEAD

    tm = min(512, seq)
    tm_big = min(1024, seq)
    tq = min(512, seq)
    wide = hg_heads * HEAD

    tf = 512
    nf = ffn_w_down.shape[2] // tf
    wgu = (ffn_w_gate_up.reshape(depth, 2, d, 2, nf, tf).transpose(0, 1, 4, 2, 3, 5)
           .reshape(depth, 2, nf, d, 2 * tf).astype(BF16))
    wdn = ffn_w_down.astype(BF16)
    w_in = hgrn_w_in.astype(BF16)
    w_ho = hgrn_w_out.astype(BF16)
    w_kvb = w_kv.astype(BF16)
    w_q = diff_w_q.astype(BF16)
    w_do = diff_w_out.astype(BF16)
    w_pg = ple_w_gate.astype(BF16)
    w_pp = ple_w_proj.astype(BF16)
    p2 = p.reshape(depth, n, p.shape[-1])

    rope = _rope_tables(seq)
    h = x.reshape(n, d)
    kv = None
    for i in range(depth):
        h = _ffn(h, ffn_norm[i, 0].reshape(1, d), wgu, wdn, i, 0, tm=tm_big)
        if i < n_a:
            gain = mix_norm[i].reshape(1, d)
            z = _norm_mm(h, gain, w_in, (i,), tm=tm, tn=wide, out_dtype=F32, ntiles=1, src_tile=lambda j: j + 1)
            per = wide // 1024
            qvg = _norm_mm(h, gain, w_in, (i,), tm=tm_big, tn=1024, out_dtype=BF16, ntiles=3 * per,
                           src_tile=lambda j: j + jnp.where(j >= per, per, 0))
            og = _hgrn(qvg, z, hgrn_lower_bounds, hgrn_out_norm[i].reshape(1, HEAD), i,
                       batch=batch, seq=seq, heads=hg_heads)
            h = _mm_resid(og, w_ho, (i,), h, tm=tm, tn=d)
        else:
            j = i - n_a
            lambda_init = 0.8 - 0.6 * math.exp(-0.3 * i)
            q = _norm_mm(h, mix_norm[i].reshape(1, d), w_q, (j,), tm=tm, tn=k_cols, out_dtype=BF16,
                         rope=rope, seq=seq, rope_cols=k_cols, scale=HEAD ** -0.5 * math.log2(math.e))
            oa = _attention(q, kv, diff_lambda[j], diff_subln[j].reshape(1, 2 * HEAD),
                            batch=batch, seq=seq, heads=da_heads, tq=tq, lambda_init=lambda_init)
            h = _mm_resid(oa, w_do, (j,), h, tm=tm, tn=d)
        h = _ffn(h, ffn_norm[i, 1].reshape(1, d), wgu, wdn, i, 1, tm=tm_big)
        last = i == depth - 1
        h = _ple(h, ple_norm[i].reshape(1, d), w_pg, p2, w_pp, i, final_norm.reshape(1, d),
                 tm=tm, tn=512, final=last)
        if i == n_a - 1:
            kv = _norm_mm(h, kv_norm.reshape(1, d), w_kvb, (), tm=tm_big, tn=k_cols, out_dtype=BF16,
                          rope=rope, seq=seq, rope_cols=k_cols)
    return h.reshape(batch, seq, d)
```

```python
import functools
import math

import jax
import jax.numpy as jnp
import numpy as np
from jax import lax
from jax.experimental import pallas as pl
from jax.experimental.pallas import tpu as pltpu

F32 = jnp.float32
BF16 = jnp.bfloat16

NORM_EPS = 1e-6
ROPE_THETA = 10000.0
HEAD = 128
LANES = 128
SUBLANES = 8
HG_CHUNK = 32
HG_SUB = 256
HG_UNROLL = 4
NEG = -1e30

VMEM_LIMIT = 60 * 1024 * 1024


def _params(sem):
    return pltpu.CompilerParams(dimension_semantics=sem, vmem_limit_bytes=VMEM_LIMIT)


def _rms(x, g):
    ms = jnp.mean(x * x, axis=-1, keepdims=True)
    return x * lax.rsqrt(ms + NORM_EPS) * g


def _sigmoid(x):
    return 1.0 / (1.0 + jnp.exp(-x))


def _ffn_body(x_ref, g_ref, wg_ref, wu_ref, wd_ref, wdl_ref, o_ref, xn_ref, act_ref):
    f = pl.program_id(1)
    last = pl.num_programs(1) - 1

    def gate_up():
        xn = xn_ref[...]
        gate = jnp.dot(xn, wg_ref[...], preferred_element_type=F32)
        up = jnp.dot(xn, wu_ref[...], preferred_element_type=F32)
        return (gate * _sigmoid(gate) * up).astype(BF16)

    @pl.when(f == 0)
    def _():
        x = x_ref[...]
        xn_ref[...] = _rms(x, g_ref[...]).astype(BF16)
        o_ref[...] = x
        act_ref[...] = gate_up()

    @pl.when(f > 0)
    def _():
        part = jnp.dot(act_ref[...], wd_ref[...], preferred_element_type=F32)
        act = gate_up()
        o_ref[...] += 0.5 * part
        act_ref[...] = act

    @pl.when(f == last)
    def _():
        o_ref[...] += 0.5 * jnp.dot(act_ref[...], wdl_ref[...], preferred_element_type=F32)


def _ffn(x, gain, wgu, wd, li, lj, *, tm, tf):
    n, d = x.shape
    ff = wd.shape[2]
    nf = ff // tf
    assert n % tm == 0 and ff % tf == 0 and nf >= 2
    return pl.pallas_call(
        _ffn_body,
        grid=(n // tm, nf),
        in_specs=[
            pl.BlockSpec((tm, d), lambda i, f: (i, 0)),
            pl.BlockSpec((1, d), lambda i, f: (0, 0)),
            pl.BlockSpec((None, None, d, tf), lambda i, f: (li, lj, 0, f)),
            pl.BlockSpec((None, None, d, tf), lambda i, f: (li, lj, 0, f + nf)),
            pl.BlockSpec((None, None, tf, d), lambda i, f: (li, lj, jnp.maximum(f - 1, 0), 0)),
            pl.BlockSpec((None, None, tf, d), lambda i, f: (li, lj, nf - 1, 0), pipeline_mode=pl.Buffered(1)),
        ],
        out_specs=pl.BlockSpec((tm, d), lambda i, f: (i, 0)),
        out_shape=jax.ShapeDtypeStruct((n, d), F32),
        scratch_shapes=[pltpu.VMEM((tm, d), BF16), pltpu.VMEM((tm, tf), BF16)],
        compiler_params=_params(("parallel", "arbitrary")),
        name="ffn",
    )(x, gain, wgu, wgu, wd, wd)


def _rope_store(y, cos, sin, o_ref, scale):
    for g in range(y.shape[1] // HEAD):
        yg = y[:, g * HEAD:(g + 1) * HEAD]
        r = yg * cos + pltpu.roll(yg, HEAD // 2, axis=1) * sin
        if scale != 1.0:
            r = r * scale
        o_ref[:, g * HEAD:(g + 1) * HEAD] = r.astype(o_ref.dtype)


def _norm_mm_body(x_ref, g_ref, w_ref, o_ref, xn_ref):
    @pl.when(pl.program_id(1) == 0)
    def _():
        xn_ref[...] = _rms(x_ref[...], g_ref[...]).astype(BF16)

    o_ref[...] = jnp.dot(xn_ref[...], w_ref[...], preferred_element_type=F32).astype(o_ref.dtype)


def _norm_mm_rope_body(x_ref, g_ref, w_ref, cos_ref, sin_ref, o_ref, xn_ref, *, rope_tiles, scale):
    j = pl.program_id(1)

    @pl.when(j == 0)
    def _():
        xn_ref[...] = _rms(x_ref[...], g_ref[...]).astype(BF16)

    y = jnp.dot(xn_ref[...], w_ref[...], preferred_element_type=F32)

    @pl.when(j < rope_tiles)
    def _():
        _rope_store(y, cos_ref[...], sin_ref[...], o_ref, scale)

    @pl.when(j >= rope_tiles)
    def _():
        o_ref[...] = y.astype(o_ref.dtype)


def _norm_mm(x, gain, w, w_index, *, tm, tn, out_dtype, rope=None, seq=None, rope_cols=0, scale=1.0,
             ntiles=None, src_tile=lambda j: j):
    n, d = x.shape
    nout = (w.shape[-1] // tn if ntiles is None else ntiles) * tn
    assert n % tm == 0 and w.shape[-1] % tn == 0
    nlead = w.ndim - 2
    w_spec = pl.BlockSpec((None,) * nlead + (d, tn), lambda i, j: tuple(w_index) + (0, src_tile(j)))
    in_specs = [pl.BlockSpec((tm, d), lambda i, j: (i, 0)),
                pl.BlockSpec((1, d), lambda i, j: (0, 0)),
                w_spec]
    args = [x, gain, w]
    if rope is None:
        body = _norm_mm_body
    else:
        assert seq % tm == 0 and rope_cols % tn == 0
        nseq = seq // tm
        tab_spec = pl.BlockSpec((tm, HEAD), lambda i, j: (i % nseq, 0))
        in_specs += [tab_spec, tab_spec]
        args += list(rope)
        body = functools.partial(_norm_mm_rope_body, rope_tiles=rope_cols // tn, scale=scale)
    return pl.pallas_call(
        body,
        grid=(n // tm, nout // tn),
        in_specs=in_specs,
        out_specs=pl.BlockSpec((tm, tn), lambda i, j: (i, j)),
        out_shape=jax.ShapeDtypeStruct((n, nout), out_dtype),
        scratch_shapes=[pltpu.VMEM((tm, d), BF16)],
        compiler_params=_params(("parallel", "arbitrary")),
        name="norm_mm",
    )(*args)


def _mm_resid_body(x_ref, w_ref, r_ref, o_ref):
    o_ref[...] = r_ref[...] + jnp.dot(x_ref[...], w_ref[...], preferred_element_type=F32)


def _mm_resid(x, w, w_index, resid, *, tm, tn):
    n, k = x.shape
    nout = w.shape[-1]
    assert n % tm == 0 and nout % tn == 0
    nlead = w.ndim - 2
    return pl.pallas_call(
        _mm_resid_body,
        grid=(n // tm, nout // tn),
        in_specs=[pl.BlockSpec((tm, k), lambda i, j: (i, 0)),
                  pl.BlockSpec((None,) * nlead + (k, tn), lambda i, j: tuple(w_index) + (0, j)),
                  pl.BlockSpec((tm, tn), lambda i, j: (i, j))],
        out_specs=pl.BlockSpec((tm, tn), lambda i, j: (i, j)),
        out_shape=jax.ShapeDtypeStruct((n, nout), F32),
        compiler_params=_params(("parallel", "arbitrary")),
        name="mm_resid",
    )(x, w, resid)


def _ple_body(x_ref, g_ref, wg_ref, p_ref, wp_ref, fg_ref, o_ref, *, tn, final):
    x = x_ref[...]
    xn = _rms(x, g_ref[...]).astype(BF16)
    pb = p_ref[...].astype(BF16)
    d = x.shape[1]
    for c in range(d // tn):
        sl = slice(c * tn, (c + 1) * tn)
        gate = jnp.dot(xn, wg_ref[:, sl], preferred_element_type=F32)
        proj = jnp.dot(pb, wp_ref[:, sl], preferred_element_type=F32)
        o_ref[:, sl] = x_ref[:, sl] + _sigmoid(gate) * proj
    if final:
        o_ref[...] = _rms(o_ref[...], fg_ref[...])


def _ple(x, gain, wg, p, wp, layer, final_gain, *, tm, tn, final):
    n, d = x.shape
    pd = p.shape[-1]
    rows = n // tm
    assert n % tm == 0 and d % tn == 0
    return pl.pallas_call(
        functools.partial(_ple_body, tn=tn, final=final),
        grid=(rows,),
        in_specs=[pl.BlockSpec((tm, d), lambda i: (i, 0)),
                  pl.BlockSpec((1, d), lambda i: (0, 0)),
                  pl.BlockSpec((None, d, d), lambda i: (layer, 0, 0)),
                  pl.BlockSpec((None, tm, pd), lambda i: (layer, i, 0)),
                  pl.BlockSpec((None, pd, d), lambda i: (layer, 0, 0)),
                  pl.BlockSpec((1, d), lambda i: (0, 0))],
        out_specs=pl.BlockSpec((tm, d), lambda i: (i, 0)),
        out_shape=jax.ShapeDtypeStruct((n, d), F32),
        compiler_params=_params(("parallel",)),
        name="ple",
    )(x, gain, wg, p, wp, final_gain)


def _rope_table_body(cos_ref, sin_ref):
    shape = cos_ref.shape
    pos = lax.broadcasted_iota(jnp.int32, shape, 0).astype(F32)
    lane = lax.broadcasted_iota(jnp.int32, shape, 1)
    half = HEAD // 2
    k = jnp.bitwise_and(lane, half - 1).astype(F32)
    inv_freq = jnp.exp(k * (-math.log(ROPE_THETA) / half))
    ang = pos * inv_freq
    cos_ref[...] = jnp.cos(ang)
    s = jnp.sin(ang)
    sin_ref[...] = jnp.where(lane < half, -s, s)


def _rope_tables(seq):
    return pl.pallas_call(
        _rope_table_body,
        out_shape=(jax.ShapeDtypeStruct((seq, HEAD), F32), jax.ShapeDtypeStruct((seq, HEAD), F32)),
        name="rope_tables",
    )()


def _split3(x):
    hi = x.astype(BF16)
    r1 = x - hi.astype(F32)
    mid = r1.astype(BF16)
    lo = (r1 - mid.astype(F32)).astype(BF16)
    return hi, mid, lo


def _hgrn_consts():
    r = np.arange(HG_SUB)
    chunk, blk = r // HG_CHUNK, r // SUBLANES
    nch = HG_SUB // HG_CHUNK
    same_chunk = chunk[:, None] == chunk[None, :]
    tril = same_chunk & (r[None, :] <= r[:, None])
    off = same_chunk & (blk[:, None] > blk[None, :])
    diag = (blk[:, None] == blk[None, :]) & (r[None, :] <= r[:, None])
    lane_blk = np.arange(SUBLANES * HEAD) // HEAD
    spread = lane_blk[:, None] == (r % SUBLANES)[None, :]
    bpc = HG_CHUNK // SUBLANES
    kmask = np.stack([np.broadcast_to((((r % HG_CHUNK) // SUBLANES) == jb)[:, None], (HG_SUB, HEAD))
                      for jb in range(bpc - 1)])
    cmask = np.stack([np.broadcast_to((chunk == c)[None, :], (HEAD, HG_SUB)) for c in range(nch)])
    eye = np.eye(HEAD)
    return (jnp.asarray(tril, BF16), jnp.asarray(off, F32), jnp.asarray(diag, F32),
            jnp.asarray(spread, BF16), jnp.asarray(kmask, F32), jnp.asarray(cmask, F32), jnp.asarray(eye, BF16))


def _hgrn_front(q, z, v, lb, tril, kmask):
    rows = q.shape[0]
    nch = rows // HG_CHUNK
    nblk = rows // SUBLANES
    bpc = HG_CHUNK // SUBLANES

    log2e = math.log2(math.e)
    soft = jnp.log2(1.0 + jnp.exp2(jnp.abs(z) * -log2e))
    lg_lb = jnp.log2(lb)
    lg_1mlb = jnp.log2(1.0 - lb)
    lg_sig = lg_1mlb + (jnp.minimum(z, 0.0) * log2e - soft)
    lg_k = lg_1mlb - (jnp.maximum(z, 0.0) * log2e + soft)
    logf = jnp.maximum(lg_lb, lg_sig) + jnp.log2(1.0 + jnp.exp2(-jnp.abs(lg_lb - lg_sig)))
    qs = q * (HEAD ** -0.5)

    hi, mid, lo = _split3(logf)
    b = (jnp.dot(tril, hi, preferred_element_type=F32)
         + jnp.dot(tril, mid, preferred_element_type=F32)
         + jnp.dot(tril, lo, preferred_element_type=F32))

    vb = v.astype(BF16)
    c = b - lg_k
    b3 = b.reshape(nch, HG_CHUNK, HEAD)
    b_last = b3[:, HG_CHUNK - 1:HG_CHUNK, :]
    q_in = (qs * jnp.exp2(b)).astype(BF16)
    k_end = jnp.exp2(b_last - c.reshape(nch, HG_CHUNK, HEAD)).reshape(rows, HEAD).astype(BF16)
    decay_cols = jnp.exp2(b_last).reshape(nch, HEAD).T

    qs3 = qs.reshape(nch, HG_CHUNK, HEAD)
    b8 = b.reshape(nblk, SUBLANES, HEAD)
    c8 = c.reshape(nblk, SUBLANES, HEAD)
    k_blk = jnp.exp2(b8[:, SUBLANES - 1:SUBLANES, :] - c8).reshape(rows, HEAD)
    q_groups, k_groups = [], []
    for jb in range(bpc - 1):
        ref = b3[:, jb * SUBLANES + SUBLANES - 1:jb * SUBLANES + SUBLANES, :]
        qg = qs3 * jnp.exp2(jnp.minimum(b3 - ref, 0.0))
        q_groups.append(qg.reshape(rows, HEAD).astype(BF16))
        k_groups.append((k_blk * kmask[jb]).astype(BF16))
    q_cat = jnp.concatenate(q_groups, axis=1)
    k_cat = jnp.concatenate(k_groups, axis=1)

    q8 = qs.reshape(nblk, SUBLANES, HEAD)
    prods = []
    for i in range(SUBLANES):
        e = jnp.exp2(jnp.minimum(b8 - c8[:, i:i + 1, :], 0.0))
        prods.append((e * q8).reshape(rows, HEAD).astype(BF16))
    p_cat = jnp.concatenate(prods, axis=1)
    return q_cat, k_cat, p_cat, q_in, k_end, decay_cols, vb


def _hgrn_back(front, state, off, diag, spread, cmask, eye):
    q_cat, k_cat, p_cat, q_in, k_end, decay_cols, vb = front
    nch = decay_cols.shape[1]
    a = lax.dot_general(q_cat, k_cat, (((1,), (1,)), ((), ())), preferred_element_type=F32) * off
    a = a + jnp.dot(p_cat, spread, preferred_element_type=F32) * diag
    o = jnp.dot(a.astype(BF16), vb, preferred_element_type=F32)

    k_end_t = lax.dot_general(eye, k_end, (((1,), (1,)), ((), ())), preferred_element_type=F32)
    k_stack = jnp.concatenate([(k_end_t * cmask[c]).astype(BF16) for c in range(nch)], axis=0)
    upd = jnp.dot(k_stack, vb, preferred_element_type=F32)

    outs = []
    for c in range(nch):
        outs.append(jnp.dot(q_in[c * HG_CHUNK:(c + 1) * HG_CHUNK], state.astype(BF16), preferred_element_type=F32))
        state = state * decay_cols[:, c:c + 1] + upd[c * HEAD:(c + 1) * HEAD]
    o = o + jnp.concatenate(outs, axis=0)
    return o, state


def _hgrn_body(q_ref, z_ref, v_ref, g_ref, lbp_ref, gain_ref, tril_ref, off_ref, diag_ref, spread_ref, kmask_ref,
               cmask_ref, eye_ref, o_ref, state_ref, *, layer):
    lbp = lbp_ref[...]
    e = jnp.exp(lbp - jnp.max(lbp, axis=0, keepdims=True))
    lb = jnp.sum(e[:layer + 1], axis=0, keepdims=True) / jnp.sum(e, axis=0, keepdims=True)

    state_ref[...] = jnp.zeros_like(state_ref)
    consts = (tril_ref[...], off_ref[...], diag_ref[...], spread_ref[...], kmask_ref[...], cmask_ref[...],
              eye_ref[...])

    def slabs(s, carry):
        state = state_ref[...]
        sls = [pl.ds(pl.multiple_of((s * HG_UNROLL + u) * HG_SUB, HG_SUB), HG_SUB) for u in range(HG_UNROLL)]
        fronts = [_hgrn_front(q_ref[sl, :].astype(F32), z_ref[sl, :], v_ref[sl, :].astype(F32), lb, consts[0],
                              consts[4]) for sl in sls]
        for sl, front in zip(sls, fronts):
            o, state = _hgrn_back(front, state, consts[1], consts[2], consts[3], consts[5], consts[6])
            on = _rms(o, gain_ref[...])
            g = g_ref[sl, :].astype(F32)
            o_ref[sl, :] = (on * (g * _sigmoid(g))).astype(o_ref.dtype)
        state_ref[...] = state
        return carry

    lax.fori_loop(0, q_ref.shape[0] // (HG_SUB * HG_UNROLL), slabs, 0)


def _hgrn(qvg, z, lbp, gain, layer, *, batch, seq, heads):
    n = z.shape[0]
    assert seq % (HG_SUB * HG_UNROLL) == 0
    consts = _hgrn_consts()
    sec = lambda k: pl.BlockSpec((seq, HEAD), lambda b, h: (b, k * heads + h))
    whole = lambda a: pl.BlockSpec(a.shape, lambda b, h: (0,) * a.ndim)
    return pl.pallas_call(
        functools.partial(_hgrn_body, layer=layer),
        grid=(batch, heads),
        in_specs=[sec(0), sec(0), sec(1), sec(2),
                  pl.BlockSpec((lbp.shape[0], HEAD), lambda b, h: (0, h)),
                  pl.BlockSpec((1, HEAD), lambda b, h: (0, 0))] + [whole(a) for a in consts],
        out_specs=pl.BlockSpec((seq, HEAD), lambda b, h: (b, h)),
        out_shape=jax.ShapeDtypeStruct((n, heads * HEAD), BF16),
        scratch_shapes=[pltpu.VMEM((HEAD, HEAD), F32)],
        compiler_params=_params(("parallel", "parallel")),
        name="hgrn",
    )(qvg, z, qvg, qvg, lbp, gain, *consts)


def _attn_body(lam_ref, sub_ref, q_ref, k_ref, v_ref, o_ref, acc_ref, m_ref, l_ref, s0_ref, s1_ref, *, tq, tk,
               lambda_init):
    nrep = tk // LANES
    lam = lam_ref[...]
    lam_full = (jnp.exp(jnp.sum(lam[0:1] * lam[1:2], axis=-1, keepdims=True))
                - jnp.exp(jnp.sum(lam[2:3] * lam[3:4], axis=-1, keepdims=True)) + lambda_init)

    def q_tile(qi, carry):
        _attn_q_tile(qi, lam_full, sub_ref, q_ref, k_ref, v_ref, o_ref, acc_ref, m_ref, l_ref, s0_ref, s1_ref,
                     tq=tq, tk=tk, nrep=nrep, lambda_init=lambda_init)
        return carry

    lax.fori_loop(0, q_ref.shape[0] // tq, q_tile, 0)


def _attn_q_tile(qi, lam_full, sub_ref, q_ref, k_ref, v_ref, o_ref, acc_ref, m_ref, l_ref, s0_ref, s1_ref, *, tq, tk,
                 nrep, lambda_init):
    rows = pl.ds(pl.multiple_of(qi * tq, tq), tq)

    m_ref[...] = jnp.full(m_ref.shape, NEG, F32)
    l_ref[...] = jnp.zeros(l_ref.shape, F32)
    acc_ref[...] = jnp.zeros(acc_ref.shape, F32)

    def scores(j, s_ref):
        kt = k_ref[pl.ds(pl.multiple_of(j * tk, tk), tk), :]
        q = q_ref[rows, :]
        for c in range(2):
            cs = slice(c * HEAD, (c + 1) * HEAD)
            s_ref[c] = lax.dot_general(q[:, cs], kt[:, cs], (((1,), (1,)), ((), ())), preferred_element_type=F32)

    def consume(j, s_ref, masked):
        vt = v_ref[pl.ds(pl.multiple_of(j * tk, tk), tk), :]
        ps, alphas = [], []
        for c in range(2):
            s = s_ref[c]
            if masked:
                ri = lax.broadcasted_iota(jnp.int32, s.shape, 0)
                ci = lax.broadcasted_iota(jnp.int32, s.shape, 1)
                s = jnp.where(ci <= ri, s, NEG)
            m_old = m_ref[c]
            m_new = jnp.maximum(m_old, jnp.max(s, axis=-1, keepdims=True))
            alpha = jnp.exp2(m_old - m_new)
            p = jnp.exp2(s - jnp.concatenate([m_new] * nrep, axis=1))
            l_ref[c] = alpha * l_ref[c] + jnp.sum(p, axis=-1, keepdims=True)
            m_ref[c] = m_new
            ps.append(p.astype(BF16))
            alphas.append(alpha)
        pv = jnp.dot(jnp.concatenate(ps, axis=0), vt, preferred_element_type=F32)
        for c in range(2):
            acc_ref[c] = jnp.concatenate([alphas[c]] * 2, axis=1) * acc_ref[c] + pv[c * tq:(c + 1) * tq]

    scores(0, s0_ref)

    def pair(i, carry):
        scores(2 * i + 1, s1_ref)
        consume(2 * i, s0_ref, False)
        scores(2 * i + 2, s0_ref)
        consume(2 * i + 1, s1_ref, False)
        return carry

    lax.fori_loop(0, qi // 2, pair, 0)

    @pl.when(qi % 2 == 1)
    def _():
        scores(qi, s1_ref)
        consume(qi - 1, s0_ref, False)
        consume(qi, s1_ref, True)

    @pl.when(qi % 2 == 0)
    def _():
        consume(qi, s0_ref, True)

    inv0 = jnp.concatenate([1.0 / l_ref[0]] * 2, axis=1)
    inv1 = jnp.concatenate([1.0 / l_ref[1]] * 2, axis=1)
    o = acc_ref[0] * inv0 - lam_full * (acc_ref[1] * inv1)
    o_ref[rows, :] = (_rms(o, sub_ref[...]) * (1.0 - lambda_init)).astype(o_ref.dtype)


def _attention(q, kv, lam, subln, *, batch, seq, heads, tq, lambda_init):
    n = q.shape[0]
    vd = 2 * HEAD
    assert seq % tq == 0
    return pl.pallas_call(
        functools.partial(_attn_body, tq=tq, tk=tq, lambda_init=lambda_init),
        grid=(batch, heads),
        in_specs=[pl.BlockSpec((4, HEAD), lambda b, h: (0, 0)),
                  pl.BlockSpec((1, vd), lambda b, h: (0, 0)),
                  pl.BlockSpec((seq, vd), lambda b, h: (b, h)),
                  pl.BlockSpec((seq, vd), lambda b, h: (b, h)),
                  pl.BlockSpec((seq, vd), lambda b, h: (b, heads + h))],
        out_specs=pl.BlockSpec((seq, vd), lambda b, h: (b, h)),
        out_shape=jax.ShapeDtypeStruct((n, heads * vd), BF16),
        scratch_shapes=[pltpu.VMEM((2, tq, vd), F32), pltpu.VMEM((2, tq, LANES), F32),
                        pltpu.VMEM((2, tq, LANES), F32), pltpu.VMEM((2, tq, tq), F32), pltpu.VMEM((2, tq, tq), F32)],
        compiler_params=_params(("parallel", "parallel")),
        name="diff_attn",
    )(lam, subln, q, kv, kv)


def kernel(x, p, ffn_norm, ffn_w_gate_up, ffn_w_down, mix_norm, hgrn_w_in, hgrn_lower_bounds, hgrn_out_norm, hgrn_w_out, kv_norm, w_kv, diff_w_q, diff_lambda, diff_subln, diff_w_out, ple_norm, ple_w_gate, ple_w_proj, final_norm):
    batch, seq, d = x.shape
    depth = p.shape[0]
    n_a = hgrn_w_in.shape[0]
    n = batch * seq
    hg_heads = d // HEAD
    da_heads = d // (2 * HEAD)
    k_cols = da_heads * 2 * HEAD

    tm = min(512, seq)
    tm_big = min(1024, seq)
    tq = min(512, seq)
    wide = hg_heads * HEAD

    wgu = ffn_w_gate_up.astype(BF16)
    wdn = ffn_w_down.astype(BF16)
    w_in = hgrn_w_in.astype(BF16)
    w_ho = hgrn_w_out.astype(BF16)
    w_kvb = w_kv.astype(BF16)
    w_q = diff_w_q.astype(BF16)
    w_do = diff_w_out.astype(BF16)
    w_pg = ple_w_gate.astype(BF16)
    w_pp = ple_w_proj.astype(BF16)
    p2 = p.reshape(depth, n, p.shape[-1])

    rope = _rope_tables(seq)
    h = x.reshape(n, d)
    kv = None
    for i in range(depth):
        h = _ffn(h, ffn_norm[i, 0].reshape(1, d), wgu, wdn, i, 0, tm=tm_big, tf=512)
        if i < n_a:
            gain = mix_norm[i].reshape(1, d)
            z = _norm_mm(h, gain, w_in, (i,), tm=tm, tn=wide, out_dtype=F32, ntiles=1, src_tile=lambda j: j + 1)
            per = wide // 1024
            qvg = _norm_mm(h, gain, w_in, (i,), tm=tm_big, tn=1024, out_dtype=BF16, ntiles=3 * per,
                           src_tile=lambda j: j + jnp.where(j >= per, per, 0))
            og = _hgrn(qvg, z, hgrn_lower_bounds, hgrn_out_norm[i].reshape(1, HEAD), i,
                       batch=batch, seq=seq, heads=hg_heads)
            h = _mm_resid(og, w_ho, (i,), h, tm=tm, tn=d)
        else:
            j = i - n_a
            lambda_init = 0.8 - 0.6 * math.exp(-0.3 * i)
            q = _norm_mm(h, mix_norm[i].reshape(1, d), w_q, (j,), tm=tm, tn=k_cols, out_dtype=BF16,
                         rope=rope, seq=seq, rope_cols=k_cols, scale=HEAD ** -0.5 * math.log2(math.e))
            oa = _attention(q, kv, diff_lambda[j], diff_subln[j].reshape(1, 2 * HEAD),
                            batch=batch, seq=seq, heads=da_heads, tq=tq, lambda_init=lambda_init)
            h = _mm_resid(oa, w_do, (j,), h, tm=tm, tn=d)
        h = _ffn(h, ffn_norm[i, 1].reshape(1, d), wgu, wdn, i, 1, tm=tm_big, tf=512)
        last = i == depth - 1
        h = _ple(h, ple_norm[i].reshape(1, d), w_pg, p2, w_pp, i, final_norm.reshape(1, d),
                 tm=tm, tn=512, final=last)
        if i == n_a - 1:
            kv = _norm_mm(h, kv_norm.reshape(1, d), w_kvb, (), tm=tm_big, tn=k_cols, out_dtype=BF16,
                          rope=rope, seq=seq, rope_cols=k_cols)
    return h.reshape(batch, seq, d)
```

```python
import functools
import math

import jax
import jax.numpy as jnp
import numpy as np
from jax import lax
from jax.experimental import pallas as pl
from jax.experimental.pallas import tpu as pltpu

F32 = jnp.float32
BF16 = jnp.bfloat16

NORM_EPS = 1e-6
ROPE_THETA = 10000.0
HEAD = 128
LANES = 128
SUBLANES = 8
HG_CHUNK = 32
HG_SUB = 256
HG_UNROLL = 8
NEG = -1e30

VMEM_LIMIT = 60 * 1024 * 1024


def _params(sem):
    return pltpu.CompilerParams(dimension_semantics=sem, vmem_limit_bytes=VMEM_LIMIT)


def _rms(x, g):
    ms = jnp.mean(x * x, axis=-1, keepdims=True)
    return x * lax.rsqrt(ms + NORM_EPS) * g


def _sigmoid(x):
    return 1.0 / (1.0 + jnp.exp(-x))


def _ffn_body(x_ref, g_ref, wg_ref, wu_ref, wd_ref, wdl_ref, o_ref, xn_ref, act_ref):
    f = pl.program_id(1)
    last = pl.num_programs(1) - 1

    def gate_up():
        xn = xn_ref[...]
        gate = jnp.dot(xn, wg_ref[...], preferred_element_type=F32)
        up = jnp.dot(xn, wu_ref[...], preferred_element_type=F32)
        return (gate * _sigmoid(gate) * up).astype(BF16)

    @pl.when(f == 0)
    def _():
        x = x_ref[...]
        xn_ref[...] = _rms(x, g_ref[...]).astype(BF16)
        o_ref[...] = x
        act_ref[...] = gate_up()

    @pl.when(f > 0)
    def _():
        part = jnp.dot(act_ref[...], wd_ref[...], preferred_element_type=F32)
        act = gate_up()
        o_ref[...] += 0.5 * part
        act_ref[...] = act

    @pl.when(f == last)
    def _():
        o_ref[...] += 0.5 * jnp.dot(act_ref[...], wdl_ref[...], preferred_element_type=F32)


def _ffn(x, gain, wgu, wd, li, lj, *, tm, tf):
    n, d = x.shape
    ff = wd.shape[2]
    nf = ff // tf
    assert n % tm == 0 and ff % tf == 0 and nf >= 2
    return pl.pallas_call(
        _ffn_body,
        grid=(n // tm, nf),
        in_specs=[
            pl.BlockSpec((tm, d), lambda i, f: (i, 0)),
            pl.BlockSpec((1, d), lambda i, f: (0, 0)),
            pl.BlockSpec((None, None, d, tf), lambda i, f: (li, lj, 0, f)),
            pl.BlockSpec((None, None, d, tf), lambda i, f: (li, lj, 0, f + nf)),
            pl.BlockSpec((None, None, tf, d), lambda i, f: (li, lj, jnp.maximum(f - 1, 0), 0)),
            pl.BlockSpec((None, None, tf, d), lambda i, f: (li, lj, nf - 1, 0), pipeline_mode=pl.Buffered(1)),
        ],
        out_specs=pl.BlockSpec((tm, d), lambda i, f: (i, 0)),
        out_shape=jax.ShapeDtypeStruct((n, d), F32),
        scratch_shapes=[pltpu.VMEM((tm, d), BF16), pltpu.VMEM((tm, tf), BF16)],
        compiler_params=_params(("parallel", "arbitrary")),
        name="ffn",
    )(x, gain, wgu, wgu, wd, wd)


def _rope_store(y, cos, sin, o_ref, scale):
    for g in range(y.shape[1] // HEAD):
        yg = y[:, g * HEAD:(g + 1) * HEAD]
        r = yg * cos + pltpu.roll(yg, HEAD // 2, axis=1) * sin
        if scale != 1.0:
            r = r * scale
        o_ref[:, g * HEAD:(g + 1) * HEAD] = r.astype(o_ref.dtype)


def _norm_mm_body(x_ref, g_ref, w_ref, o_ref, xn_ref):
    @pl.when(pl.program_id(1) == 0)
    def _():
        xn_ref[...] = _rms(x_ref[...], g_ref[...]).astype(BF16)

    o_ref[...] = jnp.dot(xn_ref[...], w_ref[...], preferred_element_type=F32).astype(o_ref.dtype)


def _norm_mm_rope_body(x_ref, g_ref, w_ref, cos_ref, sin_ref, o_ref, xn_ref, *, rope_tiles, scale):
    j = pl.program_id(1)

    @pl.when(j == 0)
    def _():
        xn_ref[...] = _rms(x_ref[...], g_ref[...]).astype(BF16)

    y = jnp.dot(xn_ref[...], w_ref[...], preferred_element_type=F32)

    @pl.when(j < rope_tiles)
    def _():
        _rope_store(y, cos_ref[...], sin_ref[...], o_ref, scale)

    @pl.when(j >= rope_tiles)
    def _():
        o_ref[...] = y.astype(o_ref.dtype)


def _norm_mm(x, gain, w, w_index, *, tm, tn, out_dtype, rope=None, seq=None, rope_cols=0, scale=1.0,
             ntiles=None, src_tile=lambda j: j):
    n, d = x.shape
    nout = (w.shape[-1] // tn if ntiles is None else ntiles) * tn
    assert n % tm == 0 and w.shape[-1] % tn == 0
    nlead = w.ndim - 2
    w_spec = pl.BlockSpec((None,) * nlead + (d, tn), lambda i, j: tuple(w_index) + (0, src_tile(j)))
    in_specs = [pl.BlockSpec((tm, d), lambda i, j: (i, 0)),
                pl.BlockSpec((1, d), lambda i, j: (0, 0)),
                w_spec]
    args = [x, gain, w]
    if rope is None:
        body = _norm_mm_body
    else:
        assert seq % tm == 0 and rope_cols % tn == 0
        nseq = seq // tm
        tab_spec = pl.BlockSpec((tm, HEAD), lambda i, j: (i % nseq, 0))
        in_specs += [tab_spec, tab_spec]
        args += list(rope)
        body = functools.partial(_norm_mm_rope_body, rope_tiles=rope_cols // tn, scale=scale)
    return pl.pallas_call(
        body,
        grid=(n // tm, nout // tn),
        in_specs=in_specs,
        out_specs=pl.BlockSpec((tm, tn), lambda i, j: (i, j)),
        out_shape=jax.ShapeDtypeStruct((n, nout), out_dtype),
        scratch_shapes=[pltpu.VMEM((tm, d), BF16)],
        compiler_params=_params(("parallel", "arbitrary")),
        name="norm_mm",
    )(*args)


def _mm_resid_body(x_ref, w_ref, r_ref, o_ref):
    o_ref[...] = r_ref[...] + jnp.dot(x_ref[...], w_ref[...], preferred_element_type=F32)


def _mm_resid(x, w, w_index, resid, *, tm, tn):
    n, k = x.shape
    nout = w.shape[-1]
    assert n % tm == 0 and nout % tn == 0
    nlead = w.ndim - 2
    return pl.pallas_call(
        _mm_resid_body,
        grid=(n // tm, nout // tn),
        in_specs=[pl.BlockSpec((tm, k), lambda i, j: (i, 0)),
                  pl.BlockSpec((None,) * nlead + (k, tn), lambda i, j: tuple(w_index) + (0, j)),
                  pl.BlockSpec((tm, tn), lambda i, j: (i, j))],
        out_specs=pl.BlockSpec((tm, tn), lambda i, j: (i, j)),
        out_shape=jax.ShapeDtypeStruct((n, nout), F32),
        compiler_params=_params(("parallel", "arbitrary")),
        name="mm_resid",
    )(x, w, resid)


def _ple_body(x_ref, g_ref, wg_ref, p_ref, wp_ref, fg_ref, o_ref, *, tn, final):
    x = x_ref[...]
    xn = _rms(x, g_ref[...]).astype(BF16)
    pb = p_ref[...].astype(BF16)
    d = x.shape[1]
    for c in range(d // tn):
        sl = slice(c * tn, (c + 1) * tn)
        gate = jnp.dot(xn, wg_ref[:, sl], preferred_element_type=F32)
        proj = jnp.dot(pb, wp_ref[:, sl], preferred_element_type=F32)
        o_ref[:, sl] = x_ref[:, sl] + _sigmoid(gate) * proj
    if final:
        o_ref[...] = _rms(o_ref[...], fg_ref[...])


def _ple(x, gain, wg, p, wp, layer, final_gain, *, tm, tn, final):
    n, d = x.shape
    pd = p.shape[-1]
    rows = n // tm
    assert n % tm == 0 and d % tn == 0
    return pl.pallas_call(
        functools.partial(_ple_body, tn=tn, final=final),
        grid=(rows,),
        in_specs=[pl.BlockSpec((tm, d), lambda i: (i, 0)),
                  pl.BlockSpec((1, d), lambda i: (0, 0)),
                  pl.BlockSpec((None, d, d), lambda i: (layer, 0, 0)),
                  pl.BlockSpec((None, tm, pd), lambda i: (layer, i, 0)),
                  pl.BlockSpec((None, pd, d), lambda i: (layer, 0, 0)),
                  pl.BlockSpec((1, d), lambda i: (0, 0))],
        out_specs=pl.BlockSpec((tm, d), lambda i: (i, 0)),
        out_shape=jax.ShapeDtypeStruct((n, d), F32),
        compiler_params=_params(("parallel",)),
        name="ple",
    )(x, gain, wg, p, wp, final_gain)


def _rope_table_body(cos_ref, sin_ref):
    shape = cos_ref.shape
    pos = lax.broadcasted_iota(jnp.int32, shape, 0).astype(F32)
    lane = lax.broadcasted_iota(jnp.int32, shape, 1)
    half = HEAD // 2
    k = jnp.bitwise_and(lane, half - 1).astype(F32)
    inv_freq = jnp.exp(k * (-math.log(ROPE_THETA) / half))
    ang = pos * inv_freq
    cos_ref[...] = jnp.cos(ang)
    s = jnp.sin(ang)
    sin_ref[...] = jnp.where(lane < half, -s, s)


def _rope_tables(seq):
    return pl.pallas_call(
        _rope_table_body,
        out_shape=(jax.ShapeDtypeStruct((seq, HEAD), F32), jax.ShapeDtypeStruct((seq, HEAD), F32)),
        name="rope_tables",
    )()


def _split3(x):
    hi = x.astype(BF16)
    r1 = x - hi.astype(F32)
    mid = r1.astype(BF16)
    lo = (r1 - mid.astype(F32)).astype(BF16)
    return hi, mid, lo


def _hgrn_consts():
    r = np.arange(HG_SUB)
    chunk, blk = r // HG_CHUNK, r // SUBLANES
    nch = HG_SUB // HG_CHUNK
    same_chunk = chunk[:, None] == chunk[None, :]
    tril = same_chunk & (r[None, :] <= r[:, None])
    off = same_chunk & (blk[:, None] > blk[None, :])
    diag = (blk[:, None] == blk[None, :]) & (r[None, :] <= r[:, None])
    lane_blk = np.arange(SUBLANES * HEAD) // HEAD
    spread = lane_blk[:, None] == (r % SUBLANES)[None, :]
    bpc = HG_CHUNK // SUBLANES
    kmask = np.stack([np.broadcast_to((((r % HG_CHUNK) // SUBLANES) == jb)[:, None], (HG_SUB, HEAD))
                      for jb in range(bpc - 1)])
    cmask = np.stack([np.broadcast_to((chunk == c)[None, :], (HEAD, HG_SUB)) for c in range(nch)])
    eye = np.eye(HEAD)
    return (jnp.asarray(tril, BF16), jnp.asarray(off, F32), jnp.asarray(diag, F32),
            jnp.asarray(spread, BF16), jnp.asarray(kmask, F32), jnp.asarray(cmask, F32), jnp.asarray(eye, BF16))


def _hgrn_front(q, z, v, lb, tril, kmask):
    rows = q.shape[0]
    nch = rows // HG_CHUNK
    nblk = rows // SUBLANES
    bpc = HG_CHUNK // SUBLANES

    log2e = math.log2(math.e)
    soft = jnp.log2(1.0 + jnp.exp2(jnp.abs(z) * -log2e))
    lg_lb = jnp.log2(lb)
    lg_1mlb = jnp.log2(1.0 - lb)
    lg_sig = lg_1mlb + (jnp.minimum(z, 0.0) * log2e - soft)
    lg_k = lg_1mlb - (jnp.maximum(z, 0.0) * log2e + soft)
    logf = jnp.maximum(lg_lb, lg_sig) + jnp.log2(1.0 + jnp.exp2(-jnp.abs(lg_lb - lg_sig)))
    qs = q * (HEAD ** -0.5)

    hi, mid, lo = _split3(logf)
    parts = jnp.dot(tril, jnp.concatenate([hi, mid, lo], axis=1), preferred_element_type=F32)
    b = parts[:, :HEAD] + parts[:, HEAD:2 * HEAD] + parts[:, 2 * HEAD:]

    vb = v.astype(BF16)
    c = b - lg_k
    b3 = b.reshape(nch, HG_CHUNK, HEAD)
    b_last = b3[:, HG_CHUNK - 1:HG_CHUNK, :]
    q_in = (qs * jnp.exp2(b)).astype(BF16)
    k_end = jnp.exp2(b_last - c.reshape(nch, HG_CHUNK, HEAD)).reshape(rows, HEAD).astype(BF16)
    decay_cols = jnp.exp2(b_last).reshape(nch, HEAD).T

    qs3 = qs.reshape(nch, HG_CHUNK, HEAD)
    b8 = b.reshape(nblk, SUBLANES, HEAD)
    c8 = c.reshape(nblk, SUBLANES, HEAD)
    k_blk = jnp.exp2(b8[:, SUBLANES - 1:SUBLANES, :] - c8).reshape(rows, HEAD)
    q_groups, k_groups = [], []
    for jb in range(bpc - 1):
        ref = b3[:, jb * SUBLANES + SUBLANES - 1:jb * SUBLANES + SUBLANES, :]
        qg = qs3 * jnp.exp2(jnp.minimum(b3 - ref, 0.0))
        q_groups.append(qg.reshape(rows, HEAD).astype(BF16))
        k_groups.append((k_blk * kmask[jb]).astype(BF16))
    q_cat = jnp.concatenate(q_groups, axis=1)
    k_cat = jnp.concatenate(k_groups, axis=1)

    q8 = qs.reshape(nblk, SUBLANES, HEAD)
    prods = []
    for i in range(SUBLANES):
        e = jnp.exp2(jnp.minimum(b8 - c8[:, i:i + 1, :], 0.0))
        prods.append((e * q8).reshape(rows, HEAD).astype(BF16))
    p_cat = jnp.concatenate(prods, axis=1)
    return q_cat, k_cat, p_cat, q_in, k_end, decay_cols, vb


def _hgrn_back(front, state, off, diag, spread, cmask, eye):
    q_cat, k_cat, p_cat, q_in, k_end, decay_cols, vb = front
    nch = decay_cols.shape[1]
    a = lax.dot_general(q_cat, k_cat, (((1,), (1,)), ((), ())), preferred_element_type=F32) * off
    a = a + jnp.dot(p_cat, spread, preferred_element_type=F32) * diag
    o = jnp.dot(a.astype(BF16), vb, preferred_element_type=F32)

    k_end_t = lax.dot_general(eye, k_end, (((1,), (1,)), ((), ())), preferred_element_type=F32)
    k_stack = jnp.concatenate([(k_end_t * cmask[c]).astype(BF16) for c in range(nch)], axis=0)
    upd = jnp.dot(k_stack, vb, preferred_element_type=F32)

    outs = []
    for c in range(nch):
        outs.append(jnp.dot(q_in[c * HG_CHUNK:(c + 1) * HG_CHUNK], state.astype(BF16), preferred_element_type=F32))
        state = state * decay_cols[:, c:c + 1] + upd[c * HEAD:(c + 1) * HEAD]
    o = o + jnp.concatenate(outs, axis=0)
    return o, state


def _hgrn_body(q_ref, z_ref, v_ref, g_ref, lbp_ref, gain_ref, tril_ref, off_ref, diag_ref, spread_ref, kmask_ref,
               cmask_ref, eye_ref, o_ref, state_ref, *, layer):
    lbp = lbp_ref[...]
    e = jnp.exp(lbp - jnp.max(lbp, axis=0, keepdims=True))
    lb = jnp.sum(e[:layer + 1], axis=0, keepdims=True) / jnp.sum(e, axis=0, keepdims=True)

    state_ref[...] = jnp.zeros_like(state_ref)
    consts = (tril_ref[...], off_ref[...], diag_ref[...], spread_ref[...], kmask_ref[...], cmask_ref[...],
              eye_ref[...])

    def slabs(s, carry):
        state = state_ref[...]
        sls = [pl.ds(pl.multiple_of((s * HG_UNROLL + u) * HG_SUB, HG_SUB), HG_SUB) for u in range(HG_UNROLL)]
        fronts = [_hgrn_front(q_ref[sl, :].astype(F32), z_ref[sl, :], v_ref[sl, :].astype(F32), lb, consts[0],
                              consts[4]) for sl in sls]
        for sl, front in zip(sls, fronts):
            o, state = _hgrn_back(front, state, consts[1], consts[2], consts[3], consts[5], consts[6])
            on = _rms(o, gain_ref[...])
            g = g_ref[sl, :].astype(F32)
            o_ref[sl, :] = (on * (g * _sigmoid(g))).astype(o_ref.dtype)
        state_ref[...] = state
        return carry

    lax.fori_loop(0, q_ref.shape[0] // (HG_SUB * HG_UNROLL), slabs, 0)


def _hgrn(qvg, z, lbp, gain, layer, *, batch, seq, heads):
    n = z.shape[0]
    assert seq % (HG_SUB * HG_UNROLL) == 0
    consts = _hgrn_consts()
    sec = lambda k: pl.BlockSpec((seq, HEAD), lambda b, h: (b, k * heads + h))
    whole = lambda a: pl.BlockSpec(a.shape, lambda b, h: (0,) * a.ndim)
    return pl.pallas_call(
        functools.partial(_hgrn_body, layer=layer),
        grid=(batch, heads),
        in_specs=[sec(0), sec(0), sec(1), sec(2),
                  pl.BlockSpec((lbp.shape[0], HEAD), lambda b, h: (0, h)),
                  pl.BlockSpec((1, HEAD), lambda b, h: (0, 0))] + [whole(a) for a in consts],
        out_specs=pl.BlockSpec((seq, HEAD), lambda b, h: (b, h)),
        out_shape=jax.ShapeDtypeStruct((n, heads * HEAD), BF16),
        scratch_shapes=[pltpu.VMEM((HEAD, HEAD), F32)],
        compiler_params=_params(("parallel", "parallel")),
        name="hgrn",
    )(qvg, z, qvg, qvg, lbp, gain, *consts)


def _attn_body(lam_ref, sub_ref, q_ref, k_ref, v_ref, o_ref, acc_ref, m_ref, l_ref, s0_ref, s1_ref, *, tq, tk,
               lambda_init):
    nrep = tk // LANES
    lam = lam_ref[...]
    lam_full = (jnp.exp(jnp.sum(lam[0:1] * lam[1:2], axis=-1, keepdims=True))
                - jnp.exp(jnp.sum(lam[2:3] * lam[3:4], axis=-1, keepdims=True)) + lambda_init)

    def q_tile(qi, carry):
        _attn_q_tile(qi, lam_full, sub_ref, q_ref, k_ref, v_ref, o_ref, acc_ref, m_ref, l_ref, s0_ref, s1_ref,
                     tq=tq, tk=tk, nrep=nrep, lambda_init=lambda_init)
        return carry

    lax.fori_loop(0, q_ref.shape[0] // tq, q_tile, 0)


def _attn_q_tile(qi, lam_full, sub_ref, q_ref, k_ref, v_ref, o_ref, acc_ref, m_ref, l_ref, s0_ref, s1_ref, *, tq, tk,
                 nrep, lambda_init):
    rows = pl.ds(pl.multiple_of(qi * tq, tq), tq)

    m_ref[...] = jnp.full(m_ref.shape, NEG, F32)
    l_ref[...] = jnp.zeros(l_ref.shape, F32)
    acc_ref[...] = jnp.zeros(acc_ref.shape, F32)

    def scores(j, s_ref):
        kt = k_ref[pl.ds(pl.multiple_of(j * tk, tk), tk), :]
        q = q_ref[rows, :]
        for c in range(2):
            cs = slice(c * HEAD, (c + 1) * HEAD)
            s_ref[c] = lax.dot_general(q[:, cs], kt[:, cs], (((1,), (1,)), ((), ())), preferred_element_type=F32)

    def consume(j, s_ref, masked):
        vt = v_ref[pl.ds(pl.multiple_of(j * tk, tk), tk), :]
        ps, alphas = [], []
        for c in range(2):
            s = s_ref[c]
            if masked:
                ri = lax.broadcasted_iota(jnp.int32, s.shape, 0)
                ci = lax.broadcasted_iota(jnp.int32, s.shape, 1)
                s = jnp.where(ci <= ri, s, NEG)
            m_old = m_ref[c]
            m_new = jnp.maximum(m_old, jnp.max(s, axis=-1, keepdims=True))
            alpha = jnp.exp2(m_old - m_new)
            p = jnp.exp2(s - jnp.concatenate([m_new] * nrep, axis=1))
            l_ref[c] = alpha * l_ref[c] + jnp.sum(p, axis=-1, keepdims=True)
            m_ref[c] = m_new
            ps.append(p.astype(BF16))
            alphas.append(alpha)
        pv = jnp.dot(jnp.concatenate(ps, axis=0), vt, preferred_element_type=F32)
        for c in range(2):
            acc_ref[c] = jnp.concatenate([alphas[c]] * 2, axis=1) * acc_ref[c] + pv[c * tq:(c + 1) * tq]

    scores(0, s0_ref)

    def pair(i, carry):
        scores(2 * i + 1, s1_ref)
        consume(2 * i, s0_ref, False)
        scores(2 * i + 2, s0_ref)
        consume(2 * i + 1, s1_ref, False)
        return carry

    lax.fori_loop(0, qi // 2, pair, 0)

    @pl.when(qi % 2 == 1)
    def _():
        scores(qi, s1_ref)
        consume(qi - 1, s0_ref, False)
        consume(qi, s1_ref, True)

    @pl.when(qi % 2 == 0)
    def _():
        consume(qi, s0_ref, True)

    inv0 = jnp.concatenate([1.0 / l_ref[0]] * 2, axis=1)
    inv1 = jnp.concatenate([1.0 / l_ref[1]] * 2, axis=1)
    o = acc_ref[0] * inv0 - lam_full * (acc_ref[1] * inv1)
    o_ref[rows, :] = (_rms(o, sub_ref[...]) * (1.0 - lambda_init)).astype(o_ref.dtype)


def _attention(q, kv, lam, subln, *, batch, seq, heads, tq, lambda_init):
    n = q.shape[0]
    vd = 2 * HEAD
    assert seq % tq == 0
    return pl.pallas_call(
        functools.partial(_attn_body, tq=tq, tk=tq, lambda_init=lambda_init),
        grid=(batch, heads),
        in_specs=[pl.BlockSpec((4, HEAD), lambda b, h: (0, 0)),
                  pl.BlockSpec((1, vd), lambda b, h: (0, 0)),
                  pl.BlockSpec((seq, vd), lambda b, h: (b, h)),
                  pl.BlockSpec((seq, vd), lambda b, h: (b, h)),
                  pl.BlockSpec((seq, vd), lambda b, h: (b, heads + h))],
        out_specs=pl.BlockSpec((seq, vd), lambda b, h: (b, h)),
        out_shape=jax.ShapeDtypeStruct((n, heads * vd), BF16),
        scratch_shapes=[pltpu.VMEM((2, tq, vd), F32), pltpu.VMEM((2, tq, LANES), F32),
                        pltpu.VMEM((2, tq, LANES), F32), pltpu.VMEM((2, tq, tq), F32), pltpu.VMEM((2, tq, tq), F32)],
        compiler_params=_params(("parallel", "parallel")),
        name="diff_attn",
    )(lam, subln, q, kv, kv)


def kernel(x, p, ffn_norm, ffn_w_gate_up, ffn_w_down, mix_norm, hgrn_w_in, hgrn_lower_bounds, hgrn_out_norm, hgrn_w_out, kv_norm, w_kv, diff_w_q, diff_lambda, diff_subln, diff_w_out, ple_norm, ple_w_gate, ple_w_proj, final_norm):
    batch, seq, d = x.shape
    depth = p.shape[0]
    n_a = hgrn_w_in.shape[0]
    n = batch * seq
    hg_heads = d // HEAD
    da_heads = d // (2 * HEAD)
    k_cols = da_heads * 2 * HEAD

    tm = min(512, seq)
    tm_big = min(1024, seq)
    tq = min(512, seq)
    wide = hg_heads * HEAD

    wgu = ffn_w_gate_up.astype(BF16)
    wdn = ffn_w_down.astype(BF16)
    w_in = hgrn_w_in.astype(BF16)
    w_ho = hgrn_w_out.astype(BF16)
    w_kvb = w_kv.astype(BF16)
    w_q = diff_w_q.astype(BF16)
    w_do = diff_w_out.astype(BF16)
    w_pg = ple_w_gate.astype(BF16)
    w_pp = ple_w_proj.astype(BF16)
    p2 = p.reshape(depth, n, p.shape[-1])

    rope = _rope_tables(seq)
    h = x.reshape(n, d)
    kv = None
    for i in range(depth):
        h = _ffn(h, ffn_norm[i, 0].reshape(1, d), wgu, wdn, i, 0, tm=tm_big, tf=512)
        if i < n_a:
            gain = mix_norm[i].reshape(1, d)
            z = _norm_mm(h, gain, w_in, (i,), tm=tm, tn=wide, out_dtype=F32, ntiles=1, src_tile=lambda j: j + 1)
            per = wide // 1024
            qvg = _norm_mm(h, gain, w_in, (i,), tm=tm_big, tn=1024, out_dtype=BF16, ntiles=3 * per,
                           src_tile=lambda j: j + jnp.where(j >= per, per, 0))
            og = _hgrn(qvg, z, hgrn_lower_bounds, hgrn_out_norm[i].reshape(1, HEAD), i,
                       batch=batch, seq=seq, heads=hg_heads)
            h = _mm_resid(og, w_ho, (i,), h, tm=tm, tn=d)
        else:
            j = i - n_a
            lambda_init = 0.8 - 0.6 * math.exp(-0.3 * i)
            q = _norm_mm(h, mix_norm[i].reshape(1, d), w_q, (j,), tm=tm, tn=k_cols, out_dtype=BF16,
                         rope=rope, seq=seq, rope_cols=k_cols, scale=HEAD ** -0.5 * math.log2(math.e))
            oa = _attention(q, kv, diff_lambda[j], diff_subln[j].reshape(1, 2 * HEAD),
                            batch=batch, seq=seq, heads=da_heads, tq=tq, lambda_init=lambda_init)
            h = _mm_resid(oa, w_do, (j,), h, tm=tm, tn=d)
        h = _ffn(h, ffn_norm[i, 1].reshape(1, d), wgu, wdn, i, 1, tm=tm_big, tf=512)
        last = i == depth - 1
        h = _ple(h, ple_norm[i].reshape(1, d), w_pg, p2, w_pp, i, final_norm.reshape(1, d),
                 tm=tm, tn=512, final=last)
        if i == n_a - 1:
            kv = _norm_mm(h, kv_norm.reshape(1, d), w_kvb, (), tm=tm_big, tn=k_cols, out_dtype=BF16,
                          rope=rope, seq=seq, rope_cols=k_cols)
    return h.reshape(batch, seq, d)
```

```python
import functools
import math

import jax
import jax.numpy as jnp
import numpy as np
from jax import lax
from jax.experimental import pallas as pl
from jax.experimental.pallas import tpu as pltpu

F32 = jnp.float32
BF16 = jnp.bfloat16

NORM_EPS = 1e-6
ROPE_THETA = 10000.0
HEAD = 128
LANES = 128
SUBLANES = 8
HG_CHUNK = 32
HG_SUB = 256
HG_UNROLL = 8
NEG = -1e30

VMEM_LIMIT = 60 * 1024 * 1024


def _params(sem):
    return pltpu.CompilerParams(dimension_semantics=sem, vmem_limit_bytes=VMEM_LIMIT)


def _rms(x, g):
    ms = jnp.mean(x * x, axis=-1, keepdims=True)
    return x * lax.rsqrt(ms + NORM_EPS) * g


def _sigmoid(x):
    return 1.0 / (1.0 + jnp.exp(-x))


def _ffn_body(x_ref, g_ref, wg_ref, wu_ref, wd_ref, wdl_ref, o_ref, xn_ref, act_ref):
    f = pl.program_id(1)
    last = pl.num_programs(1) - 1

    def gate_up():
        xn = xn_ref[...]
        gate = jnp.dot(xn, wg_ref[...], preferred_element_type=F32)
        up = jnp.dot(xn, wu_ref[...], preferred_element_type=F32)
        return (gate * _sigmoid(gate) * up).astype(BF16)

    @pl.when(f == 0)
    def _():
        x = x_ref[...]
        xn_ref[...] = _rms(x, g_ref[...]).astype(BF16)
        o_ref[...] = x
        act_ref[...] = gate_up()

    @pl.when(f > 0)
    def _():
        part = jnp.dot(act_ref[...], wd_ref[...], preferred_element_type=F32)
        act = gate_up()
        o_ref[...] += 0.5 * part
        act_ref[...] = act

    @pl.when(f == last)
    def _():
        o_ref[...] += 0.5 * jnp.dot(act_ref[...], wdl_ref[...], preferred_element_type=F32)


def _ffn(x, gain, wgu, wd, li, lj, *, tm, tf):
    n, d = x.shape
    ff = wd.shape[2]
    nf = ff // tf
    assert n % tm == 0 and ff % tf == 0 and nf >= 2
    return pl.pallas_call(
        _ffn_body,
        grid=(n // tm, nf),
        in_specs=[
            pl.BlockSpec((tm, d), lambda i, f: (i, 0)),
            pl.BlockSpec((1, d), lambda i, f: (0, 0)),
            pl.BlockSpec((None, None, d, tf), lambda i, f: (li, lj, 0, f)),
            pl.BlockSpec((None, None, d, tf), lambda i, f: (li, lj, 0, f + nf)),
            pl.BlockSpec((None, None, tf, d), lambda i, f: (li, lj, jnp.maximum(f - 1, 0), 0)),
            pl.BlockSpec((None, None, tf, d), lambda i, f: (li, lj, nf - 1, 0), pipeline_mode=pl.Buffered(1)),
        ],
        out_specs=pl.BlockSpec((tm, d), lambda i, f: (i, 0)),
        out_shape=jax.ShapeDtypeStruct((n, d), F32),
        scratch_shapes=[pltpu.VMEM((tm, d), BF16), pltpu.VMEM((tm, tf), BF16)],
        compiler_params=_params(("parallel", "arbitrary")),
        name="ffn",
    )(x, gain, wgu, wgu, wd, wd)


def _rope_store(y, cos, sin, o_ref, scale):
    for g in range(y.shape[1] // HEAD):
        yg = y[:, g * HEAD:(g + 1) * HEAD]
        r = yg * cos + pltpu.roll(yg, HEAD // 2, axis=1) * sin
        if scale != 1.0:
            r = r * scale
        o_ref[:, g * HEAD:(g + 1) * HEAD] = r.astype(o_ref.dtype)


def _norm_mm_body(x_ref, g_ref, w_ref, o_ref, xn_ref):
    @pl.when(pl.program_id(1) == 0)
    def _():
        xn_ref[...] = _rms(x_ref[...], g_ref[...]).astype(BF16)

    o_ref[...] = jnp.dot(xn_ref[...], w_ref[...], preferred_element_type=F32).astype(o_ref.dtype)


def _norm_mm_rope_body(x_ref, g_ref, w_ref, cos_ref, sin_ref, o_ref, xn_ref, *, rope_tiles, scale):
    j = pl.program_id(1)

    @pl.when(j == 0)
    def _():
        xn_ref[...] = _rms(x_ref[...], g_ref[...]).astype(BF16)

    y = jnp.dot(xn_ref[...], w_ref[...], preferred_element_type=F32)

    @pl.when(j < rope_tiles)
    def _():
        _rope_store(y, cos_ref[...], sin_ref[...], o_ref, scale)

    @pl.when(j >= rope_tiles)
    def _():
        o_ref[...] = y.astype(o_ref.dtype)


def _norm_mm(x, gain, w, w_index, *, tm, tn, out_dtype, rope=None, seq=None, rope_cols=0, scale=1.0,
             ntiles=None, src_tile=lambda j: j):
    n, d = x.shape
    nout = (w.shape[-1] // tn if ntiles is None else ntiles) * tn
    assert n % tm == 0 and w.shape[-1] % tn == 0
    nlead = w.ndim - 2
    w_spec = pl.BlockSpec((None,) * nlead + (d, tn), lambda i, j: tuple(w_index) + (0, src_tile(j)))
    in_specs = [pl.BlockSpec((tm, d), lambda i, j: (i, 0)),
                pl.BlockSpec((1, d), lambda i, j: (0, 0)),
                w_spec]
    args = [x, gain, w]
    if rope is None:
        body = _norm_mm_body
    else:
        assert seq % tm == 0 and rope_cols % tn == 0
        nseq = seq // tm
        tab_spec = pl.BlockSpec((tm, HEAD), lambda i, j: (i % nseq, 0))
        in_specs += [tab_spec, tab_spec]
        args += list(rope)
        body = functools.partial(_norm_mm_rope_body, rope_tiles=rope_cols // tn, scale=scale)
    return pl.pallas_call(
        body,
        grid=(n // tm, nout // tn),
        in_specs=in_specs,
        out_specs=pl.BlockSpec((tm, tn), lambda i, j: (i, j)),
        out_shape=jax.ShapeDtypeStruct((n, nout), out_dtype),
        scratch_shapes=[pltpu.VMEM((tm, d), BF16)],
        compiler_params=_params(("parallel", "arbitrary")),
        name="norm_mm",
    )(*args)


def _mm_resid_body(x_ref, w_ref, r_ref, o_ref):
    o_ref[...] = r_ref[...] + jnp.dot(x_ref[...], w_ref[...], preferred_element_type=F32)


def _mm_resid(x, w, w_index, resid, *, tm, tn):
    n, k = x.shape
    nout = w.shape[-1]
    assert n % tm == 0 and nout % tn == 0
    nlead = w.ndim - 2
    return pl.pallas_call(
        _mm_resid_body,
        grid=(n // tm, nout // tn),
        in_specs=[pl.BlockSpec((tm, k), lambda i, j: (i, 0)),
                  pl.BlockSpec((None,) * nlead + (k, tn), lambda i, j: tuple(w_index) + (0, j)),
                  pl.BlockSpec((tm, tn), lambda i, j: (i, j))],
        out_specs=pl.BlockSpec((tm, tn), lambda i, j: (i, j)),
        out_shape=jax.ShapeDtypeStruct((n, nout), F32),
        compiler_params=_params(("parallel", "arbitrary")),
        name="mm_resid",
    )(x, w, resid)


def _ple_body(x_ref, g_ref, wg_ref, p_ref, wp_ref, fg_ref, o_ref, *, tn, final):
    x = x_ref[...]
    xn = _rms(x, g_ref[...]).astype(BF16)
    pb = p_ref[...].astype(BF16)
    d = x.shape[1]
    for c in range(d // tn):
        sl = slice(c * tn, (c + 1) * tn)
        gate = jnp.dot(xn, wg_ref[:, sl], preferred_element_type=F32)
        proj = jnp.dot(pb, wp_ref[:, sl], preferred_element_type=F32)
        o_ref[:, sl] = x_ref[:, sl] + _sigmoid(gate) * proj
    if final:
        o_ref[...] = _rms(o_ref[...], fg_ref[...])


def _ple(x, gain, wg, p, wp, layer, final_gain, *, tm, tn, final):
    n, d = x.shape
    pd = p.shape[-1]
    rows = n // tm
    assert n % tm == 0 and d % tn == 0
    return pl.pallas_call(
        functools.partial(_ple_body, tn=tn, final=final),
        grid=(rows,),
        in_specs=[pl.BlockSpec((tm, d), lambda i: (i, 0)),
                  pl.BlockSpec((1, d), lambda i: (0, 0)),
                  pl.BlockSpec((None, d, d), lambda i: (layer, 0, 0)),
                  pl.BlockSpec((None, tm, pd), lambda i: (layer, i, 0)),
                  pl.BlockSpec((None, pd, d), lambda i: (layer, 0, 0)),
                  pl.BlockSpec((1, d), lambda i: (0, 0))],
        out_specs=pl.BlockSpec((tm, d), lambda i: (i, 0)),
        out_shape=jax.ShapeDtypeStruct((n, d), F32),
        compiler_params=_params(("parallel",)),
        name="ple",
    )(x, gain, wg, p, wp, final_gain)


def _rope_table_body(cos_ref, sin_ref):
    shape = cos_ref.shape
    pos = lax.broadcasted_iota(jnp.int32, shape, 0).astype(F32)
    lane = lax.broadcasted_iota(jnp.int32, shape, 1)
    half = HEAD // 2
    k = jnp.bitwise_and(lane, half - 1).astype(F32)
    inv_freq = jnp.exp(k * (-math.log(ROPE_THETA) / half))
    ang = pos * inv_freq
    cos_ref[...] = jnp.cos(ang)
    s = jnp.sin(ang)
    sin_ref[...] = jnp.where(lane < half, -s, s)


def _rope_tables(seq):
    return pl.pallas_call(
        _rope_table_body,
        out_shape=(jax.ShapeDtypeStruct((seq, HEAD), F32), jax.ShapeDtypeStruct((seq, HEAD), F32)),
        name="rope_tables",
    )()


def _split3(x):
    hi = x.astype(BF16)
    r1 = x - hi.astype(F32)
    mid = r1.astype(BF16)
    lo = (r1 - mid.astype(F32)).astype(BF16)
    return hi, mid, lo


def _hgrn_consts():
    r = np.arange(HG_SUB)
    chunk, blk = r // HG_CHUNK, r // SUBLANES
    nch = HG_SUB // HG_CHUNK
    same_chunk = chunk[:, None] == chunk[None, :]
    tril = same_chunk & (r[None, :] <= r[:, None])
    off = same_chunk & (blk[:, None] > blk[None, :])
    diag = (blk[:, None] == blk[None, :]) & (r[None, :] <= r[:, None])
    lane_blk = np.arange(SUBLANES * HEAD) // HEAD
    spread = lane_blk[:, None] == (r % SUBLANES)[None, :]
    bpc = HG_CHUNK // SUBLANES
    kmask = np.stack([np.broadcast_to((((r % HG_CHUNK) // SUBLANES) == jb)[:, None], (HG_SUB, HEAD))
                      for jb in range(bpc - 1)])
    cmask = np.stack([np.broadcast_to((chunk == c)[None, :], (HEAD, HG_SUB)) for c in range(nch)])
    eye = np.eye(HEAD)
    return (jnp.asarray(tril, BF16), jnp.asarray(off, F32), jnp.asarray(diag, F32),
            jnp.asarray(spread, BF16), jnp.asarray(kmask, F32), jnp.asarray(cmask, F32), jnp.asarray(eye, BF16))


def _hgrn_front(q, z, v, lb, tril, kmask):
    rows = q.shape[0]
    nch = rows // HG_CHUNK
    nblk = rows // SUBLANES
    bpc = HG_CHUNK // SUBLANES

    log2e = math.log2(math.e)
    soft = jnp.log2(1.0 + jnp.exp2(jnp.abs(z) * -log2e))
    lg_lb = jnp.log2(lb)
    lg_1mlb = jnp.log2(1.0 - lb)
    lg_sig = lg_1mlb + (jnp.minimum(z, 0.0) * log2e - soft)
    lg_k = lg_1mlb - (jnp.maximum(z, 0.0) * log2e + soft)
    logf = jnp.maximum(lg_lb, lg_sig) + jnp.log2(1.0 + jnp.exp2(-jnp.abs(lg_lb - lg_sig)))
    qs = q * (HEAD ** -0.5)

    hi, mid, lo = _split3(logf)
    parts = jnp.dot(tril, jnp.concatenate([hi, mid, lo], axis=1), preferred_element_type=F32)
    b = parts[:, :HEAD] + parts[:, HEAD:2 * HEAD] + parts[:, 2 * HEAD:]

    vb = v.astype(BF16)
    c = b - lg_k
    b3 = b.reshape(nch, HG_CHUNK, HEAD)
    b_last = b3[:, HG_CHUNK - 1:HG_CHUNK, :]
    q_in = (qs * jnp.exp2(b)).astype(BF16)
    k_end = jnp.exp2(b_last - c.reshape(nch, HG_CHUNK, HEAD)).reshape(rows, HEAD).astype(BF16)
    decay_cols = jnp.exp2(b_last).reshape(nch, HEAD).T

    qs3 = qs.reshape(nch, HG_CHUNK, HEAD)
    b8 = b.reshape(nblk, SUBLANES, HEAD)
    c8 = c.reshape(nblk, SUBLANES, HEAD)
    k_blk = jnp.exp2(b8[:, SUBLANES - 1:SUBLANES, :] - c8).reshape(rows, HEAD)
    q_groups, k_groups = [], []
    for jb in range(bpc - 1):
        ref = b3[:, jb * SUBLANES + SUBLANES - 1:jb * SUBLANES + SUBLANES, :]
        qg = qs3 * jnp.exp2(jnp.minimum(b3 - ref, 0.0))
        q_groups.append(qg.reshape(rows, HEAD).astype(BF16))
        k_groups.append((k_blk * kmask[jb]).astype(BF16))
    q_cat = jnp.concatenate(q_groups, axis=1)
    k_cat = jnp.concatenate(k_groups, axis=1)

    q8 = qs.reshape(nblk, SUBLANES, HEAD)
    prods = []
    for i in range(SUBLANES):
        e = jnp.exp2(jnp.minimum(b8 - c8[:, i:i + 1, :], 0.0))
        prods.append((e * q8).reshape(rows, HEAD).astype(BF16))
    p_cat = jnp.concatenate(prods, axis=1)
    return q_cat, k_cat, p_cat, q_in, k_end, decay_cols, vb


def _hgrn_back(front, state, off, diag, spread, cmask, eye):
    q_cat, k_cat, p_cat, q_in, k_end, decay_cols, vb = front
    nch = decay_cols.shape[1]
    a = lax.dot_general(q_cat, k_cat, (((1,), (1,)), ((), ())), preferred_element_type=F32) * off
    a = a + jnp.dot(p_cat, spread, preferred_element_type=F32) * diag
    o = jnp.dot(a.astype(BF16), vb, preferred_element_type=F32)

    k_end_t = lax.dot_general(eye, k_end, (((1,), (1,)), ((), ())), preferred_element_type=F32)
    k_stack = jnp.concatenate([(k_end_t * cmask[c]).astype(BF16) for c in range(nch)], axis=0)
    upd = jnp.dot(k_stack, vb, preferred_element_type=F32)

    outs = []
    for c in range(nch):
        outs.append(jnp.dot(q_in[c * HG_CHUNK:(c + 1) * HG_CHUNK], state.astype(BF16), preferred_element_type=F32))
        state = state * decay_cols[:, c:c + 1] + upd[c * HEAD:(c + 1) * HEAD]
    o = o + jnp.concatenate(outs, axis=0)
    return o, state


def _hgrn_body(q_ref, z_ref, v_ref, g_ref, lbp_ref, gain_ref, tril_ref, off_ref, diag_ref, spread_ref, kmask_ref,
               cmask_ref, eye_ref, o_ref, state_ref, *, layer):
    lbp = lbp_ref[...]
    e = jnp.exp(lbp - jnp.max(lbp, axis=0, keepdims=True))
    lb = jnp.sum(e[:layer + 1], axis=0, keepdims=True) / jnp.sum(e, axis=0, keepdims=True)

    state_ref[...] = jnp.zeros_like(state_ref)
    consts = (tril_ref[...], off_ref[...], diag_ref[...], spread_ref[...], kmask_ref[...], cmask_ref[...],
              eye_ref[...])

    def slabs(s, carry):
        state = state_ref[...]
        sls = [pl.ds(pl.multiple_of((s * HG_UNROLL + u) * HG_SUB, HG_SUB), HG_SUB) for u in range(HG_UNROLL)]
        fronts = [_hgrn_front(q_ref[sl, :].astype(F32), z_ref[sl, :], v_ref[sl, :].astype(F32), lb, consts[0],
                              consts[4]) for sl in sls]
        for sl, front in zip(sls, fronts):
            o, state = _hgrn_back(front, state, consts[1], consts[2], consts[3], consts[5], consts[6])
            on = _rms(o, gain_ref[...])
            g = g_ref[sl, :].astype(F32)
            o_ref[sl, :] = (on * (g * _sigmoid(g))).astype(o_ref.dtype)
        state_ref[...] = state
        return carry

    lax.fori_loop(0, q_ref.shape[0] // (HG_SUB * HG_UNROLL), slabs, 0)


def _hgrn(qvg, z, lbp, gain, layer, *, batch, seq, heads):
    n = z.shape[0]
    assert seq % (HG_SUB * HG_UNROLL) == 0
    consts = _hgrn_consts()
    sec = lambda k: pl.BlockSpec((seq, HEAD), lambda b, h: (b, k * heads + h))
    whole = lambda a: pl.BlockSpec(a.shape, lambda b, h: (0,) * a.ndim)
    return pl.pallas_call(
        functools.partial(_hgrn_body, layer=layer),
        grid=(batch, heads),
        in_specs=[sec(0), sec(0), sec(1), sec(2),
                  pl.BlockSpec((lbp.shape[0], HEAD), lambda b, h: (0, h)),
                  pl.BlockSpec((1, HEAD), lambda b, h: (0, 0))] + [whole(a) for a in consts],
        out_specs=pl.BlockSpec((seq, HEAD), lambda b, h: (b, h)),
        out_shape=jax.ShapeDtypeStruct((n, heads * HEAD), BF16),
        scratch_shapes=[pltpu.VMEM((HEAD, HEAD), F32)],
        compiler_params=_params(("parallel", "parallel")),
        name="hgrn",
    )(qvg, z, qvg, qvg, lbp, gain, *consts)


def _attn_body(lam_ref, sub_ref, q_ref, k_ref, v_ref, o_ref, acc_ref, m_ref, l_ref, s0_ref, s1_ref, *, tq, tk,
               lambda_init):
    nrep = tk // LANES
    lam = lam_ref[...]
    lam_full = (jnp.exp(jnp.sum(lam[0:1] * lam[1:2], axis=-1, keepdims=True))
                - jnp.exp(jnp.sum(lam[2:3] * lam[3:4], axis=-1, keepdims=True)) + lambda_init)
    bufs = (s0_ref, s1_ref)

    def scores(rows, j, s_ref):
        kt = k_ref[j * tk:(j + 1) * tk, :]
        q = q_ref[rows, :]
        for c in range(2):
            cs = slice(c * HEAD, (c + 1) * HEAD)
            s_ref[c] = lax.dot_general(q[:, cs], kt[:, cs], (((1,), (1,)), ((), ())), preferred_element_type=F32)

    def consume(j, s_ref, first, masked):
        vt = v_ref[j * tk:(j + 1) * tk, :]
        ps, alphas = [], []
        for c in range(2):
            s = s_ref[c]
            if masked:
                ri = lax.broadcasted_iota(jnp.int32, s.shape, 0)
                ci = lax.broadcasted_iota(jnp.int32, s.shape, 1)
                s = jnp.where(ci <= ri, s, NEG)
            m_tile = jnp.max(s, axis=-1, keepdims=True)
            if first:
                m_new = jnp.broadcast_to(m_tile, (tq, LANES))
            else:
                m_old = m_ref[c]
                m_new = jnp.maximum(m_old, m_tile)
                alphas.append(jnp.exp2(m_old - m_new))
            p = jnp.exp2(s - jnp.concatenate([m_new] * nrep, axis=1))
            l_tile = jnp.sum(p, axis=-1, keepdims=True)
            l_ref[c] = jnp.broadcast_to(l_tile, (tq, LANES)) if first else alphas[c] * l_ref[c] + l_tile
            m_ref[c] = m_new
            ps.append(p.astype(BF16))
        pv = jnp.dot(jnp.concatenate(ps, axis=0), vt, preferred_element_type=F32)
        for c in range(2):
            part = pv[c * tq:(c + 1) * tq]
            acc_ref[c] = part if first else jnp.concatenate([alphas[c]] * 2, axis=1) * acc_ref[c] + part

    for qi in range(q_ref.shape[0] // tq):
        rows = slice(qi * tq, (qi + 1) * tq)
        scores(rows, 0, bufs[0])
        for j in range(qi + 1):
            if j < qi:
                scores(rows, j + 1, bufs[(j + 1) % 2])
            consume(j, bufs[j % 2], first=(j == 0), masked=(j == qi))
        inv0 = jnp.concatenate([1.0 / l_ref[0]] * 2, axis=1)
        inv1 = jnp.concatenate([1.0 / l_ref[1]] * 2, axis=1)
        o = acc_ref[0] * inv0 - lam_full * (acc_ref[1] * inv1)
        o_ref[rows, :] = (_rms(o, sub_ref[...]) * (1.0 - lambda_init)).astype(o_ref.dtype)


def _attention(q, kv, lam, subln, *, batch, seq, heads, tq, lambda_init):
    n = q.shape[0]
    vd = 2 * HEAD
    assert seq % tq == 0
    return pl.pallas_call(
        functools.partial(_attn_body, tq=tq, tk=tq, lambda_init=lambda_init),
        grid=(batch, heads),
        in_specs=[pl.BlockSpec((4, HEAD), lambda b, h: (0, 0)),
                  pl.BlockSpec((1, vd), lambda b, h: (0, 0)),
                  pl.BlockSpec((seq, vd), lambda b, h: (b, h)),
                  pl.BlockSpec((seq, vd), lambda b, h: (b, h)),
                  pl.BlockSpec((seq, vd), lambda b, h: (b, heads + h))],
        out_specs=pl.BlockSpec((seq, vd), lambda b, h: (b, h)),
        out_shape=jax.ShapeDtypeStruct((n, heads * vd), BF16),
        scratch_shapes=[pltpu.VMEM((2, tq, vd), F32), pltpu.VMEM((2, tq, LANES), F32),
                        pltpu.VMEM((2, tq, LANES), F32), pltpu.VMEM((2, tq, tq), F32), pltpu.VMEM((2, tq, tq), F32)],
        compiler_params=_params(("parallel", "parallel")),
        name="diff_attn",
    )(lam, subln, q, kv, kv)


def kernel(x, p, ffn_norm, ffn_w_gate_up, ffn_w_down, mix_norm, hgrn_w_in, hgrn_lower_bounds, hgrn_out_norm, hgrn_w_out, kv_norm, w_kv, diff_w_q, diff_lambda, diff_subln, diff_w_out, ple_norm, ple_w_gate, ple_w_proj, final_norm):
    batch, seq, d = x.shape
    depth = p.shape[0]
    n_a = hgrn_w_in.shape[0]
    n = batch * seq
    hg_heads = d // HEAD
    da_heads = d // (2 * HEAD)
    k_cols = da_heads * 2 * HEAD

    tm = min(512, seq)
    tm_big = min(1024, seq)
    tq = min(512, seq)
    wide = hg_heads * HEAD

    wgu = ffn_w_gate_up.astype(BF16)
    wdn = ffn_w_down.astype(BF16)
    w_in = hgrn_w_in.astype(BF16)
    w_ho = hgrn_w_out.astype(BF16)
    w_kvb = w_kv.astype(BF16)
    w_q = diff_w_q.astype(BF16)
    w_do = diff_w_out.astype(BF16)
    w_pg = ple_w_gate.astype(BF16)
    w_pp = ple_w_proj.astype(BF16)
    p2 = p.reshape(depth, n, p.shape[-1])

    rope = _rope_tables(seq)
    h = x.reshape(n, d)
    kv = None
    for i in range(depth):
        h = _ffn(h, ffn_norm[i, 0].reshape(1, d), wgu, wdn, i, 0, tm=tm_big, tf=512)
        if i < n_a:
            gain = mix_norm[i].reshape(1, d)
            z = _norm_mm(h, gain, w_in, (i,), tm=tm, tn=wide, out_dtype=F32, ntiles=1, src_tile=lambda j: j + 1)
            per = wide // 1024
            qvg = _norm_mm(h, gain, w_in, (i,), tm=tm_big, tn=1024, out_dtype=BF16, ntiles=3 * per,
                           src_tile=lambda j: j + jnp.where(j >= per, per, 0))
            og = _hgrn(qvg, z, hgrn_lower_bounds, hgrn_out_norm[i].reshape(1, HEAD), i,
                       batch=batch, seq=seq, heads=hg_heads)
            h = _mm_resid(og, w_ho, (i,), h, tm=tm, tn=d)
        else:
            j = i - n_a
            lambda_init = 0.8 - 0.6 * math.exp(-0.3 * i)
            q = _norm_mm(h, mix_norm[i].reshape(1, d), w_q, (j,), tm=tm, tn=k_cols, out_dtype=BF16,
                         rope=rope, seq=seq, rope_cols=k_cols, scale=HEAD ** -0.5 * math.log2(math.e))
            oa = _attention(q, kv, diff_lambda[j], diff_subln[j].reshape(1, 2 * HEAD),
                            batch=batch, seq=seq, heads=da_heads, tq=tq, lambda_init=lambda_init)
            h = _mm_resid(oa, w_do, (j,), h, tm=tm, tn=d)
        h = _ffn(h, ffn_norm[i, 1].reshape(1, d), wgu, wdn, i, 1, tm=tm_big, tf=512)
        last = i == depth - 1
        h = _ple(h, ple_norm[i].reshape(1, d), w_pg, p2, w_pp, i, final_norm.reshape(1, d),
                 tm=tm, tn=512, final=last)
        if i == n_a - 1:
            kv = _norm_mm(h, kv_norm.reshape(1, d), w_kvb, (), tm=tm_big, tn=k_cols, out_dtype=BF16,
                          rope=rope, seq=seq, rope_cols=k_cols)
    return h.reshape(batch, seq, d)
```

```python
import functools
import math

import jax
import jax.numpy as jnp
import numpy as np
from jax import lax
from jax.experimental import pallas as pl
from jax.experimental.pallas import tpu as pltpu

F32 = jnp.float32
BF16 = jnp.bfloat16

NORM_EPS = 1e-6
ROPE_THETA = 10000.0
HEAD = 128
LANES = 128
SUBLANES = 8
HG_CHUNK = 32
HG_SUB = 256
HG_UNROLL = 8
NEG = -1e30

VMEM_LIMIT = 60 * 1024 * 1024


def _params(sem):
    return pltpu.CompilerParams(dimension_semantics=sem, vmem_limit_bytes=VMEM_LIMIT)


def _rms(x, g):
    ms = jnp.mean(x * x, axis=-1, keepdims=True)
    return x * lax.rsqrt(ms + NORM_EPS) * g


def _sigmoid(x):
    return 1.0 / (1.0 + jnp.exp(-x))


def _ffn_body(x_ref, g_ref, wg_ref, wu_ref, wd_ref, wdl_ref, o_ref, xn_ref, act_ref):
    f = pl.program_id(1)
    last = pl.num_programs(1) - 1

    def gate_up():
        xn = xn_ref[...]
        gate = jnp.dot(xn, wg_ref[...], preferred_element_type=F32)
        up = jnp.dot(xn, wu_ref[...], preferred_element_type=F32)
        return (gate * _sigmoid(gate) * up).astype(BF16)

    @pl.when(f == 0)
    def _():
        x = x_ref[...]
        xn_ref[...] = _rms(x, g_ref[...]).astype(BF16)
        o_ref[...] = x
        act_ref[...] = gate_up()

    @pl.when(f > 0)
    def _():
        part = jnp.dot(act_ref[...], wd_ref[...], preferred_element_type=F32)
        act = gate_up()
        o_ref[...] += 0.5 * part
        act_ref[...] = act

    @pl.when(f == last)
    def _():
        o_ref[...] += 0.5 * jnp.dot(act_ref[...], wdl_ref[...], preferred_element_type=F32)


def _ffn(x, gain, wgu, wd, li, lj, *, tm, tf):
    n, d = x.shape
    ff = wd.shape[2]
    nf = ff // tf
    assert n % tm == 0 and ff % tf == 0 and nf >= 2
    return pl.pallas_call(
        _ffn_body,
        grid=(n // tm, nf),
        in_specs=[
            pl.BlockSpec((tm, d), lambda i, f: (i, 0)),
            pl.BlockSpec((1, d), lambda i, f: (0, 0)),
            pl.BlockSpec((None, None, d, tf), lambda i, f: (li, lj, 0, f)),
            pl.BlockSpec((None, None, d, tf), lambda i, f: (li, lj, 0, f + nf)),
            pl.BlockSpec((None, None, tf, d), lambda i, f: (li, lj, jnp.maximum(f - 1, 0), 0)),
            pl.BlockSpec((None, None, tf, d), lambda i, f: (li, lj, nf - 1, 0), pipeline_mode=pl.Buffered(1)),
        ],
        out_specs=pl.BlockSpec((tm, d), lambda i, f: (i, 0)),
        out_shape=jax.ShapeDtypeStruct((n, d), F32),
        scratch_shapes=[pltpu.VMEM((tm, d), BF16), pltpu.VMEM((tm, tf), BF16)],
        compiler_params=_params(("parallel", "arbitrary")),
        name="ffn",
    )(x, gain, wgu, wgu, wd, wd)


def _rope_store(y, cos, sin, o_ref, scale):
    for g in range(y.shape[1] // HEAD):
        yg = y[:, g * HEAD:(g + 1) * HEAD]
        r = yg * cos + pltpu.roll(yg, HEAD // 2, axis=1) * sin
        if scale != 1.0:
            r = r * scale
        o_ref[:, g * HEAD:(g + 1) * HEAD] = r.astype(o_ref.dtype)


def _norm_mm_body(x_ref, g_ref, w_ref, o_ref, xn_ref):
    @pl.when(pl.program_id(1) == 0)
    def _():
        xn_ref[...] = _rms(x_ref[...], g_ref[...]).astype(BF16)

    o_ref[...] = jnp.dot(xn_ref[...], w_ref[...], preferred_element_type=F32).astype(o_ref.dtype)


def _norm_mm_rope_body(x_ref, g_ref, w_ref, cos_ref, sin_ref, o_ref, xn_ref, *, rope_tiles, scale):
    j = pl.program_id(1)

    @pl.when(j == 0)
    def _():
        xn_ref[...] = _rms(x_ref[...], g_ref[...]).astype(BF16)

    y = jnp.dot(xn_ref[...], w_ref[...], preferred_element_type=F32)

    @pl.when(j < rope_tiles)
    def _():
        _rope_store(y, cos_ref[...], sin_ref[...], o_ref, scale)

    @pl.when(j >= rope_tiles)
    def _():
        o_ref[...] = y.astype(o_ref.dtype)


def _norm_mm(x, gain, w, w_index, *, tm, tn, out_dtype, rope=None, seq=None, rope_cols=0, scale=1.0,
             ntiles=None, src_tile=lambda j: j):
    n, d = x.shape
    nout = (w.shape[-1] // tn if ntiles is None else ntiles) * tn
    assert n % tm == 0 and w.shape[-1] % tn == 0
    nlead = w.ndim - 2
    w_mode = dict(pipeline_mode=pl.Buffered(1)) if nout == tn else {}
    w_spec = pl.BlockSpec((None,) * nlead + (d, tn), lambda i, j: tuple(w_index) + (0, src_tile(j)), **w_mode)
    in_specs = [pl.BlockSpec((tm, d), lambda i, j: (i, 0)),
                pl.BlockSpec((1, d), lambda i, j: (0, 0)),
                w_spec]
    args = [x, gain, w]
    if rope is None:
        body = _norm_mm_body
    else:
        assert seq % tm == 0 and rope_cols % tn == 0
        nseq = seq // tm
        tab_spec = pl.BlockSpec((tm, HEAD), lambda i, j: (i % nseq, 0))
        in_specs += [tab_spec, tab_spec]
        args += list(rope)
        body = functools.partial(_norm_mm_rope_body, rope_tiles=rope_cols // tn, scale=scale)
    return pl.pallas_call(
        body,
        grid=(n // tm, nout // tn),
        in_specs=in_specs,
        out_specs=pl.BlockSpec((tm, tn), lambda i, j: (i, j)),
        out_shape=jax.ShapeDtypeStruct((n, nout), out_dtype),
        scratch_shapes=[pltpu.VMEM((tm, d), BF16)],
        compiler_params=_params(("parallel", "arbitrary")),
        name="norm_mm",
    )(*args)


def _mm_resid_body(x_ref, w_ref, r_ref, o_ref):
    o_ref[...] = r_ref[...] + jnp.dot(x_ref[...], w_ref[...], preferred_element_type=F32)


def _mm_resid(x, w, w_index, resid, *, tm, tn):
    n, k = x.shape
    nout = w.shape[-1]
    assert n % tm == 0 and nout % tn == 0
    nlead = w.ndim - 2
    w_mode = dict(pipeline_mode=pl.Buffered(1)) if nout == tn else {}
    return pl.pallas_call(
        _mm_resid_body,
        grid=(n // tm, nout // tn),
        in_specs=[pl.BlockSpec((tm, k), lambda i, j: (i, 0)),
                  pl.BlockSpec((None,) * nlead + (k, tn), lambda i, j: tuple(w_index) + (0, j), **w_mode),
                  pl.BlockSpec((tm, tn), lambda i, j: (i, j))],
        out_specs=pl.BlockSpec((tm, tn), lambda i, j: (i, j)),
        out_shape=jax.ShapeDtypeStruct((n, nout), F32),
        compiler_params=_params(("parallel", "arbitrary")),
        name="mm_resid",
    )(x, w, resid)


def _ple_body(x_ref, g_ref, wg_ref, p_ref, wp_ref, fg_ref, o_ref, *, tn, final):
    x = x_ref[...]
    xn = _rms(x, g_ref[...]).astype(BF16)
    pb = p_ref[...].astype(BF16)
    d = x.shape[1]
    for c in range(d // tn):
        sl = slice(c * tn, (c + 1) * tn)
        gate = jnp.dot(xn, wg_ref[:, sl], preferred_element_type=F32)
        proj = jnp.dot(pb, wp_ref[:, sl], preferred_element_type=F32)
        o_ref[:, sl] = x_ref[:, sl] + _sigmoid(gate) * proj
    if final:
        o_ref[...] = _rms(o_ref[...], fg_ref[...])


def _ple(x, gain, wg, p, wp, layer, final_gain, *, tm, tn, final):
    n, d = x.shape
    pd = p.shape[-1]
    rows = n // tm
    assert n % tm == 0 and d % tn == 0
    return pl.pallas_call(
        functools.partial(_ple_body, tn=tn, final=final),
        grid=(rows,),
        in_specs=[pl.BlockSpec((tm, d), lambda i: (i, 0)),
                  pl.BlockSpec((1, d), lambda i: (0, 0)),
                  pl.BlockSpec((None, d, d), lambda i: (layer, 0, 0), pipeline_mode=pl.Buffered(1)),
                  pl.BlockSpec((None, tm, pd), lambda i: (layer, i, 0)),
                  pl.BlockSpec((None, pd, d), lambda i: (layer, 0, 0), pipeline_mode=pl.Buffered(1)),
                  pl.BlockSpec((1, d), lambda i: (0, 0))],
        out_specs=pl.BlockSpec((tm, d), lambda i: (i, 0)),
        out_shape=jax.ShapeDtypeStruct((n, d), F32),
        compiler_params=_params(("parallel",)),
        name="ple",
    )(x, gain, wg, p, wp, final_gain)


def _rope_table_body(cos_ref, sin_ref):
    shape = cos_ref.shape
    pos = lax.broadcasted_iota(jnp.int32, shape, 0).astype(F32)
    lane = lax.broadcasted_iota(jnp.int32, shape, 1)
    half = HEAD // 2
    k = jnp.bitwise_and(lane, half - 1).astype(F32)
    inv_freq = jnp.exp(k * (-math.log(ROPE_THETA) / half))
    ang = pos * inv_freq
    cos_ref[...] = jnp.cos(ang)
    s = jnp.sin(ang)
    sin_ref[...] = jnp.where(lane < half, -s, s)


def _rope_tables(seq):
    return pl.pallas_call(
        _rope_table_body,
        out_shape=(jax.ShapeDtypeStruct((seq, HEAD), F32), jax.ShapeDtypeStruct((seq, HEAD), F32)),
        name="rope_tables",
    )()


def _split3(x):
    hi = x.astype(BF16)
    r1 = x - hi.astype(F32)
    mid = r1.astype(BF16)
    lo = (r1 - mid.astype(F32)).astype(BF16)
    return hi, mid, lo


def _hgrn_consts():
    r = np.arange(HG_SUB)
    chunk, blk = r // HG_CHUNK, r // SUBLANES
    nch = HG_SUB // HG_CHUNK
    same_chunk = chunk[:, None] == chunk[None, :]
    tril = same_chunk & (r[None, :] <= r[:, None])
    off = same_chunk & (blk[:, None] > blk[None, :])
    diag = (blk[:, None] == blk[None, :]) & (r[None, :] <= r[:, None])
    lane_blk = np.arange(SUBLANES * HEAD) // HEAD
    spread = lane_blk[:, None] == (r % SUBLANES)[None, :]
    bpc = HG_CHUNK // SUBLANES
    kmask = np.stack([np.broadcast_to((((r % HG_CHUNK) // SUBLANES) == jb)[:, None], (HG_SUB, HEAD))
                      for jb in range(bpc - 1)])
    cmask = np.stack([np.broadcast_to((chunk == c)[None, :], (HEAD, HG_SUB)) for c in range(nch)])
    eye = np.eye(HEAD)
    return (jnp.asarray(tril, BF16), jnp.asarray(off, F32), jnp.asarray(diag, F32),
            jnp.asarray(spread, BF16), jnp.asarray(kmask, F32), jnp.asarray(cmask, F32), jnp.asarray(eye, BF16))


def _hgrn_front(q, z, v, lb, tril, kmask):
    rows = q.shape[0]
    nch = rows // HG_CHUNK
    nblk = rows // SUBLANES
    bpc = HG_CHUNK // SUBLANES

    log2e = math.log2(math.e)
    soft = jnp.log2(1.0 + jnp.exp2(jnp.abs(z) * -log2e))
    lg_lb = jnp.log2(lb)
    lg_1mlb = jnp.log2(1.0 - lb)
    lg_sig = lg_1mlb + (jnp.minimum(z, 0.0) * log2e - soft)
    lg_k = lg_1mlb - (jnp.maximum(z, 0.0) * log2e + soft)
    logf = jnp.maximum(lg_lb, lg_sig) + jnp.log2(1.0 + jnp.exp2(-jnp.abs(lg_lb - lg_sig)))
    qs = q * (HEAD ** -0.5)

    hi, mid, lo = _split3(logf)
    parts = jnp.dot(tril, jnp.concatenate([hi, mid, lo], axis=1), preferred_element_type=F32)
    b = parts[:, :HEAD] + parts[:, HEAD:2 * HEAD] + parts[:, 2 * HEAD:]

    vb = v.astype(BF16)
    c = b - lg_k
    b3 = b.reshape(nch, HG_CHUNK, HEAD)
    b_last = b3[:, HG_CHUNK - 1:HG_CHUNK, :]
    q_in = (qs * jnp.exp2(b)).astype(BF16)
    k_end = jnp.exp2(b_last - c.reshape(nch, HG_CHUNK, HEAD)).reshape(rows, HEAD).astype(BF16)
    decay_cols = jnp.exp2(b_last).reshape(nch, HEAD).T

    qs3 = qs.reshape(nch, HG_CHUNK, HEAD)
    b8 = b.reshape(nblk, SUBLANES, HEAD)
    c8 = c.reshape(nblk, SUBLANES, HEAD)
    k_blk = jnp.exp2(b8[:, SUBLANES - 1:SUBLANES, :] - c8).reshape(rows, HEAD)
    q_groups, k_groups = [], []
    for jb in range(bpc - 1):
        ref = b3[:, jb * SUBLANES + SUBLANES - 1:jb * SUBLANES + SUBLANES, :]
        qg = qs3 * jnp.exp2(jnp.minimum(b3 - ref, 0.0))
        q_groups.append(qg.reshape(rows, HEAD).astype(BF16))
        k_groups.append((k_blk * kmask[jb]).astype(BF16))
    q_cat = jnp.concatenate(q_groups, axis=1)
    k_cat = jnp.concatenate(k_groups, axis=1)

    q8 = qs.reshape(nblk, SUBLANES, HEAD)
    prods = []
    for i in range(SUBLANES):
        e = jnp.exp2(jnp.minimum(b8 - c8[:, i:i + 1, :], 0.0))
        prods.append((e * q8).reshape(rows, HEAD).astype(BF16))
    p_cat = jnp.concatenate(prods, axis=1)
    return q_cat, k_cat, p_cat, q_in, k_end, decay_cols, vb


def _hgrn_back(front, state, off, diag, spread, cmask, eye):
    q_cat, k_cat, p_cat, q_in, k_end, decay_cols, vb = front
    nch = decay_cols.shape[1]
    a = lax.dot_general(q_cat, k_cat, (((1,), (1,)), ((), ())), preferred_element_type=F32) * off
    a = a + jnp.dot(p_cat, spread, preferred_element_type=F32) * diag
    o = jnp.dot(a.astype(BF16), vb, preferred_element_type=F32)

    k_end_t = lax.dot_general(eye, k_end, (((1,), (1,)), ((), ())), preferred_element_type=F32)
    k_stack = jnp.concatenate([(k_end_t * cmask[c]).astype(BF16) for c in range(nch)], axis=0)
    upd = jnp.dot(k_stack, vb, preferred_element_type=F32)

    outs = []
    for c in range(nch):
        outs.append(jnp.dot(q_in[c * HG_CHUNK:(c + 1) * HG_CHUNK], state.astype(BF16), preferred_element_type=F32))
        state = state * decay_cols[:, c:c + 1] + upd[c * HEAD:(c + 1) * HEAD]
    o = o + jnp.concatenate(outs, axis=0)
    return o, state


def _hgrn_body(q_ref, z_ref, v_ref, g_ref, lbp_ref, gain_ref, tril_ref, off_ref, diag_ref, spread_ref, kmask_ref,
               cmask_ref, eye_ref, o_ref, state_ref, *, layer):
    lbp = lbp_ref[...]
    e = jnp.exp(lbp - jnp.max(lbp, axis=0, keepdims=True))
    lb = jnp.sum(e[:layer + 1], axis=0, keepdims=True) / jnp.sum(e, axis=0, keepdims=True)

    state_ref[...] = jnp.zeros_like(state_ref)
    consts = (tril_ref[...], off_ref[...], diag_ref[...], spread_ref[...], kmask_ref[...], cmask_ref[...],
              eye_ref[...])

    def slabs(s, carry):
        state = state_ref[...]
        sls = [pl.ds(pl.multiple_of((s * HG_UNROLL + u) * HG_SUB, HG_SUB), HG_SUB) for u in range(HG_UNROLL)]
        fronts = [_hgrn_front(q_ref[sl, :].astype(F32), z_ref[sl, :], v_ref[sl, :].astype(F32), lb, consts[0],
                              consts[4]) for sl in sls]
        for sl, front in zip(sls, fronts):
            o, state = _hgrn_back(front, state, consts[1], consts[2], consts[3], consts[5], consts[6])
            on = _rms(o, gain_ref[...])
            g = g_ref[sl, :].astype(F32)
            o_ref[sl, :] = (on * (g * _sigmoid(g))).astype(o_ref.dtype)
        state_ref[...] = state
        return carry

    lax.fori_loop(0, q_ref.shape[0] // (HG_SUB * HG_UNROLL), slabs, 0)


def _hgrn(qvg, z, lbp, gain, layer, *, batch, seq, heads):
    n = z.shape[0]
    assert seq % (HG_SUB * HG_UNROLL) == 0
    consts = _hgrn_consts()
    sec = lambda k: pl.BlockSpec((seq, HEAD), lambda b, h: (b, k * heads + h))
    whole = lambda a: pl.BlockSpec(a.shape, lambda b, h: (0,) * a.ndim)
    return pl.pallas_call(
        functools.partial(_hgrn_body, layer=layer),
        grid=(batch, heads),
        in_specs=[sec(0), sec(0), sec(1), sec(2),
                  pl.BlockSpec((lbp.shape[0], HEAD), lambda b, h: (0, h)),
                  pl.BlockSpec((1, HEAD), lambda b, h: (0, 0))] + [whole(a) for a in consts],
        out_specs=pl.BlockSpec((seq, HEAD), lambda b, h: (b, h)),
        out_shape=jax.ShapeDtypeStruct((n, heads * HEAD), BF16),
        scratch_shapes=[pltpu.VMEM((HEAD, HEAD), F32)],
        compiler_params=_params(("parallel", "parallel")),
        name="hgrn",
    )(qvg, z, qvg, qvg, lbp, gain, *consts)


def _attn_body(lam_ref, sub_ref, q_ref, k_ref, v_ref, o_ref, acc_ref, m_ref, l_ref, s0_ref, s1_ref, *, tq, tk,
               lambda_init):
    nrep = tk // LANES
    lam = lam_ref[...]
    lam_full = (jnp.exp(jnp.sum(lam[0:1] * lam[1:2], axis=-1, keepdims=True))
                - jnp.exp(jnp.sum(lam[2:3] * lam[3:4], axis=-1, keepdims=True)) + lambda_init)
    bufs = (s0_ref, s1_ref)

    def scores(rows, j, s_ref):
        kt = k_ref[j * tk:(j + 1) * tk, :]
        q = q_ref[rows, :]
        for c in range(2):
            cs = slice(c * HEAD, (c + 1) * HEAD)
            s_ref[c] = lax.dot_general(q[:, cs], kt[:, cs], (((1,), (1,)), ((), ())), preferred_element_type=F32)

    def consume(j, s_ref, first, masked):
        vt = v_ref[j * tk:(j + 1) * tk, :]
        ps, alphas = [], []
        for c in range(2):
            s = s_ref[c]
            if masked:
                ri = lax.broadcasted_iota(jnp.int32, s.shape, 0)
                ci = lax.broadcasted_iota(jnp.int32, s.shape, 1)
                s = jnp.where(ci <= ri, s, NEG)
            m_tile = jnp.max(s, axis=-1, keepdims=True)
            if first:
                m_new = jnp.broadcast_to(m_tile, (tq, LANES))
            else:
                m_old = m_ref[c]
                m_new = jnp.maximum(m_old, m_tile)
                alphas.append(jnp.exp2(m_old - m_new))
            p = jnp.exp2(s - jnp.concatenate([m_new] * nrep, axis=1))
            l_tile = jnp.sum(p, axis=-1, keepdims=True)
            l_ref[c] = jnp.broadcast_to(l_tile, (tq, LANES)) if first else alphas[c] * l_ref[c] + l_tile
            m_ref[c] = m_new
            ps.append(p.astype(BF16))
        pv = jnp.dot(jnp.concatenate(ps, axis=0), vt, preferred_element_type=F32)
        for c in range(2):
            part = pv[c * tq:(c + 1) * tq]
            acc_ref[c] = part if first else jnp.concatenate([alphas[c]] * 2, axis=1) * acc_ref[c] + part

    for qi in range(q_ref.shape[0] // tq):
        rows = slice(qi * tq, (qi + 1) * tq)
        scores(rows, 0, bufs[0])
        for j in range(qi + 1):
            if j < qi:
                scores(rows, j + 1, bufs[(j + 1) % 2])
            consume(j, bufs[j % 2], first=(j == 0), masked=(j == qi))
        inv0 = jnp.concatenate([1.0 / l_ref[0]] * 2, axis=1)
        inv1 = jnp.concatenate([1.0 / l_ref[1]] * 2, axis=1)
        o = acc_ref[0] * inv0 - lam_full * (acc_ref[1] * inv1)
        o_ref[rows, :] = (_rms(o, sub_ref[...]) * (1.0 - lambda_init)).astype(o_ref.dtype)


def _attention(q, kv, lam, subln, *, batch, seq, heads, tq, lambda_init):
    n = q.shape[0]
    vd = 2 * HEAD
    assert seq % tq == 0
    return pl.pallas_call(
        functools.partial(_attn_body, tq=tq, tk=tq, lambda_init=lambda_init),
        grid=(batch, heads),
        in_specs=[pl.BlockSpec((4, HEAD), lambda b, h: (0, 0)),
                  pl.BlockSpec((1, vd), lambda b, h: (0, 0)),
                  pl.BlockSpec((seq, vd), lambda b, h: (b, h)),
                  pl.BlockSpec((seq, vd), lambda b, h: (b, h)),
                  pl.BlockSpec((seq, vd), lambda b, h: (b, heads + h))],
        out_specs=pl.BlockSpec((seq, vd), lambda b, h: (b, h)),
        out_shape=jax.ShapeDtypeStruct((n, heads * vd), BF16),
        scratch_shapes=[pltpu.VMEM((2, tq, vd), F32), pltpu.VMEM((2, tq, LANES), F32),
                        pltpu.VMEM((2, tq, LANES), F32), pltpu.VMEM((2, tq, tq), F32), pltpu.VMEM((2, tq, tq), F32)],
        compiler_params=_params(("parallel", "parallel")),
        name="diff_attn",
    )(lam, subln, q, kv, kv)


def kernel(x, p, ffn_norm, ffn_w_gate_up, ffn_w_down, mix_norm, hgrn_w_in, hgrn_lower_bounds, hgrn_out_norm, hgrn_w_out, kv_norm, w_kv, diff_w_q, diff_lambda, diff_subln, diff_w_out, ple_norm, ple_w_gate, ple_w_proj, final_norm):
    batch, seq, d = x.shape
    depth = p.shape[0]
    n_a = hgrn_w_in.shape[0]
    n = batch * seq
    hg_heads = d // HEAD
    da_heads = d // (2 * HEAD)
    k_cols = da_heads * 2 * HEAD

    tm = min(1024, seq)
    tq = min(512, seq)
    wide = hg_heads * HEAD

    wgu = ffn_w_gate_up.astype(BF16)
    wdn = ffn_w_down.astype(BF16)
    w_in = hgrn_w_in.astype(BF16)
    w_ho = hgrn_w_out.astype(BF16)
    w_kvb = w_kv.astype(BF16)
    w_q = diff_w_q.astype(BF16)
    w_do = diff_w_out.astype(BF16)
    w_pg = ple_w_gate.astype(BF16)
    w_pp = ple_w_proj.astype(BF16)
    p2 = p.reshape(depth, n, p.shape[-1])

    rope = _rope_tables(seq)
    h = x.reshape(n, d)
    kv = None
    for i in range(depth):
        h = _ffn(h, ffn_norm[i, 0].reshape(1, d), wgu, wdn, i, 0, tm=tm, tf=512)
        if i < n_a:
            gain = mix_norm[i].reshape(1, d)
            z = _norm_mm(h, gain, w_in, (i,), tm=tm, tn=wide, out_dtype=F32, ntiles=1, src_tile=lambda j: j + 1)
            per = wide // 1024
            qvg = _norm_mm(h, gain, w_in, (i,), tm=tm, tn=1024, out_dtype=BF16, ntiles=3 * per,
                           src_tile=lambda j: j + jnp.where(j >= per, per, 0))
            og = _hgrn(qvg, z, hgrn_lower_bounds, hgrn_out_norm[i].reshape(1, HEAD), i,
                       batch=batch, seq=seq, heads=hg_heads)
            h = _mm_resid(og, w_ho, (i,), h, tm=tm, tn=d)
        else:
            j = i - n_a
            lambda_init = 0.8 - 0.6 * math.exp(-0.3 * i)
            q = _norm_mm(h, mix_norm[i].reshape(1, d), w_q, (j,), tm=tm, tn=k_cols, out_dtype=BF16,
                         rope=rope, seq=seq, rope_cols=k_cols, scale=HEAD ** -0.5 * math.log2(math.e))
            oa = _attention(q, kv, diff_lambda[j], diff_subln[j].reshape(1, 2 * HEAD),
                            batch=batch, seq=seq, heads=da_heads, tq=tq, lambda_init=lambda_init)
            h = _mm_resid(oa, w_do, (j,), h, tm=tm, tn=d)
        h = _ffn(h, ffn_norm[i, 1].reshape(1, d), wgu, wdn, i, 1, tm=tm, tf=512)
        last = i == depth - 1
        h = _ple(h, ple_norm[i].reshape(1, d), w_pg, p2, w_pp, i, final_norm.reshape(1, d),
                 tm=min(512, seq), tn=512, final=last)
        if i == n_a - 1:
            kv = _norm_mm(h, kv_norm.reshape(1, d), w_kvb, (), tm=tm, tn=k_cols, out_dtype=BF16,
                          rope=rope, seq=seq, rope_cols=k_cols)
    return h.reshape(batch, seq, d)
```

```python
import functools
import math

import jax
import jax.numpy as jnp
import numpy as np
from jax import lax
from jax.experimental import pallas as pl
from jax.experimental.pallas import tpu as pltpu

F32 = jnp.float32
BF16 = jnp.bfloat16

NORM_EPS = 1e-6
ROPE_THETA = 10000.0
HEAD = 128
LANES = 128
SUBLANES = 8
HG_CHUNK = 32
HG_SUB = 256
HG_UNROLL = 8
NEG = -1e30

VMEM_LIMIT = 62 * 1024 * 1024


def _params(sem):
    return pltpu.CompilerParams(dimension_semantics=sem, vmem_limit_bytes=VMEM_LIMIT)


def _rms(x, g):
    ms = jnp.mean(x * x, axis=-1, keepdims=True)
    return x * lax.rsqrt(ms + NORM_EPS) * g


def _sigmoid(x):
    return 1.0 / (1.0 + jnp.exp(-x))


def _ffn_body(x_ref, g_ref, wgu_hbm, wd_hbm, o_ref, xn_ref, act_ref, ring_ref, down_ref, sem_ring, sem_down,
              *, li, lj, tf, nf):
    i = pl.program_id(0)
    last_row = pl.num_programs(0) - 1
    base = lax.rem(i * nf, 2)

    def gate_up_copies(t, slot):
        return [pltpu.make_async_copy(wgu_hbm.at[li, lj, :, pl.ds(pl.multiple_of((t + c * nf) * tf, tf), tf)],
                                      ring_ref.at[slot, c], sem_ring.at[slot, c]) for c in range(2)]

    def down_copy(t, slot):
        return pltpu.make_async_copy(wd_hbm.at[li, lj, pl.ds(pl.multiple_of(t * tf, tf), tf), :],
                                     down_ref.at[slot], sem_down.at[slot])

    def gate_up(slot):
        xn = xn_ref[...]
        gate = jnp.dot(xn, ring_ref[slot, 0], preferred_element_type=F32)
        up = jnp.dot(xn, ring_ref[slot, 1], preferred_element_type=F32)
        return (gate * _sigmoid(gate) * up).astype(BF16)

    def step(t, ring_slot, down_slot):
        nxt = jnp.where(t + 1 == nf, 0, t + 1)
        for cp in gate_up_copies(nxt, 1 - ring_slot):
            cp.start()
        down_copy(t, 1 - down_slot).start()
        for cp in gate_up_copies(t, ring_slot):
            cp.wait()
        down_copy(t - 1, down_slot).wait()
        part = jnp.dot(act_ref[...], down_ref[down_slot], preferred_element_type=F32)
        act = gate_up(ring_slot)
        o_ref[...] += 0.5 * part
        act_ref[...] = act

    @pl.when(i == 0)
    def _():
        for cp in gate_up_copies(0, base):
            cp.start()

    for cp in gate_up_copies(1, 1 - base):
        cp.start()
    down_copy(0, 0).start()
    x = x_ref[...]
    xn_ref[...] = _rms(x, g_ref[...]).astype(BF16)
    o_ref[...] = x
    for cp in gate_up_copies(0, base):
        cp.wait()
    act_ref[...] = gate_up(base)

    def pair(k, carry):
        step(2 * k + 1, 1 - base, 0)
        step(2 * k + 2, base, 1)
        return carry

    lax.fori_loop(0, (nf - 1) // 2, pair, 0)

    @pl.when(i == last_row)
    def _():
        for cp in gate_up_copies(0, 1 - base):
            cp.wait()

    down_copy(nf - 1, 0).wait()
    o_ref[...] += 0.5 * jnp.dot(act_ref[...], down_ref[0], preferred_element_type=F32)


def _ffn(x, gain, wgu, wd, li, lj, *, tm, tf):
    n, d = x.shape
    ff = wd.shape[2]
    nf = ff // tf
    assert n % tm == 0 and ff % tf == 0 and nf >= 3 and nf % 2 == 1
    return pl.pallas_call(
        functools.partial(_ffn_body, li=li, lj=lj, tf=tf, nf=nf),
        grid=(n // tm,),
        in_specs=[
            pl.BlockSpec((tm, d), lambda i: (i, 0)),
            pl.BlockSpec((1, d), lambda i: (0, 0)),
            pl.BlockSpec(memory_space=pl.ANY),
            pl.BlockSpec(memory_space=pl.ANY),
        ],
        out_specs=pl.BlockSpec((tm, d), lambda i: (i, 0)),
        out_shape=jax.ShapeDtypeStruct((n, d), F32),
        scratch_shapes=[pltpu.VMEM((tm, d), BF16), pltpu.VMEM((tm, tf), BF16),
                        pltpu.VMEM((2, 2, d, tf), BF16), pltpu.VMEM((2, tf, d), BF16),
                        pltpu.SemaphoreType.DMA((2, 2)), pltpu.SemaphoreType.DMA((2,))],
        compiler_params=_params(("arbitrary",)),
        name="ffn",
    )(x, gain, wgu, wd)


def _rope_store(y, cos, sin, o_ref, scale):
    for g in range(y.shape[1] // HEAD):
        yg = y[:, g * HEAD:(g + 1) * HEAD]
        r = yg * cos + pltpu.roll(yg, HEAD // 2, axis=1) * sin
        if scale != 1.0:
            r = r * scale
        o_ref[:, g * HEAD:(g + 1) * HEAD] = r.astype(o_ref.dtype)


def _norm_mm_body(x_ref, g_ref, w_ref, o_ref, xn_ref):
    @pl.when(pl.program_id(1) == 0)
    def _():
        xn_ref[...] = _rms(x_ref[...], g_ref[...]).astype(BF16)

    o_ref[...] = jnp.dot(xn_ref[...], w_ref[...], preferred_element_type=F32).astype(o_ref.dtype)


def _norm_mm_rope_body(x_ref, g_ref, w_ref, cos_ref, sin_ref, o_ref, xn_ref, *, rope_tiles, scale):
    j = pl.program_id(1)

    @pl.when(j == 0)
    def _():
        xn_ref[...] = _rms(x_ref[...], g_ref[...]).astype(BF16)

    y = jnp.dot(xn_ref[...], w_ref[...], preferred_element_type=F32)

    @pl.when(j < rope_tiles)
    def _():
        _rope_store(y, cos_ref[...], sin_ref[...], o_ref, scale)

    @pl.when(j >= rope_tiles)
    def _():
        o_ref[...] = y.astype(o_ref.dtype)


def _norm_mm(x, gain, w, w_index, *, tm, tn, out_dtype, rope=None, seq=None, rope_cols=0, scale=1.0,
             ntiles=None, src_tile=lambda j: j):
    n, d = x.shape
    nout = (w.shape[-1] // tn if ntiles is None else ntiles) * tn
    assert n % tm == 0 and w.shape[-1] % tn == 0
    nlead = w.ndim - 2
    w_mode = dict(pipeline_mode=pl.Buffered(1)) if nout == tn else {}
    w_spec = pl.BlockSpec((None,) * nlead + (d, tn), lambda i, j: tuple(w_index) + (0, src_tile(j)), **w_mode)
    in_specs = [pl.BlockSpec((tm, d), lambda i, j: (i, 0)),
                pl.BlockSpec((1, d), lambda i, j: (0, 0)),
                w_spec]
    args = [x, gain, w]
    if rope is None:
        body = _norm_mm_body
    else:
        assert seq % tm == 0 and rope_cols % tn == 0
        nseq = seq // tm
        tab_spec = pl.BlockSpec((tm, HEAD), lambda i, j: (i % nseq, 0))
        in_specs += [tab_spec, tab_spec]
        args += list(rope)
        body = functools.partial(_norm_mm_rope_body, rope_tiles=rope_cols // tn, scale=scale)
    return pl.pallas_call(
        body,
        grid=(n // tm, nout // tn),
        in_specs=in_specs,
        out_specs=pl.BlockSpec((tm, tn), lambda i, j: (i, j)),
        out_shape=jax.ShapeDtypeStruct((n, nout), out_dtype),
        scratch_shapes=[pltpu.VMEM((tm, d), BF16)],
        compiler_params=_params(("parallel", "arbitrary")),
        name="norm_mm",
    )(*args)


def _mm_resid_body(x_ref, w_ref, r_ref, o_ref):
    o_ref[...] = r_ref[...] + jnp.dot(x_ref[...], w_ref[...], preferred_element_type=F32)


def _mm_resid(x, w, w_index, resid, *, tm, tn):
    n, k = x.shape
    nout = w.shape[-1]
    assert n % tm == 0 and nout % tn == 0
    nlead = w.ndim - 2
    w_mode = dict(pipeline_mode=pl.Buffered(1)) if nout == tn else {}
    return pl.pallas_call(
        _mm_resid_body,
        grid=(n // tm, nout // tn),
        in_specs=[pl.BlockSpec((tm, k), lambda i, j: (i, 0)),
                  pl.BlockSpec((None,) * nlead + (k, tn), lambda i, j: tuple(w_index) + (0, j), **w_mode),
                  pl.BlockSpec((tm, tn), lambda i, j: (i, j))],
        out_specs=pl.BlockSpec((tm, tn), lambda i, j: (i, j)),
        out_shape=jax.ShapeDtypeStruct((n, nout), F32),
        compiler_params=_params(("parallel", "arbitrary")),
        name="mm_resid",
    )(x, w, resid)


def _ple_body(x_ref, g_ref, wg_ref, p_ref, wp_ref, fg_ref, o_ref, *, tn, final):
    x = x_ref[...]
    xn = _rms(x, g_ref[...]).astype(BF16)
    pb = p_ref[...].astype(BF16)
    d = x.shape[1]
    for c in range(d // tn):
        sl = slice(c * tn, (c + 1) * tn)
        gate = jnp.dot(xn, wg_ref[:, sl], preferred_element_type=F32)
        proj = jnp.dot(pb, wp_ref[:, sl], preferred_element_type=F32)
        o_ref[:, sl] = x_ref[:, sl] + _sigmoid(gate) * proj
    if final:
        o_ref[...] = _rms(o_ref[...], fg_ref[...])


def _ple(x, gain, wg, p, wp, layer, final_gain, *, tm, tn, final):
    n, d = x.shape
    pd = p.shape[-1]
    rows = n // tm
    assert n % tm == 0 and d % tn == 0
    return pl.pallas_call(
        functools.partial(_ple_body, tn=tn, final=final),
        grid=(rows,),
        in_specs=[pl.BlockSpec((tm, d), lambda i: (i, 0)),
                  pl.BlockSpec((1, d), lambda i: (0, 0)),
                  pl.BlockSpec((None, d, d), lambda i: (layer, 0, 0), pipeline_mode=pl.Buffered(1)),
                  pl.BlockSpec((None, tm, pd), lambda i: (layer, i, 0)),
                  pl.BlockSpec((None, pd, d), lambda i: (layer, 0, 0), pipeline_mode=pl.Buffered(1)),
                  pl.BlockSpec((1, d), lambda i: (0, 0))],
        out_specs=pl.BlockSpec((tm, d), lambda i: (i, 0)),
        out_shape=jax.ShapeDtypeStruct((n, d), F32),
        compiler_params=_params(("parallel",)),
        name="ple",
    )(x, gain, wg, p, wp, final_gain)


def _rope_table_body(cos_ref, sin_ref):
    shape = cos_ref.shape
    pos = lax.broadcasted_iota(jnp.int32, shape, 0).astype(F32)
    lane = lax.broadcasted_iota(jnp.int32, shape, 1)
    half = HEAD // 2
    k = jnp.bitwise_and(lane, half - 1).astype(F32)
    inv_freq = jnp.exp(k * (-math.log(ROPE_THETA) / half))
    ang = pos * inv_freq
    cos_ref[...] = jnp.cos(ang)
    s = jnp.sin(ang)
    sin_ref[...] = jnp.where(lane < half, -s, s)


def _rope_tables(seq):
    return pl.pallas_call(
        _rope_table_body,
        out_shape=(jax.ShapeDtypeStruct((seq, HEAD), F32), jax.ShapeDtypeStruct((seq, HEAD), F32)),
        name="rope_tables",
    )()


def _split3(x):
    hi = x.astype(BF16)
    r1 = x - hi.astype(F32)
    mid = r1.astype(BF16)
    lo = (r1 - mid.astype(F32)).astype(BF16)
    return hi, mid, lo


def _hgrn_consts():
    r = np.arange(HG_SUB)
    chunk, blk = r // HG_CHUNK, r // SUBLANES
    nch = HG_SUB // HG_CHUNK
    same_chunk = chunk[:, None] == chunk[None, :]
    tril = same_chunk & (r[None, :] <= r[:, None])
    off = same_chunk & (blk[:, None] > blk[None, :])
    diag = (blk[:, None] == blk[None, :]) & (r[None, :] <= r[:, None])
    lane_blk = np.arange(SUBLANES * HEAD) // HEAD
    spread = lane_blk[:, None] == (r % SUBLANES)[None, :]
    bpc = HG_CHUNK // SUBLANES
    kmask = np.stack([np.broadcast_to((((r % HG_CHUNK) // SUBLANES) == jb)[:, None], (HG_SUB, HEAD))
                      for jb in range(bpc - 1)])
    cmask = np.stack([np.broadcast_to((chunk == c)[None, :], (HEAD, HG_SUB)) for c in range(nch)])
    eye = np.eye(HEAD)
    return (jnp.asarray(tril, BF16), jnp.asarray(off, F32), jnp.asarray(diag, F32),
            jnp.asarray(spread, BF16), jnp.asarray(kmask, F32), jnp.asarray(cmask, F32), jnp.asarray(eye, BF16))


def _hgrn_front(q, z, v, lb, tril, kmask):
    rows = q.shape[0]
    nch = rows // HG_CHUNK
    nblk = rows // SUBLANES
    bpc = HG_CHUNK // SUBLANES

    log2e = math.log2(math.e)
    soft = jnp.log2(1.0 + jnp.exp2(jnp.abs(z) * -log2e))
    lg_lb = jnp.log2(lb)
    lg_1mlb = jnp.log2(1.0 - lb)
    lg_sig = lg_1mlb + (jnp.minimum(z, 0.0) * log2e - soft)
    lg_k = lg_1mlb - (jnp.maximum(z, 0.0) * log2e + soft)
    logf = jnp.maximum(lg_lb, lg_sig) + jnp.log2(1.0 + jnp.exp2(-jnp.abs(lg_lb - lg_sig)))
    qs = q * (HEAD ** -0.5)

    hi, mid, lo = _split3(logf)
    parts = jnp.dot(tril, jnp.concatenate([hi, mid, lo], axis=1), preferred_element_type=F32)
    b = parts[:, :HEAD] + parts[:, HEAD:2 * HEAD] + parts[:, 2 * HEAD:]

    vb = v.astype(BF16)
    c = b - lg_k
    b3 = b.reshape(nch, HG_CHUNK, HEAD)
    b_last = b3[:, HG_CHUNK - 1:HG_CHUNK, :]
    q_in = (qs * jnp.exp2(b)).astype(BF16)
    k_end = jnp.exp2(b_last - c.reshape(nch, HG_CHUNK, HEAD)).reshape(rows, HEAD).astype(BF16)
    decay_cols = jnp.exp2(b_last).reshape(nch, HEAD).T

    qs3 = qs.reshape(nch, HG_CHUNK, HEAD)
    b8 = b.reshape(nblk, SUBLANES, HEAD)
    c8 = c.reshape(nblk, SUBLANES, HEAD)
    k_blk = jnp.exp2(b8[:, SUBLANES - 1:SUBLANES, :] - c8).reshape(rows, HEAD)
    q_groups, k_groups = [], []
    for jb in range(bpc - 1):
        ref = b3[:, jb * SUBLANES + SUBLANES - 1:jb * SUBLANES + SUBLANES, :]
        qg = qs3 * jnp.exp2(jnp.minimum(b3 - ref, 0.0))
        q_groups.append(qg.reshape(rows, HEAD).astype(BF16))
        k_groups.append((k_blk * kmask[jb]).astype(BF16))
    q_cat = jnp.concatenate(q_groups, axis=1)
    k_cat = jnp.concatenate(k_groups, axis=1)

    q8 = qs.reshape(nblk, SUBLANES, HEAD)
    prods = []
    for i in range(SUBLANES):
        e = jnp.exp2(jnp.minimum(b8 - c8[:, i:i + 1, :], 0.0))
        prods.append((e * q8).reshape(rows, HEAD).astype(BF16))
    p_cat = jnp.concatenate(prods, axis=1)
    return q_cat, k_cat, p_cat, q_in, k_end, decay_cols, vb


def _hgrn_back(front, state, off, diag, spread, cmask, eye):
    q_cat, k_cat, p_cat, q_in, k_end, decay_cols, vb = front
    nch = decay_cols.shape[1]
    a = lax.dot_general(q_cat, k_cat, (((1,), (1,)), ((), ())), preferred_element_type=F32) * off
    a = a + jnp.dot(p_cat, spread, preferred_element_type=F32) * diag
    o = jnp.dot(a.astype(BF16), vb, preferred_element_type=F32)

    k_end_t = lax.dot_general(eye, k_end, (((1,), (1,)), ((), ())), preferred_element_type=F32)
    k_stack = jnp.concatenate([(k_end_t * cmask[c]).astype(BF16) for c in range(nch)], axis=0)
    upd = jnp.dot(k_stack, vb, preferred_element_type=F32)

    outs = []
    for c in range(nch):
        outs.append(jnp.dot(q_in[c * HG_CHUNK:(c + 1) * HG_CHUNK], state.astype(BF16), preferred_element_type=F32))
        state = state * decay_cols[:, c:c + 1] + upd[c * HEAD:(c + 1) * HEAD]
    o = o + jnp.concatenate(outs, axis=0)
    return o, state


def _hgrn_body(q_ref, z_ref, v_ref, g_ref, lbp_ref, gain_ref, tril_ref, off_ref, diag_ref, spread_ref, kmask_ref,
               cmask_ref, eye_ref, o_ref, state_ref, *, layer):
    lbp = lbp_ref[...]
    e = jnp.exp(lbp - jnp.max(lbp, axis=0, keepdims=True))
    lb = jnp.sum(e[:layer + 1], axis=0, keepdims=True) / jnp.sum(e, axis=0, keepdims=True)

    state_ref[...] = jnp.zeros_like(state_ref)
    consts = (tril_ref[...], off_ref[...], diag_ref[...], spread_ref[...], kmask_ref[...], cmask_ref[...],
              eye_ref[...])

    def slabs(s, carry):
        state = state_ref[...]
        sls = [pl.ds(pl.multiple_of((s * HG_UNROLL + u) * HG_SUB, HG_SUB), HG_SUB) for u in range(HG_UNROLL)]
        fronts = [_hgrn_front(q_ref[sl, :].astype(F32), z_ref[sl, :], v_ref[sl, :].astype(F32), lb, consts[0],
                              consts[4]) for sl in sls]
        for sl, front in zip(sls, fronts):
            o, state = _hgrn_back(front, state, consts[1], consts[2], consts[3], consts[5], consts[6])
            on = _rms(o, gain_ref[...])
            g = g_ref[sl, :].astype(F32)
            o_ref[sl, :] = (on * (g * _sigmoid(g))).astype(o_ref.dtype)
        state_ref[...] = state
        return carry

    lax.fori_loop(0, q_ref.shape[0] // (HG_SUB * HG_UNROLL), slabs, 0)


def _hgrn(qvg, z, lbp, gain, layer, *, batch, seq, heads):
    n = z.shape[0]
    assert seq % (HG_SUB * HG_UNROLL) == 0
    consts = _hgrn_consts()
    sec = lambda k: pl.BlockSpec((seq, HEAD), lambda b, h: (b, k * heads + h))
    whole = lambda a: pl.BlockSpec(a.shape, lambda b, h: (0,) * a.ndim)
    return pl.pallas_call(
        functools.partial(_hgrn_body, layer=layer),
        grid=(batch, heads),
        in_specs=[sec(0), sec(0), sec(1), sec(2),
                  pl.BlockSpec((lbp.shape[0], HEAD), lambda b, h: (0, h)),
                  pl.BlockSpec((1, HEAD), lambda b, h: (0, 0))] + [whole(a) for a in consts],
        out_specs=pl.BlockSpec((seq, HEAD), lambda b, h: (b, h)),
        out_shape=jax.ShapeDtypeStruct((n, heads * HEAD), BF16),
        scratch_shapes=[pltpu.VMEM((HEAD, HEAD), F32)],
        compiler_params=_params(("parallel", "parallel")),
        name="hgrn",
    )(qvg, z, qvg, qvg, lbp, gain, *consts)


def _attn_body(lam_ref, sub_ref, q_ref, k_ref, v_ref, o_ref, acc_ref, m_ref, l_ref, s0_ref, s1_ref, *, tq, tk,
               lambda_init):
    nrep = tk // LANES
    lam = lam_ref[...]
    lam_full = (jnp.exp(jnp.sum(lam[0:1] * lam[1:2], axis=-1, keepdims=True))
                - jnp.exp(jnp.sum(lam[2:3] * lam[3:4], axis=-1, keepdims=True)) + lambda_init)
    bufs = (s0_ref, s1_ref)

    def scores(rows, j, s_ref):
        kt = k_ref[j * tk:(j + 1) * tk, :]
        q = q_ref[rows, :]
        for c in range(2):
            cs = slice(c * HEAD, (c + 1) * HEAD)
            s_ref[c] = lax.dot_general(q[:, cs], kt[:, cs], (((1,), (1,)), ((), ())), preferred_element_type=F32)

    def consume(j, s_ref, first, masked):
        vt = v_ref[j * tk:(j + 1) * tk, :]
        ps, alphas = [], []
        for c in range(2):
            s = s_ref[c]
            if masked:
                ri = lax.broadcasted_iota(jnp.int32, s.shape, 0)
                ci = lax.broadcasted_iota(jnp.int32, s.shape, 1)
                s = jnp.where(ci <= ri, s, NEG)
            m_tile = jnp.max(s, axis=-1, keepdims=True)
            if first:
                m_new = jnp.broadcast_to(m_tile, (tq, LANES))
            else:
                m_old = m_ref[c]
                m_new = jnp.maximum(m_old, m_tile)
                alphas.append(jnp.exp2(m_old - m_new))
            p = jnp.exp2(s - jnp.concatenate([m_new] * nrep, axis=1))
            l_tile = jnp.sum(p, axis=-1, keepdims=True)
            l_ref[c] = jnp.broadcast_to(l_tile, (tq, LANES)) if first else alphas[c] * l_ref[c] + l_tile
            m_ref[c] = m_new
            ps.append(p.astype(BF16))
        pv = jnp.dot(jnp.concatenate(ps, axis=0), vt, preferred_element_type=F32)
        for c in range(2):
            part = pv[c * tq:(c + 1) * tq]
            acc_ref[c] = part if first else jnp.concatenate([alphas[c]] * 2, axis=1) * acc_ref[c] + part

    for qi in range(q_ref.shape[0] // tq):
        rows = slice(qi * tq, (qi + 1) * tq)
        scores(rows, 0, bufs[0])
        for j in range(qi + 1):
            if j < qi:
                scores(rows, j + 1, bufs[(j + 1) % 2])
            consume(j, bufs[j % 2], first=(j == 0), masked=(j == qi))
        inv0 = jnp.concatenate([1.0 / l_ref[0]] * 2, axis=1)
        inv1 = jnp.concatenate([1.0 / l_ref[1]] * 2, axis=1)
        o = acc_ref[0] * inv0 - lam_full * (acc_ref[1] * inv1)
        o_ref[rows, :] = (_rms(o, sub_ref[...]) * (1.0 - lambda_init)).astype(o_ref.dtype)


def _attention(q, kv, lam, subln, *, batch, seq, heads, tq, lambda_init):
    n = q.shape[0]
    vd = 2 * HEAD
    assert seq % tq == 0
    return pl.pallas_call(
        functools.partial(_attn_body, tq=tq, tk=tq, lambda_init=lambda_init),
        grid=(batch, heads),
        in_specs=[pl.BlockSpec((4, HEAD), lambda b, h: (0, 0)),
                  pl.BlockSpec((1, vd), lambda b, h: (0, 0)),
                  pl.BlockSpec((seq, vd), lambda b, h: (b, h)),
                  pl.BlockSpec((seq, vd), lambda b, h: (b, h)),
                  pl.BlockSpec((seq, vd), lambda b, h: (b, heads + h))],
        out_specs=pl.BlockSpec((seq, vd), lambda b, h: (b, h)),
        out_shape=jax.ShapeDtypeStruct((n, heads * vd), BF16),
        scratch_shapes=[pltpu.VMEM((2, tq, vd), F32), pltpu.VMEM((2, tq, LANES), F32),
                        pltpu.VMEM((2, tq, LANES), F32), pltpu.VMEM((2, tq, tq), F32), pltpu.VMEM((2, tq, tq), F32)],
        compiler_params=_params(("parallel", "parallel")),
        name="diff_attn",
    )(lam, subln, q, kv, kv)


def kernel(x, p, ffn_norm, ffn_w_gate_up, ffn_w_down, mix_norm, hgrn_w_in, hgrn_lower_bounds, hgrn_out_norm, hgrn_w_out, kv_norm, w_kv, diff_w_q, diff_lambda, diff_subln, diff_w_out, ple_norm, ple_w_gate, ple_w_proj, final_norm):
    batch, seq, d = x.shape
    depth = p.shape[0]
    n_a = hgrn_w_in.shape[0]
    n = batch * seq
    hg_heads = d // HEAD
    da_heads = d // (2 * HEAD)
    k_cols = da_heads * 2 * HEAD

    tm = min(1024, seq)
    tq = min(512, seq)
    wide = hg_heads * HEAD

    wgu = ffn_w_gate_up.astype(BF16)
    wdn = ffn_w_down.astype(BF16)
    w_in = hgrn_w_in.astype(BF16)
    w_ho = hgrn_w_out.astype(BF16)
    w_kvb = w_kv.astype(BF16)
    w_q = diff_w_q.astype(BF16)
    w_do = diff_w_out.astype(BF16)
    w_pg = ple_w_gate.astype(BF16)
    w_pp = ple_w_proj.astype(BF16)
    p2 = p.reshape(depth, n, p.shape[-1])

    rope = _rope_tables(seq)
    h = x.reshape(n, d)
    kv = None
    for i in range(depth):
        h = _ffn(h, ffn_norm[i, 0].reshape(1, d), wgu, wdn, i, 0, tm=tm, tf=512)
        if i < n_a:
            gain = mix_norm[i].reshape(1, d)
            z = _norm_mm(h, gain, w_in, (i,), tm=tm, tn=wide, out_dtype=F32, ntiles=1, src_tile=lambda j: j + 1)
            per = wide // 1024
            qvg = _norm_mm(h, gain, w_in, (i,), tm=tm, tn=1024, out_dtype=BF16, ntiles=3 * per,
                           src_tile=lambda j: j + jnp.where(j >= per, per, 0))
            og = _hgrn(qvg, z, hgrn_lower_bounds, hgrn_out_norm[i].reshape(1, HEAD), i,
                       batch=batch, seq=seq, heads=hg_heads)
            h = _mm_resid(og, w_ho, (i,), h, tm=tm, tn=d)
        else:
            j = i - n_a
            lambda_init = 0.8 - 0.6 * math.exp(-0.3 * i)
            q = _norm_mm(h, mix_norm[i].reshape(1, d), w_q, (j,), tm=tm, tn=k_cols, out_dtype=BF16,
                         rope=rope, seq=seq, rope_cols=k_cols, scale=HEAD ** -0.5 * math.log2(math.e))
            oa = _attention(q, kv, diff_lambda[j], diff_subln[j].reshape(1, 2 * HEAD),
                            batch=batch, seq=seq, heads=da_heads, tq=tq, lambda_init=lambda_init)
            h = _mm_resid(oa, w_do, (j,), h, tm=tm, tn=d)
        h = _ffn(h, ffn_norm[i, 1].reshape(1, d), wgu, wdn, i, 1, tm=tm, tf=512)
        last = i == depth - 1
        h = _ple(h, ple_norm[i].reshape(1, d), w_pg, p2, w_pp, i, final_norm.reshape(1, d),
                 tm=min(512, seq), tn=512, final=last)
        if i == n_a - 1:
            kv = _norm_mm(h, kv_norm.reshape(1, d), w_kvb, (), tm=tm, tn=k_cols, out_dtype=BF16,
                          rope=rope, seq=seq, rope_cols=k_cols)
    return h.reshape(batch, seq, d)
```

```python
import functools
import math

import jax
import jax.numpy as jnp
import numpy as np
from jax import lax
from jax.experimental import pallas as pl
from jax.experimental.pallas import tpu as pltpu

F32 = jnp.float32
BF16 = jnp.bfloat16

NORM_EPS = 1e-6
ROPE_THETA = 10000.0
HEAD = 128
LANES = 128
SUBLANES = 8
HG_CHUNK = 32
HG_SUB = 256
HG_UNROLL = 8
NEG = -1e30

VMEM_LIMIT = 60 * 1024 * 1024


def _params(sem):
    return pltpu.CompilerParams(dimension_semantics=sem, vmem_limit_bytes=VMEM_LIMIT)


def _rms(x, g):
    ms = jnp.mean(x * x, axis=-1, keepdims=True)
    return x * lax.rsqrt(ms + NORM_EPS) * g


def _sigmoid(x):
    return 1.0 / (1.0 + jnp.exp(-x))


def _ffn_body(x_ref, g_ref, wg_ref, wu_ref, wd_ref, wdl_ref, o_ref, xn_ref, act_ref):
    f = pl.program_id(1)
    last = pl.num_programs(1) - 1

    def gate_up():
        xn = xn_ref[...]
        gate = jnp.dot(xn, wg_ref[...], preferred_element_type=F32)
        up = jnp.dot(xn, wu_ref[...], preferred_element_type=F32)
        return (gate * _sigmoid(gate) * up).astype(BF16)

    @pl.when(f == 0)
    def _():
        x = x_ref[...]
        xn_ref[...] = _rms(x, g_ref[...]).astype(BF16)
        o_ref[...] = x
        act_ref[...] = gate_up()

    @pl.when(f > 0)
    def _():
        part = jnp.dot(act_ref[...], wd_ref[...], preferred_element_type=F32)
        act = gate_up()
        o_ref[...] += 0.5 * part
        act_ref[...] = act

    @pl.when(f == last)
    def _():
        o_ref[...] += 0.5 * jnp.dot(act_ref[...], wdl_ref[...], preferred_element_type=F32)


def _ffn(x, gain, wgu, wd, li, lj, *, tm, tf):
    n, d = x.shape
    ff = wd.shape[2]
    nf = ff // tf
    assert n % tm == 0 and ff % tf == 0 and nf >= 2
    return pl.pallas_call(
        _ffn_body,
        grid=(n // tm, nf),
        in_specs=[
            pl.BlockSpec((tm, d), lambda i, f: (i, 0)),
            pl.BlockSpec((1, d), lambda i, f: (0, 0)),
            pl.BlockSpec((None, None, d, tf), lambda i, f: (li, lj, 0, f)),
            pl.BlockSpec((None, None, d, tf), lambda i, f: (li, lj, 0, f + nf)),
            pl.BlockSpec((None, None, tf, d), lambda i, f: (li, lj, jnp.maximum(f - 1, 0), 0)),
            pl.BlockSpec((None, None, tf, d), lambda i, f: (li, lj, nf - 1, 0), pipeline_mode=pl.Buffered(1)),
        ],
        out_specs=pl.BlockSpec((tm, d), lambda i, f: (i, 0)),
        out_shape=jax.ShapeDtypeStruct((n, d), F32),
        scratch_shapes=[pltpu.VMEM((tm, d), BF16), pltpu.VMEM((tm, tf), BF16)],
        compiler_params=_params(("parallel", "arbitrary")),
        name="ffn",
    )(x, gain, wgu, wgu, wd, wd)


def _rope_store(y, cos, sin, o_ref, scale):
    for g in range(y.shape[1] // HEAD):
        yg = y[:, g * HEAD:(g + 1) * HEAD]
        r = yg * cos + pltpu.roll(yg, HEAD // 2, axis=1) * sin
        if scale != 1.0:
            r = r * scale
        o_ref[:, g * HEAD:(g + 1) * HEAD] = r.astype(o_ref.dtype)


def _norm_mm_body(x_ref, g_ref, w_ref, o_ref, xn_ref):
    @pl.when(pl.program_id(1) == 0)
    def _():
        xn_ref[...] = _rms(x_ref[...], g_ref[...]).astype(BF16)

    o_ref[...] = jnp.dot(xn_ref[...], w_ref[...], preferred_element_type=F32).astype(o_ref.dtype)


def _norm_mm_rope_body(x_ref, g_ref, w_ref, cos_ref, sin_ref, o_ref, xn_ref, *, rope_tiles, scale):
    j = pl.program_id(1)

    @pl.when(j == 0)
    def _():
        xn_ref[...] = _rms(x_ref[...], g_ref[...]).astype(BF16)

    y = jnp.dot(xn_ref[...], w_ref[...], preferred_element_type=F32)

    @pl.when(j < rope_tiles)
    def _():
        _rope_store(y, cos_ref[...], sin_ref[...], o_ref, scale)

    @pl.when(j >= rope_tiles)
    def _():
        o_ref[...] = y.astype(o_ref.dtype)


def _norm_mm(x, gain, w, w_index, *, tm, tn, out_dtype, rope=None, seq=None, rope_cols=0, scale=1.0,
             ntiles=None, src_tile=lambda j: j):
    n, d = x.shape
    nout = (w.shape[-1] // tn if ntiles is None else ntiles) * tn
    assert n % tm == 0 and w.shape[-1] % tn == 0
    nlead = w.ndim - 2
    w_mode = dict(pipeline_mode=pl.Buffered(1)) if nout == tn else {}
    w_spec = pl.BlockSpec((None,) * nlead + (d, tn), lambda i, j: tuple(w_index) + (0, src_tile(j)), **w_mode)
    in_specs = [pl.BlockSpec((tm, d), lambda i, j: (i, 0)),
                pl.BlockSpec((1, d), lambda i, j: (0, 0)),
                w_spec]
    args = [x, gain, w]
    if rope is None:
        body = _norm_mm_body
    else:
        assert seq % tm == 0 and rope_cols % tn == 0
        nseq = seq // tm
        tab_spec = pl.BlockSpec((tm, HEAD), lambda i, j: (i % nseq, 0))
        in_specs += [tab_spec, tab_spec]
        args += list(rope)
        body = functools.partial(_norm_mm_rope_body, rope_tiles=rope_cols // tn, scale=scale)
    return pl.pallas_call(
        body,
        grid=(n // tm, nout // tn),
        in_specs=in_specs,
        out_specs=pl.BlockSpec((tm, tn), lambda i, j: (i, j)),
        out_shape=jax.ShapeDtypeStruct((n, nout), out_dtype),
        scratch_shapes=[pltpu.VMEM((tm, d), BF16)],
        compiler_params=_params(("parallel", "arbitrary")),
        name="norm_mm",
    )(*args)


def _norm_mm_rows_body(x_ref, g_ref, w_ref, cos_ref, sin_ref, o_ref, *, tn, rope_tiles, scale):
    xn = _rms(x_ref[...], g_ref[...]).astype(BF16)
    cos, sin = cos_ref[...], sin_ref[...]
    for t in range(w_ref.shape[1] // tn):
        cols = slice(t * tn, (t + 1) * tn)
        y = jnp.dot(xn, w_ref[:, cols], preferred_element_type=F32)
        if t < rope_tiles:
            _rope_store(y, cos, sin, o_ref.at[:, cols], scale)
        else:
            o_ref[:, cols] = y.astype(o_ref.dtype)


def _norm_mm_rows(x, gain, w, w_index, rope, *, tm, tn, seq, rope_cols, scale=1.0):
    n, d = x.shape
    nout = w.shape[-1]
    assert n % tm == 0 and nout % tn == 0 and seq % tm == 0 and rope_cols % tn == 0
    nlead = w.ndim - 2
    nseq = seq // tm
    tab_spec = pl.BlockSpec((tm, HEAD), lambda i: (i % nseq, 0))
    return pl.pallas_call(
        functools.partial(_norm_mm_rows_body, tn=tn, rope_tiles=rope_cols // tn, scale=scale),
        grid=(n // tm,),
        in_specs=[pl.BlockSpec((tm, d), lambda i: (i, 0)),
                  pl.BlockSpec((1, d), lambda i: (0, 0)),
                  pl.BlockSpec((None,) * nlead + (d, nout), lambda i: tuple(w_index) + (0, 0),
                               pipeline_mode=pl.Buffered(1)),
                  tab_spec, tab_spec],
        out_specs=pl.BlockSpec((tm, nout), lambda i: (i, 0)),
        out_shape=jax.ShapeDtypeStruct((n, nout), BF16),
        compiler_params=_params(("parallel",)),
        name="norm_mm_rows",
    )(x, gain, w, *rope)


def _mm_resid_body(x_ref, w_ref, r_ref, o_ref):
    o_ref[...] = r_ref[...] + jnp.dot(x_ref[...], w_ref[...], preferred_element_type=F32)


def _mm_resid(x, w, w_index, resid, *, tm, tn):
    n, k = x.shape
    nout = w.shape[-1]
    assert n % tm == 0 and nout % tn == 0
    nlead = w.ndim - 2
    w_mode = dict(pipeline_mode=pl.Buffered(1)) if nout == tn else {}
    return pl.pallas_call(
        _mm_resid_body,
        grid=(n // tm, nout // tn),
        in_specs=[pl.BlockSpec((tm, k), lambda i, j: (i, 0)),
                  pl.BlockSpec((None,) * nlead + (k, tn), lambda i, j: tuple(w_index) + (0, j), **w_mode),
                  pl.BlockSpec((tm, tn), lambda i, j: (i, j))],
        out_specs=pl.BlockSpec((tm, tn), lambda i, j: (i, j)),
        out_shape=jax.ShapeDtypeStruct((n, nout), F32),
        compiler_params=_params(("parallel", "arbitrary")),
        name="mm_resid",
    )(x, w, resid)


def _ple_body(x_ref, g_ref, wg_ref, p_ref, wp_ref, fg_ref, o_ref, *, tn, final):
    x = x_ref[...]
    xn = _rms(x, g_ref[...]).astype(BF16)
    pb = p_ref[...].astype(BF16)
    d = x.shape[1]
    for c in range(d // tn):
        sl = slice(c * tn, (c + 1) * tn)
        gate = jnp.dot(xn, wg_ref[:, sl], preferred_element_type=F32)
        proj = jnp.dot(pb, wp_ref[:, sl], preferred_element_type=F32)
        o_ref[:, sl] = x_ref[:, sl] + _sigmoid(gate) * proj
    if final:
        o_ref[...] = _rms(o_ref[...], fg_ref[...])


def _ple(x, gain, wg, p, wp, layer, final_gain, *, tm, tn, final):
    n, d = x.shape
    pd = p.shape[-1]
    rows = n // tm
    assert n % tm == 0 and d % tn == 0
    return pl.pallas_call(
        functools.partial(_ple_body, tn=tn, final=final),
        grid=(rows,),
        in_specs=[pl.BlockSpec((tm, d), lambda i: (i, 0)),
                  pl.BlockSpec((1, d), lambda i: (0, 0)),
                  pl.BlockSpec((None, d, d), lambda i: (layer, 0, 0), pipeline_mode=pl.Buffered(1)),
                  pl.BlockSpec((None, tm, pd), lambda i: (layer, i, 0)),
                  pl.BlockSpec((None, pd, d), lambda i: (layer, 0, 0), pipeline_mode=pl.Buffered(1)),
                  pl.BlockSpec((1, d), lambda i: (0, 0))],
        out_specs=pl.BlockSpec((tm, d), lambda i: (i, 0)),
        out_shape=jax.ShapeDtypeStruct((n, d), F32),
        compiler_params=_params(("parallel",)),
        name="ple",
    )(x, gain, wg, p, wp, final_gain)


def _rope_table_body(cos_ref, sin_ref):
    shape = cos_ref.shape
    pos = lax.broadcasted_iota(jnp.int32, shape, 0).astype(F32)
    lane = lax.broadcasted_iota(jnp.int32, shape, 1)
    half = HEAD // 2
    k = jnp.bitwise_and(lane, half - 1).astype(F32)
    inv_freq = jnp.exp(k * (-math.log(ROPE_THETA) / half))
    ang = pos * inv_freq
    cos_ref[...] = jnp.cos(ang)
    s = jnp.sin(ang)
    sin_ref[...] = jnp.where(lane < half, -s, s)


def _rope_tables(seq):
    return pl.pallas_call(
        _rope_table_body,
        out_shape=(jax.ShapeDtypeStruct((seq, HEAD), F32), jax.ShapeDtypeStruct((seq, HEAD), F32)),
        name="rope_tables",
    )()


def _split3(x):
    hi = x.astype(BF16)
    r1 = x - hi.astype(F32)
    mid = r1.astype(BF16)
    lo = (r1 - mid.astype(F32)).astype(BF16)
    return hi, mid, lo


def _hgrn_consts():
    r = np.arange(HG_SUB)
    chunk, blk = r // HG_CHUNK, r // SUBLANES
    nch = HG_SUB // HG_CHUNK
    same_chunk = chunk[:, None] == chunk[None, :]
    tril = same_chunk & (r[None, :] <= r[:, None])
    off = same_chunk & (blk[:, None] > blk[None, :])
    diag = (blk[:, None] == blk[None, :]) & (r[None, :] <= r[:, None])
    lane_blk = np.arange(SUBLANES * HEAD) // HEAD
    spread = lane_blk[:, None] == (r % SUBLANES)[None, :]
    bpc = HG_CHUNK // SUBLANES
    kmask = np.stack([np.broadcast_to((((r % HG_CHUNK) // SUBLANES) == jb)[:, None], (HG_SUB, HEAD))
                      for jb in range(bpc - 1)])
    cmask = np.stack([np.broadcast_to((chunk == c)[None, :], (HEAD, HG_SUB)) for c in range(nch)])
    eye = np.eye(HEAD)
    return (jnp.asarray(tril, BF16), jnp.asarray(off, F32), jnp.asarray(diag, F32),
            jnp.asarray(spread, BF16), jnp.asarray(kmask, F32), jnp.asarray(cmask, F32), jnp.asarray(eye, BF16))


def _hgrn_front(q, z, v, lb, tril, kmask):
    rows = q.shape[0]
    nch = rows // HG_CHUNK
    nblk = rows // SUBLANES
    bpc = HG_CHUNK // SUBLANES

    log2e = math.log2(math.e)
    soft = jnp.log2(1.0 + jnp.exp2(jnp.abs(z) * -log2e))
    lg_lb = jnp.log2(lb)
    lg_1mlb = jnp.log2(1.0 - lb)
    lg_sig = lg_1mlb + (jnp.minimum(z, 0.0) * log2e - soft)
    lg_k = lg_1mlb - (jnp.maximum(z, 0.0) * log2e + soft)
    logf = jnp.maximum(lg_lb, lg_sig) + jnp.log2(1.0 + jnp.exp2(-jnp.abs(lg_lb - lg_sig)))
    qs = q * (HEAD ** -0.5)

    hi, mid, lo = _split3(logf)
    parts = jnp.dot(tril, jnp.concatenate([hi, mid, lo], axis=1), preferred_element_type=F32)
    b = parts[:, :HEAD] + parts[:, HEAD:2 * HEAD] + parts[:, 2 * HEAD:]

    vb = v.astype(BF16)
    c = b - lg_k
    b3 = b.reshape(nch, HG_CHUNK, HEAD)
    b_last = b3[:, HG_CHUNK - 1:HG_CHUNK, :]
    q_in = (qs * jnp.exp2(b)).astype(BF16)
    k_end = jnp.exp2(b_last - c.reshape(nch, HG_CHUNK, HEAD)).reshape(rows, HEAD).astype(BF16)
    decay_cols = jnp.exp2(b_last).reshape(nch, HEAD).T

    qs3 = qs.reshape(nch, HG_CHUNK, HEAD)
    b8 = b.reshape(nblk, SUBLANES, HEAD)
    c8 = c.reshape(nblk, SUBLANES, HEAD)
    k_blk = jnp.exp2(b8[:, SUBLANES - 1:SUBLANES, :] - c8).reshape(rows, HEAD)
    q_groups, k_groups = [], []
    for jb in range(bpc - 1):
        ref = b3[:, jb * SUBLANES + SUBLANES - 1:jb * SUBLANES + SUBLANES, :]
        qg = qs3 * jnp.exp2(jnp.minimum(b3 - ref, 0.0))
        q_groups.append(qg.reshape(rows, HEAD).astype(BF16))
        k_groups.append((k_blk * kmask[jb]).astype(BF16))
    q_cat = jnp.concatenate(q_groups, axis=1)
    k_cat = jnp.concatenate(k_groups, axis=1)

    q8 = qs.reshape(nblk, SUBLANES, HEAD)
    prods = []
    for i in range(SUBLANES):
        e = jnp.exp2(jnp.minimum(b8 - c8[:, i:i + 1, :], 0.0))
        prods.append((e * q8).reshape(rows, HEAD).astype(BF16))
    p_cat = jnp.concatenate(prods, axis=1)
    return q_cat, k_cat, p_cat, q_in, k_end, decay_cols, vb


def _hgrn_back(front, state, off, diag, spread, cmask, eye):
    q_cat, k_cat, p_cat, q_in, k_end, decay_cols, vb = front
    nch = decay_cols.shape[1]
    a = lax.dot_general(q_cat, k_cat, (((1,), (1,)), ((), ())), preferred_element_type=F32) * off
    a = a + jnp.dot(p_cat, spread, preferred_element_type=F32) * diag
    o = jnp.dot(a.astype(BF16), vb, preferred_element_type=F32)

    k_end_t = lax.dot_general(eye, k_end, (((1,), (1,)), ((), ())), preferred_element_type=F32)
    k_stack = jnp.concatenate([(k_end_t * cmask[c]).astype(BF16) for c in range(nch)], axis=0)
    upd = jnp.dot(k_stack, vb, preferred_element_type=F32)

    outs = []
    for c in range(nch):
        outs.append(jnp.dot(q_in[c * HG_CHUNK:(c + 1) * HG_CHUNK], state.astype(BF16), preferred_element_type=F32))
        state = state * decay_cols[:, c:c + 1] + upd[c * HEAD:(c + 1) * HEAD]
    o = o + jnp.concatenate(outs, axis=0)
    return o, state


def _hgrn_body(q_ref, z_ref, v_ref, g_ref, lbp_ref, gain_ref, tril_ref, off_ref, diag_ref, spread_ref, kmask_ref,
               cmask_ref, eye_ref, o_ref, state_ref, *, layer):
    lbp = lbp_ref[...]
    e = jnp.exp(lbp - jnp.max(lbp, axis=0, keepdims=True))
    lb = jnp.sum(e[:layer + 1], axis=0, keepdims=True) / jnp.sum(e, axis=0, keepdims=True)

    state_ref[...] = jnp.zeros_like(state_ref)
    consts = (tril_ref[...], off_ref[...], diag_ref[...], spread_ref[...], kmask_ref[...], cmask_ref[...],
              eye_ref[...])

    def slabs(s, carry):
        state = state_ref[...]
        sls = [pl.ds(pl.multiple_of((s * HG_UNROLL + u) * HG_SUB, HG_SUB), HG_SUB) for u in range(HG_UNROLL)]
        fronts = [_hgrn_front(q_ref[sl, :].astype(F32), z_ref[sl, :], v_ref[sl, :].astype(F32), lb, consts[0],
                              consts[4]) for sl in sls]
        for sl, front in zip(sls, fronts):
            o, state = _hgrn_back(front, state, consts[1], consts[2], consts[3], consts[5], consts[6])
            on = _rms(o, gain_ref[...])
            g = g_ref[sl, :].astype(F32)
            o_ref[sl, :] = (on * (g * _sigmoid(g))).astype(o_ref.dtype)
        state_ref[...] = state
        return carry

    lax.fori_loop(0, q_ref.shape[0] // (HG_SUB * HG_UNROLL), slabs, 0)


def _hgrn(qvg, z, lbp, gain, layer, *, batch, seq, heads):
    n = z.shape[0]
    assert seq % (HG_SUB * HG_UNROLL) == 0
    consts = _hgrn_consts()
    sec = lambda k: pl.BlockSpec((seq, HEAD), lambda b, h: (b, k * heads + h))
    whole = lambda a: pl.BlockSpec(a.shape, lambda b, h: (0,) * a.ndim)
    return pl.pallas_call(
        functools.partial(_hgrn_body, layer=layer),
        grid=(batch, heads),
        in_specs=[sec(0), sec(0), sec(1), sec(2),
                  pl.BlockSpec((lbp.shape[0], HEAD), lambda b, h: (0, h)),
                  pl.BlockSpec((1, HEAD), lambda b, h: (0, 0))] + [whole(a) for a in consts],
        out_specs=pl.BlockSpec((seq, HEAD), lambda b, h: (b, h)),
        out_shape=jax.ShapeDtypeStruct((n, heads * HEAD), BF16),
        scratch_shapes=[pltpu.VMEM((HEAD, HEAD), F32)],
        compiler_params=_params(("parallel", "parallel")),
        name="hgrn",
    )(qvg, z, qvg, qvg, lbp, gain, *consts)


def _attn_body(lam_ref, sub_ref, q_ref, k_ref, v_ref, o_ref, acc_ref, m_ref, l_ref, s0_ref, s1_ref, *, tq, tk,
               lambda_init):
    nrep = tk // LANES
    lam = lam_ref[...]
    lam_full = (jnp.exp(jnp.sum(lam[0:1] * lam[1:2], axis=-1, keepdims=True))
                - jnp.exp(jnp.sum(lam[2:3] * lam[3:4], axis=-1, keepdims=True)) + lambda_init)
    bufs = (s0_ref, s1_ref)

    def scores(rows, j, s_ref):
        kt = k_ref[j * tk:(j + 1) * tk, :]
        q = q_ref[rows, :]
        for c in range(2):
            cs = slice(c * HEAD, (c + 1) * HEAD)
            s_ref[c] = lax.dot_general(q[:, cs], kt[:, cs], (((1,), (1,)), ((), ())), preferred_element_type=F32)

    def consume(j, s_ref, first, masked):
        vt = v_ref[j * tk:(j + 1) * tk, :]
        ps, alphas = [], []
        for c in range(2):
            s = s_ref[c]
            if masked:
                ri = lax.broadcasted_iota(jnp.int32, s.shape, 0)
                ci = lax.broadcasted_iota(jnp.int32, s.shape, 1)
                s = jnp.where(ci <= ri, s, NEG)
            m_tile = jnp.max(s, axis=-1, keepdims=True)
            if first:
                m_new = jnp.broadcast_to(m_tile, (tq, LANES))
            else:
                m_old = m_ref[c]
                m_new = jnp.maximum(m_old, m_tile)
                alphas.append(jnp.exp2(m_old - m_new))
            p = jnp.exp2(s - jnp.concatenate([m_new] * nrep, axis=1))
            l_tile = jnp.sum(p, axis=-1, keepdims=True)
            l_ref[c] = jnp.broadcast_to(l_tile, (tq, LANES)) if first else alphas[c] * l_ref[c] + l_tile
            m_ref[c] = m_new
            ps.append(p.astype(BF16))
        pv = jnp.dot(jnp.concatenate(ps, axis=0), vt, preferred_element_type=F32)
        for c in range(2):
            part = pv[c * tq:(c + 1) * tq]
            acc_ref[c] = part if first else jnp.concatenate([alphas[c]] * 2, axis=1) * acc_ref[c] + part

    for qi in range(q_ref.shape[0] // tq):
        rows = slice(qi * tq, (qi + 1) * tq)
        scores(rows, 0, bufs[0])
        for j in range(qi + 1):
            if j < qi:
                scores(rows, j + 1, bufs[(j + 1) % 2])
            consume(j, bufs[j % 2], first=(j == 0), masked=(j == qi))
        inv0 = jnp.concatenate([1.0 / l_ref[0]] * 2, axis=1)
        inv1 = jnp.concatenate([1.0 / l_ref[1]] * 2, axis=1)
        o = acc_ref[0] * inv0 - lam_full * (acc_ref[1] * inv1)
        o_ref[rows, :] = (_rms(o, sub_ref[...]) * (1.0 - lambda_init)).astype(o_ref.dtype)


def _attention(q, kv, lam, subln, *, batch, seq, heads, tq, lambda_init):
    n = q.shape[0]
    vd = 2 * HEAD
    assert seq % tq == 0
    return pl.pallas_call(
        functools.partial(_attn_body, tq=tq, tk=tq, lambda_init=lambda_init),
        grid=(batch, heads),
        in_specs=[pl.BlockSpec((4, HEAD), lambda b, h: (0, 0)),
                  pl.BlockSpec((1, vd), lambda b, h: (0, 0)),
                  pl.BlockSpec((seq, vd), lambda b, h: (b, h)),
                  pl.BlockSpec((seq, vd), lambda b, h: (b, h)),
                  pl.BlockSpec((seq, vd), lambda b, h: (b, heads + h))],
        out_specs=pl.BlockSpec((seq, vd), lambda b, h: (b, h)),
        out_shape=jax.ShapeDtypeStruct((n, heads * vd), BF16),
        scratch_shapes=[pltpu.VMEM((2, tq, vd), F32), pltpu.VMEM((2, tq, LANES), F32),
                        pltpu.VMEM((2, tq, LANES), F32), pltpu.VMEM((2, tq, tq), F32), pltpu.VMEM((2, tq, tq), F32)],
        compiler_params=_params(("parallel", "parallel")),
        name="diff_attn",
    )(lam, subln, q, kv, kv)


def kernel(x, p, ffn_norm, ffn_w_gate_up, ffn_w_down, mix_norm, hgrn_w_in, hgrn_lower_bounds, hgrn_out_norm, hgrn_w_out, kv_norm, w_kv, diff_w_q, diff_lambda, diff_subln, diff_w_out, ple_norm, ple_w_gate, ple_w_proj, final_norm):
    batch, seq, d = x.shape
    depth = p.shape[0]
    n_a = hgrn_w_in.shape[0]
    n = batch * seq
    hg_heads = d // HEAD
    da_heads = d // (2 * HEAD)
    k_cols = da_heads * 2 * HEAD

    tm = min(1024, seq)
    tq = min(512, seq)
    wide = hg_heads * HEAD

    wgu = ffn_w_gate_up.astype(BF16)
    wdn = ffn_w_down.astype(BF16)
    w_in = hgrn_w_in.astype(BF16)
    w_ho = hgrn_w_out.astype(BF16)
    w_kvb = w_kv.astype(BF16)
    w_q = diff_w_q.astype(BF16)
    w_do = diff_w_out.astype(BF16)
    w_pg = ple_w_gate.astype(BF16)
    w_pp = ple_w_proj.astype(BF16)
    p2 = p.reshape(depth, n, p.shape[-1])

    rope = _rope_tables(seq)
    h = x.reshape(n, d)
    kv = None
    for i in range(depth):
        h = _ffn(h, ffn_norm[i, 0].reshape(1, d), wgu, wdn, i, 0, tm=tm, tf=512)
        if i < n_a:
            gain = mix_norm[i].reshape(1, d)
            z = _norm_mm(h, gain, w_in, (i,), tm=tm, tn=wide, out_dtype=F32, ntiles=1, src_tile=lambda j: j + 1)
            per = wide // 1024
            qvg = _norm_mm(h, gain, w_in, (i,), tm=tm, tn=1024, out_dtype=BF16, ntiles=3 * per,
                           src_tile=lambda j: j + jnp.where(j >= per, per, 0))
            og = _hgrn(qvg, z, hgrn_lower_bounds, hgrn_out_norm[i].reshape(1, HEAD), i,
                       batch=batch, seq=seq, heads=hg_heads)
            h = _mm_resid(og, w_ho, (i,), h, tm=tm, tn=d)
        else:
            j = i - n_a
            lambda_init = 0.8 - 0.6 * math.exp(-0.3 * i)
            q = _norm_mm(h, mix_norm[i].reshape(1, d), w_q, (j,), tm=tm, tn=k_cols, out_dtype=BF16,
                         rope=rope, seq=seq, rope_cols=k_cols, scale=HEAD ** -0.5 * math.log2(math.e))
            oa = _attention(q, kv, diff_lambda[j], diff_subln[j].reshape(1, 2 * HEAD),
                            batch=batch, seq=seq, heads=da_heads, tq=tq, lambda_init=lambda_init)
            h = _mm_resid(oa, w_do, (j,), h, tm=tm, tn=d)
        h = _ffn(h, ffn_norm[i, 1].reshape(1, d), wgu, wdn, i, 1, tm=tm, tf=512)
        last = i == depth - 1
        h = _ple(h, ple_norm[i].reshape(1, d), w_pg, p2, w_pp, i, final_norm.reshape(1, d),
                 tm=min(512, seq), tn=512, final=last)
        if i == n_a - 1:
            kv = _norm_mm_rows(h, kv_norm.reshape(1, d), w_kvb, (), rope, tm=min(512, seq), tn=512, seq=seq,
                               rope_cols=k_cols)
    return h.reshape(batch, seq, d)
```

```python
import functools
import math

import jax
import jax.numpy as jnp
import numpy as np
from jax import lax
from jax.experimental import pallas as pl
from jax.experimental.pallas import tpu as pltpu

F32 = jnp.float32
BF16 = jnp.bfloat16

NORM_EPS = 1e-6
ROPE_THETA = 10000.0
HEAD = 128
LANES = 128
SUBLANES = 8
HG_CHUNK = 32
HG_SUB = 256
HG_UNROLL = 8
NEG = -1e30

VMEM_LIMIT = 60 * 1024 * 1024


def _params(sem):
    return pltpu.CompilerParams(dimension_semantics=sem, vmem_limit_bytes=VMEM_LIMIT)


def _rms(x, g):
    ms = jnp.mean(x * x, axis=-1, keepdims=True)
    return x * lax.rsqrt(ms + NORM_EPS) * g


def _sigmoid(x):
    return 1.0 / (1.0 + jnp.exp(-x))


def _ffn_body(x_ref, g_ref, wg_ref, wu_ref, wd_ref, wdl_ref, o_ref, xn_ref, act_ref):
    f = pl.program_id(1)
    last = pl.num_programs(1) - 1

    def gate_up():
        xn = xn_ref[...]
        gate = jnp.dot(xn, wg_ref[...], preferred_element_type=F32)
        up = jnp.dot(xn, wu_ref[...], preferred_element_type=F32)
        return (gate * _sigmoid(gate) * up).astype(BF16)

    @pl.when(f == 0)
    def _():
        x = x_ref[...]
        xn_ref[...] = _rms(x, g_ref[...]).astype(BF16)
        o_ref[...] = x
        act_ref[...] = gate_up()

    @pl.when(f > 0)
    def _():
        part = jnp.dot(act_ref[...], wd_ref[...], preferred_element_type=F32)
        act = gate_up()
        o_ref[...] += 0.5 * part
        act_ref[...] = act

    @pl.when(f == last)
    def _():
        o_ref[...] += 0.5 * jnp.dot(act_ref[...], wdl_ref[...], preferred_element_type=F32)


def _ffn(x, gain, wgu, wd, li, lj, *, tm, tf):
    n, d = x.shape
    ff = wd.shape[2]
    nf = ff // tf
    assert n % tm == 0 and ff % tf == 0 and nf >= 2
    return pl.pallas_call(
        _ffn_body,
        grid=(n // tm, nf),
        in_specs=[
            pl.BlockSpec((tm, d), lambda i, f: (i, 0)),
            pl.BlockSpec((1, d), lambda i, f: (0, 0)),
            pl.BlockSpec((None, None, d, tf), lambda i, f: (li, lj, 0, f)),
            pl.BlockSpec((None, None, d, tf), lambda i, f: (li, lj, 0, f + nf)),
            pl.BlockSpec((None, None, tf, d), lambda i, f: (li, lj, jnp.maximum(f - 1, 0), 0)),
            pl.BlockSpec((None, None, tf, d), lambda i, f: (li, lj, nf - 1, 0), pipeline_mode=pl.Buffered(1)),
        ],
        out_specs=pl.BlockSpec((tm, d), lambda i, f: (i, 0)),
        out_shape=jax.ShapeDtypeStruct((n, d), F32),
        scratch_shapes=[pltpu.VMEM((tm, d), BF16), pltpu.VMEM((tm, tf), BF16)],
        compiler_params=_params(("parallel", "arbitrary")),
        name="ffn",
    )(x, gain, wgu, wgu, wd, wd)


def _rope_store(y, cos, sin, o_ref, scale):
    for g in range(y.shape[1] // HEAD):
        yg = y[:, g * HEAD:(g + 1) * HEAD]
        r = yg * cos + pltpu.roll(yg, HEAD // 2, axis=1) * sin
        if scale != 1.0:
            r = r * scale
        o_ref[:, g * HEAD:(g + 1) * HEAD] = r.astype(o_ref.dtype)


def _norm_mm_body(x_ref, g_ref, w_ref, o_ref, xn_ref):
    @pl.when(pl.program_id(1) == 0)
    def _():
        xn_ref[...] = _rms(x_ref[...], g_ref[...]).astype(BF16)

    o_ref[...] = jnp.dot(xn_ref[...], w_ref[...], preferred_element_type=F32).astype(o_ref.dtype)


def _norm_mm_rope_body(x_ref, g_ref, w_ref, cos_ref, sin_ref, o_ref, xn_ref, *, rope_tiles, scale):
    j = pl.program_id(1)

    @pl.when(j == 0)
    def _():
        xn_ref[...] = _rms(x_ref[...], g_ref[...]).astype(BF16)

    y = jnp.dot(xn_ref[...], w_ref[...], preferred_element_type=F32)

    @pl.when(j < rope_tiles)
    def _():
        _rope_store(y, cos_ref[...], sin_ref[...], o_ref, scale)

    @pl.when(j >= rope_tiles)
    def _():
        o_ref[...] = y.astype(o_ref.dtype)


def _norm_mm(x, gain, w, w_index, *, tm, tn, out_dtype, rope=None, seq=None, rope_cols=0, scale=1.0,
             ntiles=None, src_tile=lambda j: j):
    n, d = x.shape
    nout = (w.shape[-1] // tn if ntiles is None else ntiles) * tn
    assert n % tm == 0 and w.shape[-1] % tn == 0
    nlead = w.ndim - 2
    w_mode = dict(pipeline_mode=pl.Buffered(1)) if nout == tn else {}
    w_spec = pl.BlockSpec((None,) * nlead + (d, tn), lambda i, j: tuple(w_index) + (0, src_tile(j)), **w_mode)
    in_specs = [pl.BlockSpec((tm, d), lambda i, j: (i, 0)),
                pl.BlockSpec((1, d), lambda i, j: (0, 0)),
                w_spec]
    args = [x, gain, w]
    if rope is None:
        body = _norm_mm_body
    else:
        assert seq % tm == 0 and rope_cols % tn == 0
        nseq = seq // tm
        tab_spec = pl.BlockSpec((tm, HEAD), lambda i, j: (i % nseq, 0))
        in_specs += [tab_spec, tab_spec]
        args += list(rope)
        body = functools.partial(_norm_mm_rope_body, rope_tiles=rope_cols // tn, scale=scale)
    return pl.pallas_call(
        body,
        grid=(n // tm, nout // tn),
        in_specs=in_specs,
        out_specs=pl.BlockSpec((tm, tn), lambda i, j: (i, j)),
        out_shape=jax.ShapeDtypeStruct((n, nout), out_dtype),
        scratch_shapes=[pltpu.VMEM((tm, d), BF16)],
        compiler_params=_params(("parallel", "arbitrary")),
        name="norm_mm",
    )(*args)


def _norm_mm_rows_body(*refs, nw, tn, rope_tiles, scale):
    x_ref, g_ref = refs[:2]
    w_refs = refs[2:2 + nw]
    o_ref = refs[-1]
    xn = _rms(x_ref[...], g_ref[...]).astype(BF16)
    if rope_tiles:
        cos, sin = refs[2 + nw][...], refs[3 + nw][...]
    t = 0
    for w_ref in w_refs:
        for c in range(w_ref.shape[1] // tn):
            y = jnp.dot(xn, w_ref[:, c * tn:(c + 1) * tn], preferred_element_type=F32)
            cols = slice(t * tn, (t + 1) * tn)
            if t < rope_tiles:
                _rope_store(y, cos, sin, o_ref.at[:, cols], scale)
            else:
                o_ref[:, cols] = y.astype(o_ref.dtype)
            t += 1


def _norm_mm_rows(x, gain, w, w_index, col_blocks, *, tm, tn, rope=None, seq=None, rope_cols=0, scale=1.0):
    n, d = x.shape
    nout = sum(width for _, width in col_blocks)
    assert n % tm == 0 and all(width % tn == 0 for _, width in col_blocks) and rope_cols % tn == 0
    nlead = w.ndim - 2
    in_specs = [pl.BlockSpec((tm, d), lambda i: (i, 0)), pl.BlockSpec((1, d), lambda i: (0, 0))]
    in_specs += [pl.BlockSpec((None,) * nlead + (d, width), lambda i, blk=blk: tuple(w_index) + (0, blk),
                              pipeline_mode=pl.Buffered(1)) for blk, width in col_blocks]
    args = [x, gain] + [w] * len(col_blocks)
    if rope is not None:
        assert seq % tm == 0
        nseq = seq // tm
        in_specs += [pl.BlockSpec((tm, HEAD), lambda i: (i % nseq, 0))] * 2
        args += list(rope)
    return pl.pallas_call(
        functools.partial(_norm_mm_rows_body, nw=len(col_blocks), tn=tn, rope_tiles=rope_cols // tn, scale=scale),
        grid=(n // tm,),
        in_specs=in_specs,
        out_specs=pl.BlockSpec((tm, nout), lambda i: (i, 0)),
        out_shape=jax.ShapeDtypeStruct((n, nout), BF16),
        compiler_params=_params(("parallel",)),
        name="norm_mm_rows",
    )(*args)


def _mm_resid_body(x_ref, w_ref, r_ref, o_ref):
    o_ref[...] = r_ref[...] + jnp.dot(x_ref[...], w_ref[...], preferred_element_type=F32)


def _mm_resid(x, w, w_index, resid, *, tm, tn):
    n, k = x.shape
    nout = w.shape[-1]
    assert n % tm == 0 and nout % tn == 0
    nlead = w.ndim - 2
    w_mode = dict(pipeline_mode=pl.Buffered(1)) if nout == tn else {}
    return pl.pallas_call(
        _mm_resid_body,
        grid=(n // tm, nout // tn),
        in_specs=[pl.BlockSpec((tm, k), lambda i, j: (i, 0)),
                  pl.BlockSpec((None,) * nlead + (k, tn), lambda i, j: tuple(w_index) + (0, j), **w_mode),
                  pl.BlockSpec((tm, tn), lambda i, j: (i, j))],
        out_specs=pl.BlockSpec((tm, tn), lambda i, j: (i, j)),
        out_shape=jax.ShapeDtypeStruct((n, nout), F32),
        compiler_params=_params(("parallel", "arbitrary")),
        name="mm_resid",
    )(x, w, resid)


def _ple_body(x_ref, g_ref, wg_ref, p_ref, wp_ref, fg_ref, o_ref, *, tn, final):
    x = x_ref[...]
    xn = _rms(x, g_ref[...]).astype(BF16)
    pb = p_ref[...].astype(BF16)
    d = x.shape[1]
    for c in range(d // tn):
        sl = slice(c * tn, (c + 1) * tn)
        gate = jnp.dot(xn, wg_ref[:, sl], preferred_element_type=F32)
        proj = jnp.dot(pb, wp_ref[:, sl], preferred_element_type=F32)
        o_ref[:, sl] = x_ref[:, sl] + _sigmoid(gate) * proj
    if final:
        o_ref[...] = _rms(o_ref[...], fg_ref[...])


def _ple(x, gain, wg, p, wp, layer, final_gain, *, tm, tn, final):
    n, d = x.shape
    pd = p.shape[-1]
    rows = n // tm
    assert n % tm == 0 and d % tn == 0
    return pl.pallas_call(
        functools.partial(_ple_body, tn=tn, final=final),
        grid=(rows,),
        in_specs=[pl.BlockSpec((tm, d), lambda i: (i, 0)),
                  pl.BlockSpec((1, d), lambda i: (0, 0)),
                  pl.BlockSpec((None, d, d), lambda i: (layer, 0, 0), pipeline_mode=pl.Buffered(1)),
                  pl.BlockSpec((None, tm, pd), lambda i: (layer, i, 0)),
                  pl.BlockSpec((None, pd, d), lambda i: (layer, 0, 0), pipeline_mode=pl.Buffered(1)),
                  pl.BlockSpec((1, d), lambda i: (0, 0))],
        out_specs=pl.BlockSpec((tm, d), lambda i: (i, 0)),
        out_shape=jax.ShapeDtypeStruct((n, d), F32),
        compiler_params=_params(("parallel",)),
        name="ple",
    )(x, gain, wg, p, wp, final_gain)


def _rope_table_body(cos_ref, sin_ref):
    shape = cos_ref.shape
    pos = lax.broadcasted_iota(jnp.int32, shape, 0).astype(F32)
    lane = lax.broadcasted_iota(jnp.int32, shape, 1)
    half = HEAD // 2
    k = jnp.bitwise_and(lane, half - 1).astype(F32)
    inv_freq = jnp.exp(k * (-math.log(ROPE_THETA) / half))
    ang = pos * inv_freq
    cos_ref[...] = jnp.cos(ang)
    s = jnp.sin(ang)
    sin_ref[...] = jnp.where(lane < half, -s, s)


def _rope_tables(seq):
    return pl.pallas_call(
        _rope_table_body,
        out_shape=(jax.ShapeDtypeStruct((seq, HEAD), F32), jax.ShapeDtypeStruct((seq, HEAD), F32)),
        name="rope_tables",
    )()


def _split3(x):
    hi = x.astype(BF16)
    r1 = x - hi.astype(F32)
    mid = r1.astype(BF16)
    lo = (r1 - mid.astype(F32)).astype(BF16)
    return hi, mid, lo


def _hgrn_consts():
    r = np.arange(HG_SUB)
    chunk, blk = r // HG_CHUNK, r // SUBLANES
    nch = HG_SUB // HG_CHUNK
    same_chunk = chunk[:, None] == chunk[None, :]
    tril = same_chunk & (r[None, :] <= r[:, None])
    off = same_chunk & (blk[:, None] > blk[None, :])
    diag = (blk[:, None] == blk[None, :]) & (r[None, :] <= r[:, None])
    lane_blk = np.arange(SUBLANES * HEAD) // HEAD
    spread = lane_blk[:, None] == (r % SUBLANES)[None, :]
    bpc = HG_CHUNK // SUBLANES
    kmask = np.stack([np.broadcast_to((((r % HG_CHUNK) // SUBLANES) == jb)[:, None], (HG_SUB, HEAD))
                      for jb in range(bpc - 1)])
    cmask = np.stack([np.broadcast_to((chunk == c)[None, :], (HEAD, HG_SUB)) for c in range(nch)])
    eye = np.eye(HEAD)
    return (jnp.asarray(tril, BF16), jnp.asarray(off, F32), jnp.asarray(diag, F32),
            jnp.asarray(spread, BF16), jnp.asarray(kmask, F32), jnp.asarray(cmask, F32), jnp.asarray(eye, BF16))


def _hgrn_front(q, z, v, lb, tril, kmask):
    rows = q.shape[0]
    nch = rows // HG_CHUNK
    nblk = rows // SUBLANES
    bpc = HG_CHUNK // SUBLANES

    log2e = math.log2(math.e)
    soft = jnp.log2(1.0 + jnp.exp2(jnp.abs(z) * -log2e))
    lg_lb = jnp.log2(lb)
    lg_1mlb = jnp.log2(1.0 - lb)
    lg_sig = lg_1mlb + (jnp.minimum(z, 0.0) * log2e - soft)
    lg_k = lg_1mlb - (jnp.maximum(z, 0.0) * log2e + soft)
    logf = jnp.maximum(lg_lb, lg_sig) + jnp.log2(1.0 + jnp.exp2(-jnp.abs(lg_lb - lg_sig)))
    qs = q * (HEAD ** -0.5)

    hi, mid, lo = _split3(logf)
    parts = jnp.dot(tril, jnp.concatenate([hi, mid, lo], axis=1), preferred_element_type=F32)
    b = parts[:, :HEAD] + parts[:, HEAD:2 * HEAD] + parts[:, 2 * HEAD:]

    vb = v.astype(BF16)
    c = b - lg_k
    b3 = b.reshape(nch, HG_CHUNK, HEAD)
    b_last = b3[:, HG_CHUNK - 1:HG_CHUNK, :]
    q_in = (qs * jnp.exp2(b)).astype(BF16)
    k_end = jnp.exp2(b_last - c.reshape(nch, HG_CHUNK, HEAD)).reshape(rows, HEAD).astype(BF16)
    decay_cols = jnp.exp2(b_last).reshape(nch, HEAD).T

    qs3 = qs.reshape(nch, HG_CHUNK, HEAD)
    b8 = b.reshape(nblk, SUBLANES, HEAD)
    c8 = c.reshape(nblk, SUBLANES, HEAD)
    k_blk = jnp.exp2(b8[:, SUBLANES - 1:SUBLANES, :] - c8).reshape(rows, HEAD)
    q_groups, k_groups = [], []
    for jb in range(bpc - 1):
        ref = b3[:, jb * SUBLANES + SUBLANES - 1:jb * SUBLANES + SUBLANES, :]
        qg = qs3 * jnp.exp2(jnp.minimum(b3 - ref, 0.0))
        q_groups.append(qg.reshape(rows, HEAD).astype(BF16))
        k_groups.append((k_blk * kmask[jb]).astype(BF16))
    q_cat = jnp.concatenate(q_groups, axis=1)
    k_cat = jnp.concatenate(k_groups, axis=1)

    q8 = qs.reshape(nblk, SUBLANES, HEAD)
    prods = []
    for i in range(SUBLANES):
        e = jnp.exp2(jnp.minimum(b8 - c8[:, i:i + 1, :], 0.0))
        prods.append((e * q8).reshape(rows, HEAD).astype(BF16))
    p_cat = jnp.concatenate(prods, axis=1)
    return q_cat, k_cat, p_cat, q_in, k_end, decay_cols, vb


def _hgrn_back(front, state, off, diag, spread, cmask, eye):
    q_cat, k_cat, p_cat, q_in, k_end, decay_cols, vb = front
    nch = decay_cols.shape[1]
    a = lax.dot_general(q_cat, k_cat, (((1,), (1,)), ((), ())), preferred_element_type=F32) * off
    a = a + jnp.dot(p_cat, spread, preferred_element_type=F32) * diag
    o = jnp.dot(a.astype(BF16), vb, preferred_element_type=F32)

    k_end_t = lax.dot_general(eye, k_end, (((1,), (1,)), ((), ())), preferred_element_type=F32)
    k_stack = jnp.concatenate([(k_end_t * cmask[c]).astype(BF16) for c in range(nch)], axis=0)
    upd = jnp.dot(k_stack, vb, preferred_element_type=F32)

    outs = []
    for c in range(nch):
        outs.append(jnp.dot(q_in[c * HG_CHUNK:(c + 1) * HG_CHUNK], state.astype(BF16), preferred_element_type=F32))
        state = state * decay_cols[:, c:c + 1] + upd[c * HEAD:(c + 1) * HEAD]
    o = o + jnp.concatenate(outs, axis=0)
    return o, state


def _hgrn_body(q_ref, z_ref, v_ref, g_ref, lbp_ref, gain_ref, tril_ref, off_ref, diag_ref, spread_ref, kmask_ref,
               cmask_ref, eye_ref, o_ref, state_ref, *, layer):
    lbp = lbp_ref[...]
    e = jnp.exp(lbp - jnp.max(lbp, axis=0, keepdims=True))
    lb = jnp.sum(e[:layer + 1], axis=0, keepdims=True) / jnp.sum(e, axis=0, keepdims=True)

    state_ref[...] = jnp.zeros_like(state_ref)
    consts = (tril_ref[...], off_ref[...], diag_ref[...], spread_ref[...], kmask_ref[...], cmask_ref[...],
              eye_ref[...])

    def slabs(s, carry):
        state = state_ref[...]
        sls = [pl.ds(pl.multiple_of((s * HG_UNROLL + u) * HG_SUB, HG_SUB), HG_SUB) for u in range(HG_UNROLL)]
        fronts = [_hgrn_front(q_ref[sl, :].astype(F32), z_ref[sl, :], v_ref[sl, :].astype(F32), lb, consts[0],
                              consts[4]) for sl in sls]
        for sl, front in zip(sls, fronts):
            o, state = _hgrn_back(front, state, consts[1], consts[2], consts[3], consts[5], consts[6])
            on = _rms(o, gain_ref[...])
            g = g_ref[sl, :].astype(F32)
            o_ref[sl, :] = (on * (g * _sigmoid(g))).astype(o_ref.dtype)
        state_ref[...] = state
        return carry

    lax.fori_loop(0, q_ref.shape[0] // (HG_SUB * HG_UNROLL), slabs, 0)


def _hgrn(qvg, z, lbp, gain, layer, *, batch, seq, heads):
    n = z.shape[0]
    assert seq % (HG_SUB * HG_UNROLL) == 0
    consts = _hgrn_consts()
    sec = lambda k: pl.BlockSpec((seq, HEAD), lambda b, h: (b, k * heads + h))
    whole = lambda a: pl.BlockSpec(a.shape, lambda b, h: (0,) * a.ndim)
    return pl.pallas_call(
        functools.partial(_hgrn_body, layer=layer),
        grid=(batch, heads),
        in_specs=[sec(0), sec(0), sec(1), sec(2),
                  pl.BlockSpec((lbp.shape[0], HEAD), lambda b, h: (0, h)),
                  pl.BlockSpec((1, HEAD), lambda b, h: (0, 0))] + [whole(a) for a in consts],
        out_specs=pl.BlockSpec((seq, HEAD), lambda b, h: (b, h)),
        out_shape=jax.ShapeDtypeStruct((n, heads * HEAD), BF16),
        scratch_shapes=[pltpu.VMEM((HEAD, HEAD), F32)],
        compiler_params=_params(("parallel", "parallel")),
        name="hgrn",
    )(qvg, z, qvg, qvg, lbp, gain, *consts)


def _attn_body(lam_ref, sub_ref, q_ref, k_ref, v_ref, o_ref, acc_ref, m_ref, l_ref, s0_ref, s1_ref, *, tq, tk,
               lambda_init):
    nrep = tk // LANES
    lam = lam_ref[...]
    lam_full = (jnp.exp(jnp.sum(lam[0:1] * lam[1:2], axis=-1, keepdims=True))
                - jnp.exp(jnp.sum(lam[2:3] * lam[3:4], axis=-1, keepdims=True)) + lambda_init)
    bufs = (s0_ref, s1_ref)

    def scores(rows, j, s_ref):
        kt = k_ref[j * tk:(j + 1) * tk, :]
        q = q_ref[rows, :]
        for c in range(2):
            cs = slice(c * HEAD, (c + 1) * HEAD)
            s_ref[c] = lax.dot_general(q[:, cs], kt[:, cs], (((1,), (1,)), ((), ())), preferred_element_type=F32)

    def consume(j, s_ref, first, masked):
        vt = v_ref[j * tk:(j + 1) * tk, :]
        ps, alphas = [], []
        for c in range(2):
            s = s_ref[c]
            if masked:
                ri = lax.broadcasted_iota(jnp.int32, s.shape, 0)
                ci = lax.broadcasted_iota(jnp.int32, s.shape, 1)
                s = jnp.where(ci <= ri, s, NEG)
            m_tile = jnp.max(s, axis=-1, keepdims=True)
            if first:
                m_new = jnp.broadcast_to(m_tile, (tq, LANES))
            else:
                m_old = m_ref[c]
                m_new = jnp.maximum(m_old, m_tile)
                alphas.append(jnp.exp2(m_old - m_new))
            p = jnp.exp2(s - jnp.concatenate([m_new] * nrep, axis=1))
            l_tile = jnp.sum(p, axis=-1, keepdims=True)
            l_ref[c] = jnp.broadcast_to(l_tile, (tq, LANES)) if first else alphas[c] * l_ref[c] + l_tile
            m_ref[c] = m_new
            ps.append(p.astype(BF16))
        pv = jnp.dot(jnp.concatenate(ps, axis=0), vt, preferred_element_type=F32)
        for c in range(2):
            part = pv[c * tq:(c + 1) * tq]
            acc_ref[c] = part if first else jnp.concatenate([alphas[c]] * 2, axis=1) * acc_ref[c] + part

    for qi in range(q_ref.shape[0] // tq):
        rows = slice(qi * tq, (qi + 1) * tq)
        scores(rows, 0, bufs[0])
        for j in range(qi + 1):
            if j < qi:
                scores(rows, j + 1, bufs[(j + 1) % 2])
            consume(j, bufs[j % 2], first=(j == 0), masked=(j == qi))
        inv0 = jnp.concatenate([1.0 / l_ref[0]] * 2, axis=1)
        inv1 = jnp.concatenate([1.0 / l_ref[1]] * 2, axis=1)
        o = acc_ref[0] * inv0 - lam_full * (acc_ref[1] * inv1)
        o_ref[rows, :] = (_rms(o, sub_ref[...]) * (1.0 - lambda_init)).astype(o_ref.dtype)


def _attention(q, kv, lam, subln, *, batch, seq, heads, tq, lambda_init):
    n = q.shape[0]
    vd = 2 * HEAD
    assert seq % tq == 0
    return pl.pallas_call(
        functools.partial(_attn_body, tq=tq, tk=tq, lambda_init=lambda_init),
        grid=(batch, heads),
        in_specs=[pl.BlockSpec((4, HEAD), lambda b, h: (0, 0)),
                  pl.BlockSpec((1, vd), lambda b, h: (0, 0)),
                  pl.BlockSpec((seq, vd), lambda b, h: (b, h)),
                  pl.BlockSpec((seq, vd), lambda b, h: (b, h)),
                  pl.BlockSpec((seq, vd), lambda b, h: (b, heads + h))],
        out_specs=pl.BlockSpec((seq, vd), lambda b, h: (b, h)),
        out_shape=jax.ShapeDtypeStruct((n, heads * vd), BF16),
        scratch_shapes=[pltpu.VMEM((2, tq, vd), F32), pltpu.VMEM((2, tq, LANES), F32),
                        pltpu.VMEM((2, tq, LANES), F32), pltpu.VMEM((2, tq, tq), F32), pltpu.VMEM((2, tq, tq), F32)],
        compiler_params=_params(("parallel", "parallel")),
        name="diff_attn",
    )(lam, subln, q, kv, kv)


def kernel(x, p, ffn_norm, ffn_w_gate_up, ffn_w_down, mix_norm, hgrn_w_in, hgrn_lower_bounds, hgrn_out_norm, hgrn_w_out, kv_norm, w_kv, diff_w_q, diff_lambda, diff_subln, diff_w_out, ple_norm, ple_w_gate, ple_w_proj, final_norm):
    batch, seq, d = x.shape
    depth = p.shape[0]
    n_a = hgrn_w_in.shape[0]
    n = batch * seq
    hg_heads = d // HEAD
    da_heads = d // (2 * HEAD)
    k_cols = da_heads * 2 * HEAD

    tm = min(1024, seq)
    tq = min(512, seq)
    wide = hg_heads * HEAD

    wgu = ffn_w_gate_up.astype(BF16)
    wdn = ffn_w_down.astype(BF16)
    w_in = hgrn_w_in.astype(BF16)
    w_ho = hgrn_w_out.astype(BF16)
    w_kvb = w_kv.astype(BF16)
    w_q = diff_w_q.astype(BF16)
    w_do = diff_w_out.astype(BF16)
    w_pg = ple_w_gate.astype(BF16)
    w_pp = ple_w_proj.astype(BF16)
    p2 = p.reshape(depth, n, p.shape[-1])

    rope = _rope_tables(seq)
    h = x.reshape(n, d)
    kv = None
    for i in range(depth):
        h = _ffn(h, ffn_norm[i, 0].reshape(1, d), wgu, wdn, i, 0, tm=tm, tf=512)
        if i < n_a:
            gain = mix_norm[i].reshape(1, d)
            z = _norm_mm(h, gain, w_in, (i,), tm=tm, tn=wide, out_dtype=F32, ntiles=1, src_tile=lambda j: j + 1)
            qvg = _norm_mm_rows(h, gain, w_in, (i,), ((0, wide), (1, 2 * wide)), tm=min(512, seq), tn=1024)
            og = _hgrn(qvg, z, hgrn_lower_bounds, hgrn_out_norm[i].reshape(1, HEAD), i,
                       batch=batch, seq=seq, heads=hg_heads)
            h = _mm_resid(og, w_ho, (i,), h, tm=tm, tn=d)
        else:
            j = i - n_a
            lambda_init = 0.8 - 0.6 * math.exp(-0.3 * i)
            q = _norm_mm(h, mix_norm[i].reshape(1, d), w_q, (j,), tm=tm, tn=k_cols, out_dtype=BF16,
                         rope=rope, seq=seq, rope_cols=k_cols, scale=HEAD ** -0.5 * math.log2(math.e))
            oa = _attention(q, kv, diff_lambda[j], diff_subln[j].reshape(1, 2 * HEAD),
                            batch=batch, seq=seq, heads=da_heads, tq=tq, lambda_init=lambda_init)
            h = _mm_resid(oa, w_do, (j,), h, tm=tm, tn=d)
        h = _ffn(h, ffn_norm[i, 1].reshape(1, d), wgu, wdn, i, 1, tm=tm, tf=512)
        last = i == depth - 1
        h = _ple(h, ple_norm[i].reshape(1, d), w_pg, p2, w_pp, i, final_norm.reshape(1, d),
                 tm=min(512, seq), tn=512, final=last)
        if i == n_a - 1:
            kv = _norm_mm_rows(h, kv_norm.reshape(1, d), w_kvb, (), ((0, w_kvb.shape[-1]),), tm=min(512, seq), tn=512,
                               rope=rope, seq=seq, rope_cols=k_cols)
    return h.reshape(batch, seq, d)
```

```python
import functools
import math

import jax
import jax.numpy as jnp
import numpy as np
from jax import lax
from jax.experimental import pallas as pl
from jax.experimental.pallas import tpu as pltpu

F32 = jnp.float32
BF16 = jnp.bfloat16

NORM_EPS = 1e-6
ROPE_THETA = 10000.0
HEAD = 128
LANES = 128
SUBLANES = 8
HG_CHUNK = 32
HG_SUB = 256
HG_UNROLL = 8
NEG = -1e30

VMEM_LIMIT = 60 * 1024 * 1024


def _params(sem):
    return pltpu.CompilerParams(dimension_semantics=sem, vmem_limit_bytes=VMEM_LIMIT)


def _rms(x, g):
    ms = jnp.mean(x * x, axis=-1, keepdims=True)
    return x * lax.rsqrt(ms + NORM_EPS) * g


def _sigmoid(x):
    return 1.0 / (1.0 + jnp.exp(-x))


def _ffn_body(x_ref, g_ref, wg_ref, wu_ref, wd_ref, wdl_ref, o_ref, xn_ref, act_ref):
    f = pl.program_id(1)
    last = pl.num_programs(1) - 1

    def gate_up():
        xn = xn_ref[...]
        gate = jnp.dot(xn, wg_ref[...], preferred_element_type=F32)
        up = jnp.dot(xn, wu_ref[...], preferred_element_type=F32)
        return (gate * _sigmoid(gate) * up).astype(BF16)

    @pl.when(f == 0)
    def _():
        x = x_ref[...]
        xn_ref[...] = _rms(x, g_ref[...]).astype(BF16)
        o_ref[...] = x
        act_ref[...] = gate_up()

    @pl.when(f > 0)
    def _():
        part = jnp.dot(act_ref[...], wd_ref[...], preferred_element_type=F32)
        act = gate_up()
        o_ref[...] += 0.5 * part
        act_ref[...] = act

    @pl.when(f == last)
    def _():
        o_ref[...] += 0.5 * jnp.dot(act_ref[...], wdl_ref[...], preferred_element_type=F32)


def _ffn(x, gain, wgu, wd, li, lj, *, tm, tf):
    n, d = x.shape
    ff = wd.shape[2]
    nf = ff // tf
    assert n % tm == 0 and ff % tf == 0 and nf >= 2
    return pl.pallas_call(
        _ffn_body,
        grid=(n // tm, nf),
        in_specs=[
            pl.BlockSpec((tm, d), lambda i, f: (i, 0)),
            pl.BlockSpec((1, d), lambda i, f: (0, 0)),
            pl.BlockSpec((None, None, d, tf), lambda i, f: (li, lj, 0, f)),
            pl.BlockSpec((None, None, d, tf), lambda i, f: (li, lj, 0, f + nf)),
            pl.BlockSpec((None, None, tf, d), lambda i, f: (li, lj, jnp.maximum(f - 1, 0), 0)),
            pl.BlockSpec((None, None, tf, d), lambda i, f: (li, lj, nf - 1, 0), pipeline_mode=pl.Buffered(1)),
        ],
        out_specs=pl.BlockSpec((tm, d), lambda i, f: (i, 0)),
        out_shape=jax.ShapeDtypeStruct((n, d), F32),
        scratch_shapes=[pltpu.VMEM((tm, d), BF16), pltpu.VMEM((tm, tf), BF16)],
        compiler_params=_params(("parallel", "arbitrary")),
        name="ffn",
    )(x, gain, wgu, wgu, wd, wd)


def _rope_store(y, cos, sin, o_ref, scale):
    for g in range(y.shape[1] // HEAD):
        yg = y[:, g * HEAD:(g + 1) * HEAD]
        r = yg * cos + pltpu.roll(yg, HEAD // 2, axis=1) * sin
        if scale != 1.0:
            r = r * scale
        o_ref[:, g * HEAD:(g + 1) * HEAD] = r.astype(o_ref.dtype)


def _norm_mm_body(x_ref, g_ref, w_ref, o_ref, xn_ref):
    @pl.when(pl.program_id(1) == 0)
    def _():
        xn_ref[...] = _rms(x_ref[...], g_ref[...]).astype(BF16)

    o_ref[...] = jnp.dot(xn_ref[...], w_ref[...], preferred_element_type=F32).astype(o_ref.dtype)


def _norm_mm_rope_body(x_ref, g_ref, w_ref, cos_ref, sin_ref, o_ref, xn_ref, *, rope_tiles, scale):
    j = pl.program_id(1)

    @pl.when(j == 0)
    def _():
        xn_ref[...] = _rms(x_ref[...], g_ref[...]).astype(BF16)

    y = jnp.dot(xn_ref[...], w_ref[...], preferred_element_type=F32)

    @pl.when(j < rope_tiles)
    def _():
        _rope_store(y, cos_ref[...], sin_ref[...], o_ref, scale)

    @pl.when(j >= rope_tiles)
    def _():
        o_ref[...] = y.astype(o_ref.dtype)


def _norm_mm(x, gain, w, w_index, *, tm, tn, out_dtype, rope=None, seq=None, rope_cols=0, scale=1.0,
             ntiles=None, src_tile=lambda j: j):
    n, d = x.shape
    nout = (w.shape[-1] // tn if ntiles is None else ntiles) * tn
    assert n % tm == 0 and w.shape[-1] % tn == 0
    nlead = w.ndim - 2
    w_mode = dict(pipeline_mode=pl.Buffered(1)) if nout == tn else {}
    w_spec = pl.BlockSpec((None,) * nlead + (d, tn), lambda i, j: tuple(w_index) + (0, src_tile(j)), **w_mode)
    in_specs = [pl.BlockSpec((tm, d), lambda i, j: (i, 0)),
                pl.BlockSpec((1, d), lambda i, j: (0, 0)),
                w_spec]
    args = [x, gain, w]
    if rope is None:
        body = _norm_mm_body
    else:
        assert seq % tm == 0 and rope_cols % tn == 0
        nseq = seq // tm
        tab_spec = pl.BlockSpec((tm, HEAD), lambda i, j: (i % nseq, 0))
        in_specs += [tab_spec, tab_spec]
        args += list(rope)
        body = functools.partial(_norm_mm_rope_body, rope_tiles=rope_cols // tn, scale=scale)
    return pl.pallas_call(
        body,
        grid=(n // tm, nout // tn),
        in_specs=in_specs,
        out_specs=pl.BlockSpec((tm, tn), lambda i, j: (i, j)),
        out_shape=jax.ShapeDtypeStruct((n, nout), out_dtype),
        scratch_shapes=[pltpu.VMEM((tm, d), BF16)],
        compiler_params=_params(("parallel", "arbitrary")),
        name="norm_mm",
    )(*args)


def _norm_mm_rows_body(*refs, nw, tn, rope_tiles, scale):
    x_ref, g_ref = refs[:2]
    w_refs = refs[2:2 + nw]
    o_ref = refs[-1]
    xn = _rms(x_ref[...], g_ref[...]).astype(BF16)
    if rope_tiles:
        cos, sin = refs[2 + nw][...], refs[3 + nw][...]
    t = 0
    for w_ref in w_refs:
        for c in range(w_ref.shape[1] // tn):
            y = jnp.dot(xn, w_ref[:, c * tn:(c + 1) * tn], preferred_element_type=F32)
            cols = slice(t * tn, (t + 1) * tn)
            if t < rope_tiles:
                _rope_store(y, cos, sin, o_ref.at[:, cols], scale)
            else:
                o_ref[:, cols] = y.astype(o_ref.dtype)
            t += 1


def _norm_mm_rows(x, gain, w, w_index, col_blocks, *, tm, tn, rope=None, seq=None, rope_cols=0, scale=1.0):
    n, d = x.shape
    nout = sum(width for _, width in col_blocks)
    assert n % tm == 0 and all(width % tn == 0 for _, width in col_blocks) and rope_cols % tn == 0
    nlead = w.ndim - 2
    in_specs = [pl.BlockSpec((tm, d), lambda i: (i, 0)), pl.BlockSpec((1, d), lambda i: (0, 0))]
    in_specs += [pl.BlockSpec((None,) * nlead + (d, width), lambda i, blk=blk: tuple(w_index) + (0, blk),
                              pipeline_mode=pl.Buffered(1)) for blk, width in col_blocks]
    args = [x, gain] + [w] * len(col_blocks)
    if rope is not None:
        assert seq % tm == 0
        nseq = seq // tm
        in_specs += [pl.BlockSpec((tm, HEAD), lambda i: (i % nseq, 0))] * 2
        args += list(rope)
    return pl.pallas_call(
        functools.partial(_norm_mm_rows_body, nw=len(col_blocks), tn=tn, rope_tiles=rope_cols // tn, scale=scale),
        grid=(n // tm,),
        in_specs=in_specs,
        out_specs=pl.BlockSpec((tm, nout), lambda i: (i, 0)),
        out_shape=jax.ShapeDtypeStruct((n, nout), BF16),
        compiler_params=_params(("parallel",)),
        name="norm_mm_rows",
    )(*args)


def _mm_resid_body(x_ref, w_ref, r_ref, o_ref):
    o_ref[...] = r_ref[...] + jnp.dot(x_ref[...], w_ref[...], preferred_element_type=F32)


def _mm_resid(x, w, w_index, resid, *, tm, tn):
    n, k = x.shape
    nout = w.shape[-1]
    assert n % tm == 0 and nout % tn == 0
    nlead = w.ndim - 2
    w_mode = dict(pipeline_mode=pl.Buffered(1)) if nout == tn else {}
    return pl.pallas_call(
        _mm_resid_body,
        grid=(n // tm, nout // tn),
        in_specs=[pl.BlockSpec((tm, k), lambda i, j: (i, 0)),
                  pl.BlockSpec((None,) * nlead + (k, tn), lambda i, j: tuple(w_index) + (0, j), **w_mode),
                  pl.BlockSpec((tm, tn), lambda i, j: (i, j))],
        out_specs=pl.BlockSpec((tm, tn), lambda i, j: (i, j)),
        out_shape=jax.ShapeDtypeStruct((n, nout), F32),
        compiler_params=_params(("parallel", "arbitrary")),
        name="mm_resid",
    )(x, w, resid)


def _ple_body(x_ref, g_ref, wg_ref, p_ref, wp_ref, fg_ref, o_ref, *, tn, final):
    x = x_ref[...]
    xn = _rms(x, g_ref[...]).astype(BF16)
    pb = p_ref[...].astype(BF16)
    d = x.shape[1]
    for c in range(d // tn):
        sl = slice(c * tn, (c + 1) * tn)
        gate = jnp.dot(xn, wg_ref[:, sl], preferred_element_type=F32)
        proj = jnp.dot(pb, wp_ref[:, sl], preferred_element_type=F32)
        o_ref[:, sl] = x_ref[:, sl] + _sigmoid(gate) * proj
    if final:
        o_ref[...] = _rms(o_ref[...], fg_ref[...])


def _ple(x, gain, wg, p, wp, layer, final_gain, *, tm, tn, final):
    n, d = x.shape
    pd = p.shape[-1]
    rows = n // tm
    assert n % tm == 0 and d % tn == 0
    return pl.pallas_call(
        functools.partial(_ple_body, tn=tn, final=final),
        grid=(rows,),
        in_specs=[pl.BlockSpec((tm, d), lambda i: (i, 0)),
                  pl.BlockSpec((1, d), lambda i: (0, 0)),
                  pl.BlockSpec((None, d, d), lambda i: (layer, 0, 0), pipeline_mode=pl.Buffered(1)),
                  pl.BlockSpec((None, tm, pd), lambda i: (layer, i, 0)),
                  pl.BlockSpec((None, pd, d), lambda i: (layer, 0, 0), pipeline_mode=pl.Buffered(1)),
                  pl.BlockSpec((1, d), lambda i: (0, 0))],
        out_specs=pl.BlockSpec((tm, d), lambda i: (i, 0)),
        out_shape=jax.ShapeDtypeStruct((n, d), F32),
        compiler_params=_params(("parallel",)),
        name="ple",
    )(x, gain, wg, p, wp, final_gain)


def _rope_table_body(cos_ref, sin_ref):
    shape = cos_ref.shape
    pos = lax.broadcasted_iota(jnp.int32, shape, 0).astype(F32)
    lane = lax.broadcasted_iota(jnp.int32, shape, 1)
    half = HEAD // 2
    k = jnp.bitwise_and(lane, half - 1).astype(F32)
    inv_freq = jnp.exp(k * (-math.log(ROPE_THETA) / half))
    ang = pos * inv_freq
    cos_ref[...] = jnp.cos(ang)
    s = jnp.sin(ang)
    sin_ref[...] = jnp.where(lane < half, -s, s)


def _rope_tables(seq):
    return pl.pallas_call(
        _rope_table_body,
        out_shape=(jax.ShapeDtypeStruct((seq, HEAD), F32), jax.ShapeDtypeStruct((seq, HEAD), F32)),
        name="rope_tables",
    )()


def _split3(x):
    hi = x.astype(BF16)
    r1 = x - hi.astype(F32)
    mid = r1.astype(BF16)
    lo = (r1 - mid.astype(F32)).astype(BF16)
    return hi, mid, lo


def _hgrn_consts():
    r = np.arange(HG_SUB)
    chunk, blk = r // HG_CHUNK, r // SUBLANES
    nch = HG_SUB // HG_CHUNK
    same_chunk = chunk[:, None] == chunk[None, :]
    tril = same_chunk & (r[None, :] <= r[:, None])
    off = same_chunk & (blk[:, None] > blk[None, :])
    diag = (blk[:, None] == blk[None, :]) & (r[None, :] <= r[:, None])
    lane_blk = np.arange(SUBLANES * HEAD) // HEAD
    spread = lane_blk[:, None] == (r % SUBLANES)[None, :]
    bpc = HG_CHUNK // SUBLANES
    kmask = np.stack([np.broadcast_to((((r % HG_CHUNK) // SUBLANES) == jb)[:, None], (HG_SUB, HEAD))
                      for jb in range(bpc - 1)])
    cmask = np.stack([np.broadcast_to((chunk == c)[None, :], (HEAD, HG_SUB)) for c in range(nch)])
    eye = np.eye(HEAD)
    return (jnp.asarray(tril, BF16), jnp.asarray(off, BF16), jnp.asarray(diag, BF16),
            jnp.asarray(spread, BF16), jnp.asarray(kmask, BF16), jnp.asarray(cmask, BF16), jnp.asarray(eye, BF16))


def _hgrn_front(q, z, v, lb, tril, kmask):
    rows = q.shape[0]
    nch = rows // HG_CHUNK
    nblk = rows // SUBLANES
    bpc = HG_CHUNK // SUBLANES

    log2e = math.log2(math.e)
    soft = jnp.log2(1.0 + jnp.exp2(jnp.abs(z) * -log2e))
    lg_lb = jnp.log2(lb)
    lg_1mlb = jnp.log2(1.0 - lb)
    lg_sig = lg_1mlb + (jnp.minimum(z, 0.0) * log2e - soft)
    lg_k = lg_1mlb - (jnp.maximum(z, 0.0) * log2e + soft)
    logf = jnp.maximum(lg_lb, lg_sig) + jnp.log2(1.0 + jnp.exp2(-jnp.abs(lg_lb - lg_sig)))
    qs = q * (HEAD ** -0.5)

    hi, mid, lo = _split3(logf)
    parts = jnp.dot(tril, jnp.concatenate([hi, mid, lo], axis=1), preferred_element_type=F32)
    b = parts[:, :HEAD] + parts[:, HEAD:2 * HEAD] + parts[:, 2 * HEAD:]

    vb = v.astype(BF16)
    c = b - lg_k
    b3 = b.reshape(nch, HG_CHUNK, HEAD)
    b_last = b3[:, HG_CHUNK - 1:HG_CHUNK, :]
    q_in = (qs * jnp.exp2(b)).astype(BF16)
    k_end = jnp.exp2(b_last - c.reshape(nch, HG_CHUNK, HEAD)).reshape(rows, HEAD).astype(BF16)
    decay_cols = jnp.exp2(b_last).reshape(nch, HEAD).T

    qs3 = qs.reshape(nch, HG_CHUNK, HEAD)
    b8 = b.reshape(nblk, SUBLANES, HEAD)
    c8 = c.reshape(nblk, SUBLANES, HEAD)
    k_blk = jnp.exp2(b8[:, SUBLANES - 1:SUBLANES, :] - c8).reshape(rows, HEAD).astype(BF16)
    q_groups, k_groups = [], []
    for jb in range(bpc - 1):
        ref = b3[:, jb * SUBLANES + SUBLANES - 1:jb * SUBLANES + SUBLANES, :]
        qg = qs3 * jnp.exp2(jnp.minimum(b3 - ref, 0.0))
        q_groups.append(qg.reshape(rows, HEAD).astype(BF16))
        k_groups.append(k_blk * kmask[jb])
    q_cat = jnp.concatenate(q_groups, axis=1)
    k_cat = jnp.concatenate(k_groups, axis=1)

    q8 = qs.reshape(nblk, SUBLANES, HEAD)
    prods = []
    for i in range(SUBLANES):
        e = jnp.exp2(jnp.minimum(b8 - c8[:, i:i + 1, :], 0.0))
        prods.append((e * q8).reshape(rows, HEAD).astype(BF16))
    p_cat = jnp.concatenate(prods, axis=1)
    return q_cat, k_cat, p_cat, q_in, k_end, decay_cols, vb


def _hgrn_back(front, state, off, diag, spread, cmask, eye):
    q_cat, k_cat, p_cat, q_in, k_end, decay_cols, vb = front
    nch = decay_cols.shape[1]
    a_off = lax.dot_general(q_cat, k_cat, (((1,), (1,)), ((), ())), preferred_element_type=F32)
    a_diag = jnp.dot(p_cat, spread, preferred_element_type=F32)
    a = a_off.astype(BF16) * off + a_diag.astype(BF16) * diag
    o = jnp.dot(a, vb, preferred_element_type=F32)

    k_end_t = lax.dot_general(eye, k_end, (((1,), (1,)), ((), ())), preferred_element_type=F32)
    k_end_tb = k_end_t.astype(BF16)
    k_stack = jnp.concatenate([k_end_tb * cmask[c] for c in range(nch)], axis=0)
    upd = jnp.dot(k_stack, vb, preferred_element_type=F32)

    outs = []
    for c in range(nch):
        outs.append(jnp.dot(q_in[c * HG_CHUNK:(c + 1) * HG_CHUNK], state.astype(BF16), preferred_element_type=F32))
        state = state * decay_cols[:, c:c + 1] + upd[c * HEAD:(c + 1) * HEAD]
    o = o + jnp.concatenate(outs, axis=0)
    return o, state


def _hgrn_body(q_ref, z_ref, v_ref, g_ref, lbp_ref, gain_ref, tril_ref, off_ref, diag_ref, spread_ref, kmask_ref,
               cmask_ref, eye_ref, o_ref, state_ref, *, layer):
    lbp = lbp_ref[...]
    e = jnp.exp(lbp - jnp.max(lbp, axis=0, keepdims=True))
    lb = jnp.sum(e[:layer + 1], axis=0, keepdims=True) / jnp.sum(e, axis=0, keepdims=True)

    state_ref[...] = jnp.zeros_like(state_ref)
    consts = (tril_ref[...], off_ref[...], diag_ref[...], spread_ref[...], kmask_ref[...], cmask_ref[...],
              eye_ref[...])

    def slabs(s, carry):
        state = state_ref[...]
        sls = [pl.ds(pl.multiple_of((s * HG_UNROLL + u) * HG_SUB, HG_SUB), HG_SUB) for u in range(HG_UNROLL)]
        fronts = [_hgrn_front(q_ref[sl, :].astype(F32), z_ref[sl, :], v_ref[sl, :].astype(F32), lb, consts[0],
                              consts[4]) for sl in sls]
        for sl, front in zip(sls, fronts):
            o, state = _hgrn_back(front, state, consts[1], consts[2], consts[3], consts[5], consts[6])
            on = _rms(o, gain_ref[...])
            g = g_ref[sl, :].astype(F32)
            o_ref[sl, :] = (on * (g * _sigmoid(g))).astype(o_ref.dtype)
        state_ref[...] = state
        return carry

    lax.fori_loop(0, q_ref.shape[0] // (HG_SUB * HG_UNROLL), slabs, 0)


def _hgrn(qvg, z, lbp, gain, layer, *, batch, seq, heads):
    n = z.shape[0]
    assert seq % (HG_SUB * HG_UNROLL) == 0
    consts = _hgrn_consts()
    sec = lambda k: pl.BlockSpec((seq, HEAD), lambda b, h: (b, k * heads + h))
    whole = lambda a: pl.BlockSpec(a.shape, lambda b, h: (0,) * a.ndim)
    return pl.pallas_call(
        functools.partial(_hgrn_body, layer=layer),
        grid=(batch, heads),
        in_specs=[sec(0), sec(0), sec(1), sec(2),
                  pl.BlockSpec((lbp.shape[0], HEAD), lambda b, h: (0, h)),
                  pl.BlockSpec((1, HEAD), lambda b, h: (0, 0))] + [whole(a) for a in consts],
        out_specs=pl.BlockSpec((seq, HEAD), lambda b, h: (b, h)),
        out_shape=jax.ShapeDtypeStruct((n, heads * HEAD), BF16),
        scratch_shapes=[pltpu.VMEM((HEAD, HEAD), F32)],
        compiler_params=_params(("parallel", "parallel")),
        name="hgrn",
    )(qvg, z, qvg, qvg, lbp, gain, *consts)


def _attn_body(lam_ref, sub_ref, q_ref, k_ref, v_ref, o_ref, acc_ref, m_ref, l_ref, s0_ref, s1_ref, *, tq, tk,
               lambda_init):
    nrep = tk // LANES
    lam = lam_ref[...]
    lam_full = (jnp.exp(jnp.sum(lam[0:1] * lam[1:2], axis=-1, keepdims=True))
                - jnp.exp(jnp.sum(lam[2:3] * lam[3:4], axis=-1, keepdims=True)) + lambda_init)
    bufs = (s0_ref, s1_ref)

    def scores(rows, j, s_ref):
        kt = k_ref[j * tk:(j + 1) * tk, :]
        q = q_ref[rows, :]
        for c in range(2):
            cs = slice(c * HEAD, (c + 1) * HEAD)
            s_ref[c] = lax.dot_general(q[:, cs], kt[:, cs], (((1,), (1,)), ((), ())), preferred_element_type=F32)

    def consume(j, s_ref, first, masked):
        vt = v_ref[j * tk:(j + 1) * tk, :]
        ps, alphas = [], []
        for c in range(2):
            s = s_ref[c]
            if masked:
                ri = lax.broadcasted_iota(jnp.int32, s.shape, 0)
                ci = lax.broadcasted_iota(jnp.int32, s.shape, 1)
                s = jnp.where(ci <= ri, s, NEG)
            m_tile = jnp.max(s, axis=-1, keepdims=True)
            if first:
                m_new = jnp.broadcast_to(m_tile, (tq, LANES))
            else:
                m_old = m_ref[c]
                m_new = jnp.maximum(m_old, m_tile)
                alphas.append(jnp.exp2(m_old - m_new))
            p = jnp.exp2(s - jnp.concatenate([m_new] * nrep, axis=1))
            l_tile = jnp.sum(p, axis=-1, keepdims=True)
            l_ref[c] = jnp.broadcast_to(l_tile, (tq, LANES)) if first else alphas[c] * l_ref[c] + l_tile
            m_ref[c] = m_new
            ps.append(p.astype(BF16))
        pv = jnp.dot(jnp.concatenate(ps, axis=0), vt, preferred_element_type=F32)
        for c in range(2):
            part = pv[c * tq:(c + 1) * tq]
            acc_ref[c] = part if first else jnp.concatenate([alphas[c]] * 2, axis=1) * acc_ref[c] + part

    for qi in range(q_ref.shape[0] // tq):
        rows = slice(qi * tq, (qi + 1) * tq)
        scores(rows, 0, bufs[0])
        for j in range(qi + 1):
            if j < qi:
                scores(rows, j + 1, bufs[(j + 1) % 2])
            consume(j, bufs[j % 2], first=(j == 0), masked=(j == qi))
        inv0 = jnp.concatenate([1.0 / l_ref[0]] * 2, axis=1)
        inv1 = jnp.concatenate([1.0 / l_ref[1]] * 2, axis=1)
        o = acc_ref[0] * inv0 - lam_full * (acc_ref[1] * inv1)
        o_ref[rows, :] = (_rms(o, sub_ref[...]) * (1.0 - lambda_init)).astype(o_ref.dtype)


def _attention(q, kv, lam, subln, *, batch, seq, heads, tq, lambda_init):
    n = q.shape[0]
    vd = 2 * HEAD
    assert seq % tq == 0
    return pl.pallas_call(
        functools.partial(_attn_body, tq=tq, tk=tq, lambda_init=lambda_init),
        grid=(batch, heads),
        in_specs=[pl.BlockSpec((4, HEAD), lambda b, h: (0, 0)),
                  pl.BlockSpec((1, vd), lambda b, h: (0, 0)),
                  pl.BlockSpec((seq, vd), lambda b, h: (b, h)),
                  pl.BlockSpec((seq, vd), lambda b, h: (b, h)),
                  pl.BlockSpec((seq, vd), lambda b, h: (b, heads + h))],
        out_specs=pl.BlockSpec((seq, vd), lambda b, h: (b, h)),
        out_shape=jax.ShapeDtypeStruct((n, heads * vd), BF16),
        scratch_shapes=[pltpu.VMEM((2, tq, vd), F32), pltpu.VMEM((2, tq, LANES), F32),
                        pltpu.VMEM((2, tq, LANES), F32), pltpu.VMEM((2, tq, tq), F32), pltpu.VMEM((2, tq, tq), F32)],
        compiler_params=_params(("parallel", "parallel")),
        name="diff_attn",
    )(lam, subln, q, kv, kv)


def kernel(x, p, ffn_norm, ffn_w_gate_up, ffn_w_down, mix_norm, hgrn_w_in, hgrn_lower_bounds, hgrn_out_norm, hgrn_w_out, kv_norm, w_kv, diff_w_q, diff_lambda, diff_subln, diff_w_out, ple_norm, ple_w_gate, ple_w_proj, final_norm):
    batch, seq, d = x.shape
    depth = p.shape[0]
    n_a = hgrn_w_in.shape[0]
    n = batch * seq
    hg_heads = d // HEAD
    da_heads = d // (2 * HEAD)
    k_cols = da_heads * 2 * HEAD

    tm = min(1024, seq)
    tq = min(512, seq)
    wide = hg_heads * HEAD

    wgu = ffn_w_gate_up.astype(BF16)
    wdn = ffn_w_down.astype(BF16)
    w_in = hgrn_w_in.astype(BF16)
    w_ho = hgrn_w_out.astype(BF16)
    w_kvb = w_kv.astype(BF16)
    w_q = diff_w_q.astype(BF16)
    w_do = diff_w_out.astype(BF16)
    w_pg = ple_w_gate.astype(BF16)
    w_pp = ple_w_proj.astype(BF16)
    p2 = p.reshape(depth, n, p.shape[-1])

    rope = _rope_tables(seq)
    h = x.reshape(n, d)
    kv = None
    for i in range(depth):
        h = _ffn(h, ffn_norm[i, 0].reshape(1, d), wgu, wdn, i, 0, tm=tm, tf=512)
        if i < n_a:
            gain = mix_norm[i].reshape(1, d)
            z = _norm_mm(h, gain, w_in, (i,), tm=tm, tn=wide, out_dtype=F32, ntiles=1, src_tile=lambda j: j + 1)
            qvg = _norm_mm_rows(h, gain, w_in, (i,), ((0, wide), (1, 2 * wide)), tm=min(512, seq), tn=1024)
            og = _hgrn(qvg, z, hgrn_lower_bounds, hgrn_out_norm[i].reshape(1, HEAD), i,
                       batch=batch, seq=seq, heads=hg_heads)
            h = _mm_resid(og, w_ho, (i,), h, tm=tm, tn=d)
        else:
            j = i - n_a
            lambda_init = 0.8 - 0.6 * math.exp(-0.3 * i)
            q = _norm_mm(h, mix_norm[i].reshape(1, d), w_q, (j,), tm=tm, tn=k_cols, out_dtype=BF16,
                         rope=rope, seq=seq, rope_cols=k_cols, scale=HEAD ** -0.5 * math.log2(math.e))
            oa = _attention(q, kv, diff_lambda[j], diff_subln[j].reshape(1, 2 * HEAD),
                            batch=batch, seq=seq, heads=da_heads, tq=tq, lambda_init=lambda_init)
            h = _mm_resid(oa, w_do, (j,), h, tm=tm, tn=d)
        h = _ffn(h, ffn_norm[i, 1].reshape(1, d), wgu, wdn, i, 1, tm=tm, tf=512)
        last = i == depth - 1
        h = _ple(h, ple_norm[i].reshape(1, d), w_pg, p2, w_pp, i, final_norm.reshape(1, d),
                 tm=min(512, seq), tn=512, final=last)
        if i == n_a - 1:
            kv = _norm_mm_rows(h, kv_norm.reshape(1, d), w_kvb, (), ((0, w_kvb.shape[-1]),), tm=min(512, seq), tn=512,
                               rope=rope, seq=seq, rope_cols=k_cols)
    return h.reshape(batch, seq, d)
```

```python
import functools
import math

import jax
import jax.numpy as jnp
import numpy as np
from jax import lax
from jax.experimental import pallas as pl
from jax.experimental.pallas import tpu as pltpu

F32 = jnp.float32
BF16 = jnp.bfloat16

NORM_EPS = 1e-6
ROPE_THETA = 10000.0
HEAD = 128
LANES = 128
SUBLANES = 8
HG_CHUNK = 32
HG_SUB = 256
HG_UNROLL = 8
NEG = -1e30

VMEM_LIMIT = 60 * 1024 * 1024


def _params(sem):
    return pltpu.CompilerParams(dimension_semantics=sem, vmem_limit_bytes=VMEM_LIMIT)


def _rms(x, g):
    ms = jnp.mean(x * x, axis=-1, keepdims=True)
    return x * lax.rsqrt(ms + NORM_EPS) * g


def _sigmoid(x):
    return 1.0 / (1.0 + jnp.exp(-x))


def _ffn_body(x_ref, g_ref, wg_ref, wu_ref, wd_ref, wdl_ref, o_ref, xn_ref, act_ref):
    f = pl.program_id(1)
    last = pl.num_programs(1) - 1

    def gate_up():
        xn = xn_ref[...]
        gate = jnp.dot(xn, wg_ref[...], preferred_element_type=F32)
        up = jnp.dot(xn, wu_ref[...], preferred_element_type=F32)
        return (gate * _sigmoid(gate) * (0.5 * up)).astype(BF16)

    @pl.when(f == 0)
    def _():
        x = x_ref[...]
        xn_ref[...] = _rms(x, g_ref[...]).astype(BF16)
        o_ref[...] = x
        act_ref[...] = gate_up()

    @pl.when(f > 0)
    def _():
        part = jnp.dot(act_ref[...], wd_ref[...], preferred_element_type=F32)
        act = gate_up()
        o_ref[...] += part
        act_ref[...] = act

    @pl.when(f == last)
    def _():
        o_ref[...] += jnp.dot(act_ref[...], wdl_ref[...], preferred_element_type=F32)


def _ffn(x, gain, wgu, wd, li, lj, *, tm, tf):
    n, d = x.shape
    ff = wd.shape[2]
    nf = ff // tf
    assert n % tm == 0 and ff % tf == 0 and nf >= 2
    return pl.pallas_call(
        _ffn_body,
        grid=(n // tm, nf),
        in_specs=[
            pl.BlockSpec((tm, d), lambda i, f: (i, 0)),
            pl.BlockSpec((1, d), lambda i, f: (0, 0)),
            pl.BlockSpec((None, None, d, tf), lambda i, f: (li, lj, 0, f)),
            pl.BlockSpec((None, None, d, tf), lambda i, f: (li, lj, 0, f + nf)),
            pl.BlockSpec((None, None, tf, d), lambda i, f: (li, lj, jnp.maximum(f - 1, 0), 0)),
            pl.BlockSpec((None, None, tf, d), lambda i, f: (li, lj, nf - 1, 0), pipeline_mode=pl.Buffered(1)),
        ],
        out_specs=pl.BlockSpec((tm, d), lambda i, f: (i, 0)),
        out_shape=jax.ShapeDtypeStruct((n, d), F32),
        scratch_shapes=[pltpu.VMEM((tm, d), BF16), pltpu.VMEM((tm, tf), BF16)],
        compiler_params=_params(("parallel", "arbitrary")),
        name="ffn",
    )(x, gain, wgu, wgu, wd, wd)


def _rope_store(y, cos, sin, o_ref, scale):
    for g in range(y.shape[1] // HEAD):
        yg = y[:, g * HEAD:(g + 1) * HEAD]
        r = yg * cos + pltpu.roll(yg, HEAD // 2, axis=1) * sin
        if scale != 1.0:
            r = r * scale
        o_ref[:, g * HEAD:(g + 1) * HEAD] = r.astype(o_ref.dtype)


def _norm_mm_body(x_ref, g_ref, w_ref, o_ref, xn_ref):
    @pl.when(pl.program_id(1) == 0)
    def _():
        xn_ref[...] = _rms(x_ref[...], g_ref[...]).astype(BF16)

    o_ref[...] = jnp.dot(xn_ref[...], w_ref[...], preferred_element_type=F32).astype(o_ref.dtype)


def _norm_mm_rope_body(x_ref, g_ref, w_ref, cos_ref, sin_ref, o_ref, xn_ref, *, rope_tiles, scale):
    j = pl.program_id(1)

    @pl.when(j == 0)
    def _():
        xn_ref[...] = _rms(x_ref[...], g_ref[...]).astype(BF16)

    y = jnp.dot(xn_ref[...], w_ref[...], preferred_element_type=F32)

    @pl.when(j < rope_tiles)
    def _():
        _rope_store(y, cos_ref[...], sin_ref[...], o_ref, scale)

    @pl.when(j >= rope_tiles)
    def _():
        o_ref[...] = y.astype(o_ref.dtype)


def _norm_mm(x, gain, w, w_index, *, tm, tn, out_dtype, rope=None, seq=None, rope_cols=0, scale=1.0,
             ntiles=None, src_tile=lambda j: j):
    n, d = x.shape
    nout = (w.shape[-1] // tn if ntiles is None else ntiles) * tn
    assert n % tm == 0 and w.shape[-1] % tn == 0
    nlead = w.ndim - 2
    w_mode = dict(pipeline_mode=pl.Buffered(1)) if nout == tn else {}
    w_spec = pl.BlockSpec((None,) * nlead + (d, tn), lambda i, j: tuple(w_index) + (0, src_tile(j)), **w_mode)
    in_specs = [pl.BlockSpec((tm, d), lambda i, j: (i, 0)),
                pl.BlockSpec((1, d), lambda i, j: (0, 0)),
                w_spec]
    args = [x, gain, w]
    if rope is None:
        body = _norm_mm_body
    else:
        assert seq % tm == 0 and rope_cols % tn == 0
        nseq = seq // tm
        tab_spec = pl.BlockSpec((tm, HEAD), lambda i, j: (i % nseq, 0))
        in_specs += [tab_spec, tab_spec]
        args += list(rope)
        body = functools.partial(_norm_mm_rope_body, rope_tiles=rope_cols // tn, scale=scale)
    return pl.pallas_call(
        body,
        grid=(n // tm, nout // tn),
        in_specs=in_specs,
        out_specs=pl.BlockSpec((tm, tn), lambda i, j: (i, j)),
        out_shape=jax.ShapeDtypeStruct((n, nout), out_dtype),
        scratch_shapes=[pltpu.VMEM((tm, d), BF16)],
        compiler_params=_params(("parallel", "arbitrary")),
        name="norm_mm",
    )(*args)


def _norm_mm_rows_body(*refs, nw, tn, rope_tiles, scale):
    x_ref, g_ref = refs[:2]
    w_refs = refs[2:2 + nw]
    o_ref = refs[-1]
    xn = _rms(x_ref[...], g_ref[...]).astype(BF16)
    if rope_tiles:
        cos, sin = refs[2 + nw][...], refs[3 + nw][...]
    t = 0
    for w_ref in w_refs:
        for c in range(w_ref.shape[1] // tn):
            y = jnp.dot(xn, w_ref[:, c * tn:(c + 1) * tn], preferred_element_type=F32)
            cols = slice(t * tn, (t + 1) * tn)
            if t < rope_tiles:
                _rope_store(y, cos, sin, o_ref.at[:, cols], scale)
            else:
                o_ref[:, cols] = y.astype(o_ref.dtype)
            t += 1


def _norm_mm_rows(x, gain, w, w_index, col_blocks, *, tm, tn, rope=None, seq=None, rope_cols=0, scale=1.0):
    n, d = x.shape
    nout = sum(width for _, width in col_blocks)
    assert n % tm == 0 and all(width % tn == 0 for _, width in col_blocks) and rope_cols % tn == 0
    nlead = w.ndim - 2
    in_specs = [pl.BlockSpec((tm, d), lambda i: (i, 0)), pl.BlockSpec((1, d), lambda i: (0, 0))]
    in_specs += [pl.BlockSpec((None,) * nlead + (d, width), lambda i, blk=blk: tuple(w_index) + (0, blk),
                              pipeline_mode=pl.Buffered(1)) for blk, width in col_blocks]
    args = [x, gain] + [w] * len(col_blocks)
    if rope is not None:
        assert seq % tm == 0
        nseq = seq // tm
        in_specs += [pl.BlockSpec((tm, HEAD), lambda i: (i % nseq, 0))] * 2
        args += list(rope)
    return pl.pallas_call(
        functools.partial(_norm_mm_rows_body, nw=len(col_blocks), tn=tn, rope_tiles=rope_cols // tn, scale=scale),
        grid=(n // tm,),
        in_specs=in_specs,
        out_specs=pl.BlockSpec((tm, nout), lambda i: (i, 0)),
        out_shape=jax.ShapeDtypeStruct((n, nout), BF16),
        compiler_params=_params(("parallel",)),
        name="norm_mm_rows",
    )(*args)


def _mm_resid_body(x_ref, w_ref, r_ref, o_ref):
    o_ref[...] = r_ref[...] + jnp.dot(x_ref[...], w_ref[...], preferred_element_type=F32)


def _mm_resid(x, w, w_index, resid, *, tm, tn):
    n, k = x.shape
    nout = w.shape[-1]
    assert n % tm == 0 and nout % tn == 0
    nlead = w.ndim - 2
    w_mode = dict(pipeline_mode=pl.Buffered(1)) if nout == tn else {}
    return pl.pallas_call(
        _mm_resid_body,
        grid=(n // tm, nout // tn),
        in_specs=[pl.BlockSpec((tm, k), lambda i, j: (i, 0)),
                  pl.BlockSpec((None,) * nlead + (k, tn), lambda i, j: tuple(w_index) + (0, j), **w_mode),
                  pl.BlockSpec((tm, tn), lambda i, j: (i, j))],
        out_specs=pl.BlockSpec((tm, tn), lambda i, j: (i, j)),
        out_shape=jax.ShapeDtypeStruct((n, nout), F32),
        compiler_params=_params(("parallel", "arbitrary")),
        name="mm_resid",
    )(x, w, resid)


def _ple_body(x_ref, g_ref, wg_ref, p_ref, wp_ref, fg_ref, o_ref, *, tn, final):
    x = x_ref[...]
    xn = _rms(x, g_ref[...]).astype(BF16)
    pb = p_ref[...].astype(BF16)
    d = x.shape[1]
    for c in range(d // tn):
        sl = slice(c * tn, (c + 1) * tn)
        gate = jnp.dot(xn, wg_ref[:, sl], preferred_element_type=F32)
        proj = jnp.dot(pb, wp_ref[:, sl], preferred_element_type=F32)
        o_ref[:, sl] = x_ref[:, sl] + _sigmoid(gate) * proj
    if final:
        o_ref[...] = _rms(o_ref[...], fg_ref[...])


def _ple(x, gain, wg, p, wp, layer, final_gain, *, tm, tn, final):
    n, d = x.shape
    pd = p.shape[-1]
    rows = n // tm
    assert n % tm == 0 and d % tn == 0
    return pl.pallas_call(
        functools.partial(_ple_body, tn=tn, final=final),
        grid=(rows,),
        in_specs=[pl.BlockSpec((tm, d), lambda i: (i, 0)),
                  pl.BlockSpec((1, d), lambda i: (0, 0)),
                  pl.BlockSpec((None, d, d), lambda i: (layer, 0, 0), pipeline_mode=pl.Buffered(1)),
                  pl.BlockSpec((None, tm, pd), lambda i: (layer, i, 0)),
                  pl.BlockSpec((None, pd, d), lambda i: (layer, 0, 0), pipeline_mode=pl.Buffered(1)),
                  pl.BlockSpec((1, d), lambda i: (0, 0))],
        out_specs=pl.BlockSpec((tm, d), lambda i: (i, 0)),
        out_shape=jax.ShapeDtypeStruct((n, d), F32),
        compiler_params=_params(("parallel",)),
        name="ple",
    )(x, gain, wg, p, wp, final_gain)


def _rope_table_body(cos_ref, sin_ref):
    shape = cos_ref.shape
    pos = lax.broadcasted_iota(jnp.int32, shape, 0).astype(F32)
    lane = lax.broadcasted_iota(jnp.int32, shape, 1)
    half = HEAD // 2
    k = jnp.bitwise_and(lane, half - 1).astype(F32)
    inv_freq = jnp.exp(k * (-math.log(ROPE_THETA) / half))
    ang = pos * inv_freq
    cos_ref[...] = jnp.cos(ang)
    s = jnp.sin(ang)
    sin_ref[...] = jnp.where(lane < half, -s, s)


def _rope_tables(seq):
    return pl.pallas_call(
        _rope_table_body,
        out_shape=(jax.ShapeDtypeStruct((seq, HEAD), F32), jax.ShapeDtypeStruct((seq, HEAD), F32)),
        name="rope_tables",
    )()


def _split3(x):
    hi = x.astype(BF16)
    r1 = x - hi.astype(F32)
    mid = r1.astype(BF16)
    lo = (r1 - mid.astype(F32)).astype(BF16)
    return hi, mid, lo


def _hgrn_consts():
    r = np.arange(HG_SUB)
    chunk, blk = r // HG_CHUNK, r // SUBLANES
    nch = HG_SUB // HG_CHUNK
    same_chunk = chunk[:, None] == chunk[None, :]
    tril = same_chunk & (r[None, :] <= r[:, None])
    off = same_chunk & (blk[:, None] > blk[None, :])
    diag = (blk[:, None] == blk[None, :]) & (r[None, :] <= r[:, None])
    lane_blk = np.arange(SUBLANES * HEAD) // HEAD
    spread = lane_blk[:, None] == (r % SUBLANES)[None, :]
    bpc = HG_CHUNK // SUBLANES
    kmask = np.stack([np.broadcast_to((((r % HG_CHUNK) // SUBLANES) == jb)[:, None], (HG_SUB, HEAD))
                      for jb in range(bpc - 1)])
    cmask = np.stack([np.broadcast_to((chunk == c)[None, :], (HEAD, HG_SUB)) for c in range(nch)])
    eye = np.eye(HEAD)
    return (jnp.asarray(tril, BF16), jnp.asarray(off, BF16), jnp.asarray(diag, BF16),
            jnp.asarray(spread, BF16), jnp.asarray(kmask, BF16), jnp.asarray(cmask, BF16), jnp.asarray(eye, BF16))


def _hgrn_front(q, z, v, lb, tril, kmask):
    rows = q.shape[0]
    nch = rows // HG_CHUNK
    nblk = rows // SUBLANES
    bpc = HG_CHUNK // SUBLANES

    log2e = math.log2(math.e)
    soft = jnp.log2(1.0 + jnp.exp2(jnp.abs(z) * -log2e))
    lg_lb = jnp.log2(lb)
    lg_1mlb = jnp.log2(1.0 - lb)
    lg_sig = lg_1mlb + (jnp.minimum(z, 0.0) * log2e - soft)
    lg_k = lg_1mlb - (jnp.maximum(z, 0.0) * log2e + soft)
    logf = jnp.maximum(lg_lb, lg_sig) + jnp.log2(1.0 + jnp.exp2(-jnp.abs(lg_lb - lg_sig)))
    qs = q * (HEAD ** -0.5)

    hi, mid, lo = _split3(logf)
    parts = jnp.dot(tril, jnp.concatenate([hi, mid, lo], axis=1), preferred_element_type=F32)
    b = parts[:, :HEAD] + parts[:, HEAD:2 * HEAD] + parts[:, 2 * HEAD:]

    vb = v.astype(BF16)
    c = b - lg_k
    b3 = b.reshape(nch, HG_CHUNK, HEAD)
    b_last = b3[:, HG_CHUNK - 1:HG_CHUNK, :]
    q_in = (qs * jnp.exp2(b)).astype(BF16)
    k_end = jnp.exp2(b_last - c.reshape(nch, HG_CHUNK, HEAD)).reshape(rows, HEAD).astype(BF16)
    decay_cols = jnp.exp2(b_last).reshape(nch, HEAD).T

    qs3 = qs.reshape(nch, HG_CHUNK, HEAD)
    b8 = b.reshape(nblk, SUBLANES, HEAD)
    c8 = c.reshape(nblk, SUBLANES, HEAD)
    k_blk = jnp.exp2(b8[:, SUBLANES - 1:SUBLANES, :] - c8).reshape(rows, HEAD).astype(BF16)
    q_groups, k_groups = [], []
    for jb in range(bpc - 1):
        ref = b3[:, jb * SUBLANES + SUBLANES - 1:jb * SUBLANES + SUBLANES, :]
        qg = qs3 * jnp.exp2(jnp.minimum(b3 - ref, 0.0))
        q_groups.append(qg.reshape(rows, HEAD).astype(BF16))
        k_groups.append(k_blk * kmask[jb])
    q_cat = jnp.concatenate(q_groups, axis=1)
    k_cat = jnp.concatenate(k_groups, axis=1)

    q8 = qs.reshape(nblk, SUBLANES, HEAD)
    prods = []
    for i in range(SUBLANES):
        e = jnp.exp2(jnp.minimum(b8 - c8[:, i:i + 1, :], 0.0))
        prods.append((e * q8).reshape(rows, HEAD).astype(BF16))
    p_cat = jnp.concatenate(prods, axis=1)
    return q_cat, k_cat, p_cat, q_in, k_end, decay_cols, vb


def _hgrn_back(front, state, off, diag, spread, cmask, eye):
    q_cat, k_cat, p_cat, q_in, k_end, decay_cols, vb = front
    nch = decay_cols.shape[1]
    a_off = lax.dot_general(q_cat, k_cat, (((1,), (1,)), ((), ())), preferred_element_type=F32)
    a_diag = jnp.dot(p_cat, spread, preferred_element_type=F32)
    a = a_off.astype(BF16) * off + a_diag.astype(BF16) * diag
    o = jnp.dot(a, vb, preferred_element_type=F32)

    k_end_t = lax.dot_general(eye, k_end, (((1,), (1,)), ((), ())), preferred_element_type=F32)
    k_end_tb = k_end_t.astype(BF16)
    k_stack = jnp.concatenate([k_end_tb * cmask[c] for c in range(nch)], axis=0)
    upd = jnp.dot(k_stack, vb, preferred_element_type=F32)

    outs = []
    for c in range(nch):
        outs.append(jnp.dot(q_in[c * HG_CHUNK:(c + 1) * HG_CHUNK], state.astype(BF16), preferred_element_type=F32))
        state = state * decay_cols[:, c:c + 1] + upd[c * HEAD:(c + 1) * HEAD]
    o = o + jnp.concatenate(outs, axis=0)
    return o, state


def _hgrn_body(q_ref, z_ref, v_ref, g_ref, lbp_ref, gain_ref, tril_ref, off_ref, diag_ref, spread_ref, kmask_ref,
               cmask_ref, eye_ref, o_ref, state_ref, *, layer):
    lbp = lbp_ref[...]
    e = jnp.exp(lbp - jnp.max(lbp, axis=0, keepdims=True))
    lb = jnp.sum(e[:layer + 1], axis=0, keepdims=True) / jnp.sum(e, axis=0, keepdims=True)

    state_ref[...] = jnp.zeros_like(state_ref)
    consts = (tril_ref[...], off_ref[...], diag_ref[...], spread_ref[...], kmask_ref[...], cmask_ref[...],
              eye_ref[...])

    def slabs(s, carry):
        state = state_ref[...]
        sls = [pl.ds(pl.multiple_of((s * HG_UNROLL + u) * HG_SUB, HG_SUB), HG_SUB) for u in range(HG_UNROLL)]
        fronts = [_hgrn_front(q_ref[sl, :].astype(F32), z_ref[sl, :], v_ref[sl, :].astype(F32), lb, consts[0],
                              consts[4]) for sl in sls]
        for sl, front in zip(sls, fronts):
            o, state = _hgrn_back(front, state, consts[1], consts[2], consts[3], consts[5], consts[6])
            on = _rms(o, gain_ref[...])
            g = g_ref[sl, :].astype(F32)
            o_ref[sl, :] = (on * (g * _sigmoid(g))).astype(o_ref.dtype)
        state_ref[...] = state
        return carry

    lax.fori_loop(0, q_ref.shape[0] // (HG_SUB * HG_UNROLL), slabs, 0)


def _hgrn(qvg, z, lbp, gain, layer, *, batch, seq, heads):
    n = z.shape[0]
    assert seq % (HG_SUB * HG_UNROLL) == 0
    consts = _hgrn_consts()
    sec = lambda k: pl.BlockSpec((seq, HEAD), lambda b, h: (b, k * heads + h))
    whole = lambda a: pl.BlockSpec(a.shape, lambda b, h: (0,) * a.ndim)
    return pl.pallas_call(
        functools.partial(_hgrn_body, layer=layer),
        grid=(batch, heads),
        in_specs=[sec(0), sec(0), sec(1), sec(2),
                  pl.BlockSpec((lbp.shape[0], HEAD), lambda b, h: (0, h)),
                  pl.BlockSpec((1, HEAD), lambda b, h: (0, 0))] + [whole(a) for a in consts],
        out_specs=pl.BlockSpec((seq, HEAD), lambda b, h: (b, h)),
        out_shape=jax.ShapeDtypeStruct((n, heads * HEAD), BF16),
        scratch_shapes=[pltpu.VMEM((HEAD, HEAD), F32)],
        compiler_params=_params(("parallel", "parallel")),
        name="hgrn",
    )(qvg, z, qvg, qvg, lbp, gain, *consts)


def _attn_body(lam_ref, sub_ref, q_ref, k_ref, v_ref, o_ref, acc_ref, m_ref, l_ref, s0_ref, s1_ref, *, tq, tk,
               lambda_init):
    nrep = tk // LANES
    lam = lam_ref[...]
    lam_full = (jnp.exp(jnp.sum(lam[0:1] * lam[1:2], axis=-1, keepdims=True))
                - jnp.exp(jnp.sum(lam[2:3] * lam[3:4], axis=-1, keepdims=True)) + lambda_init)
    bufs = (s0_ref, s1_ref)

    def scores(rows, j, s_ref):
        kt = k_ref[j * tk:(j + 1) * tk, :]
        q = q_ref[rows, :]
        for c in range(2):
            cs = slice(c * HEAD, (c + 1) * HEAD)
            s_ref[c] = lax.dot_general(q[:, cs], kt[:, cs], (((1,), (1,)), ((), ())), preferred_element_type=F32)

    def consume(j, s_ref, first, masked):
        vt = v_ref[j * tk:(j + 1) * tk, :]
        ps, alphas = [], []
        for c in range(2):
            s = s_ref[c]
            if masked:
                ri = lax.broadcasted_iota(jnp.int32, s.shape, 0)
                ci = lax.broadcasted_iota(jnp.int32, s.shape, 1)
                s = jnp.where(ci <= ri, s, NEG)
            m_tile = jnp.max(s, axis=-1, keepdims=True)
            if first:
                m_new = jnp.broadcast_to(m_tile, (tq, LANES))
            else:
                m_old = m_ref[c]
                m_new = jnp.maximum(m_old, m_tile)
                alphas.append(jnp.exp2(m_old - m_new))
            p = jnp.exp2(s - jnp.concatenate([m_new] * nrep, axis=1))
            l_tile = jnp.sum(p, axis=-1, keepdims=True)
            l_ref[c] = jnp.broadcast_to(l_tile, (tq, LANES)) if first else alphas[c] * l_ref[c] + l_tile
            m_ref[c] = m_new
            ps.append(p.astype(BF16))
        pv = jnp.dot(jnp.concatenate(ps, axis=0), vt, preferred_element_type=F32)
        for c in range(2):
            part = pv[c * tq:(c + 1) * tq]
            acc_ref[c] = part if first else jnp.concatenate([alphas[c]] * 2, axis=1) * acc_ref[c] + part

    for qi in range(q_ref.shape[0] // tq):
        rows = slice(qi * tq, (qi + 1) * tq)
        scores(rows, 0, bufs[0])
        for j in range(qi + 1):
            if j < qi:
                scores(rows, j + 1, bufs[(j + 1) % 2])
            consume(j, bufs[j % 2], first=(j == 0), masked=(j == qi))
        inv0 = jnp.concatenate([1.0 / l_ref[0]] * 2, axis=1)
        inv1 = jnp.concatenate([1.0 / l_ref[1]] * 2, axis=1)
        o = acc_ref[0] * inv0 - lam_full * (acc_ref[1] * inv1)
        o_ref[rows, :] = (_rms(o, sub_ref[...]) * (1.0 - lambda_init)).astype(o_ref.dtype)


def _attention(q, kv, lam, subln, *, batch, seq, heads, tq, lambda_init):
    n = q.shape[0]
    vd = 2 * HEAD
    assert seq % tq == 0
    return pl.pallas_call(
        functools.partial(_attn_body, tq=tq, tk=tq, lambda_init=lambda_init),
        grid=(batch, heads),
        in_specs=[pl.BlockSpec((4, HEAD), lambda b, h: (0, 0)),
                  pl.BlockSpec((1, vd), lambda b, h: (0, 0)),
                  pl.BlockSpec((seq, vd), lambda b, h: (b, h)),
                  pl.BlockSpec((seq, vd), lambda b, h: (b, h)),
                  pl.BlockSpec((seq, vd), lambda b, h: (b, heads + h))],
        out_specs=pl.BlockSpec((seq, vd), lambda b, h: (b, h)),
        out_shape=jax.ShapeDtypeStruct((n, heads * vd), BF16),
        scratch_shapes=[pltpu.VMEM((2, tq, vd), F32), pltpu.VMEM((2, tq, LANES), F32),
                        pltpu.VMEM((2, tq, LANES), F32), pltpu.VMEM((2, tq, tq), F32), pltpu.VMEM((2, tq, tq), F32)],
        compiler_params=_params(("parallel", "parallel")),
        name="diff_attn",
    )(lam, subln, q, kv, kv)


def kernel(x, p, ffn_norm, ffn_w_gate_up, ffn_w_down, mix_norm, hgrn_w_in, hgrn_lower_bounds, hgrn_out_norm, hgrn_w_out, kv_norm, w_kv, diff_w_q, diff_lambda, diff_subln, diff_w_out, ple_norm, ple_w_gate, ple_w_proj, final_norm):
    batch, seq, d = x.shape
    depth = p.shape[0]
    n_a = hgrn_w_in.shape[0]
    n = batch * seq
    hg_heads = d // HEAD
    da_heads = d // (2 * HEAD)
    k_cols = da_heads * 2 * HEAD

    tm = min(1024, seq)
    tq = min(512, seq)
    wide = hg_heads * HEAD

    wgu = ffn_w_gate_up.astype(BF16)
    wdn = ffn_w_down.astype(BF16)
    w_in = hgrn_w_in.astype(BF16)
    w_ho = hgrn_w_out.astype(BF16)
    w_kvb = w_kv.astype(BF16)
    w_q = diff_w_q.astype(BF16)
    w_do = diff_w_out.astype(BF16)
    w_pg = ple_w_gate.astype(BF16)
    w_pp = ple_w_proj.astype(BF16)
    p2 = p.reshape(depth, n, p.shape[-1])

    rope = _rope_tables(seq)
    h = x.reshape(n, d)
    kv = None
    for i in range(depth):
        h = _ffn(h, ffn_norm[i, 0].reshape(1, d), wgu, wdn, i, 0, tm=tm, tf=512)
        if i < n_a:
            gain = mix_norm[i].reshape(1, d)
            z = _norm_mm(h, gain, w_in, (i,), tm=tm, tn=wide, out_dtype=F32, ntiles=1, src_tile=lambda j: j + 1)
            qvg = _norm_mm_rows(h, gain, w_in, (i,), ((0, wide), (1, 2 * wide)), tm=min(512, seq), tn=1024)
            og = _hgrn(qvg, z, hgrn_lower_bounds, hgrn_out_norm[i].reshape(1, HEAD), i,
                       batch=batch, seq=seq, heads=hg_heads)
            h = _mm_resid(og, w_ho, (i,), h, tm=tm, tn=d)
        else:
            j = i - n_a
            lambda_init = 0.8 - 0.6 * math.exp(-0.3 * i)
            q = _norm_mm(h, mix_norm[i].reshape(1, d), w_q, (j,), tm=tm, tn=k_cols, out_dtype=BF16,
                         rope=rope, seq=seq, rope_cols=k_cols, scale=HEAD ** -0.5 * math.log2(math.e))
            oa = _attention(q, kv, diff_lambda[j], diff_subln[j].reshape(1, 2 * HEAD),
                            batch=batch, seq=seq, heads=da_heads, tq=tq, lambda_init=lambda_init)
            h = _mm_resid(oa, w_do, (j,), h, tm=tm, tn=d)
        h = _ffn(h, ffn_norm[i, 1].reshape(1, d), wgu, wdn, i, 1, tm=tm, tf=512)
        last = i == depth - 1
        h = _ple(h, ple_norm[i].reshape(1, d), w_pg, p2, w_pp, i, final_norm.reshape(1, d),
                 tm=min(512, seq), tn=256, final=last)
        if i == n_a - 1:
            kv = _norm_mm_rows(h, kv_norm.reshape(1, d), w_kvb, (), ((0, w_kvb.shape[-1]),), tm=tm, tn=512,
                               rope=rope, seq=seq, rope_cols=k_cols)
    return h.reshape(batch, seq, d)
```

```python
import functools
import math

import jax
import jax.numpy as jnp
import numpy as np
from jax import lax
from jax.experimental import pallas as pl
from jax.experimental.pallas import tpu as pltpu

F32 = jnp.float32
BF16 = jnp.bfloat16

NORM_EPS = 1e-6
ROPE_THETA = 10000.0
HEAD = 128
LANES = 128
SUBLANES = 8
HG_CHUNK = 32
HG_SUB = 256
HG_UNROLL = 8
NEG = -1e30

VMEM_LIMIT = 60 * 1024 * 1024


def _params(sem):
    return pltpu.CompilerParams(dimension_semantics=sem, vmem_limit_bytes=VMEM_LIMIT)


def _rms(x, g):
    ms = jnp.mean(x * x, axis=-1, keepdims=True)
    return x * lax.rsqrt(ms + NORM_EPS) * g


def _sigmoid(x):
    return 1.0 / (1.0 + jnp.exp(-x))


def _ffn_body(x_ref, g_ref, wg_ref, wu_ref, wd_ref, wdl_ref, o_ref, xn_ref, act_ref):
    f = pl.program_id(1)
    last = pl.num_programs(1) - 1

    def gate_up():
        xn = xn_ref[...]
        gate = jnp.dot(xn, wg_ref[...], preferred_element_type=F32)
        up = jnp.dot(xn, wu_ref[...], preferred_element_type=F32)
        return (gate * _sigmoid(gate) * (0.5 * up)).astype(BF16)

    @pl.when(f == 0)
    def _():
        x = x_ref[...]
        xn_ref[...] = _rms(x, g_ref[...]).astype(BF16)
        o_ref[...] = x
        act_ref[...] = gate_up()

    @pl.when(f > 0)
    def _():
        part = jnp.dot(act_ref[...], wd_ref[...], preferred_element_type=F32)
        act = gate_up()
        o_ref[...] += part
        act_ref[...] = act

    @pl.when(f == last)
    def _():
        o_ref[...] += jnp.dot(act_ref[...], wdl_ref[...], preferred_element_type=F32)


def _ffn(x, gain, wgu, wd, li, lj, *, tm, tf):
    n, d = x.shape
    ff = wd.shape[2]
    nf = ff // tf
    assert n % tm == 0 and ff % tf == 0 and nf >= 2
    return pl.pallas_call(
        _ffn_body,
        grid=(n // tm, nf),
        in_specs=[
            pl.BlockSpec((tm, d), lambda i, f: (i, 0)),
            pl.BlockSpec((1, d), lambda i, f: (0, 0)),
            pl.BlockSpec((None, None, d, tf), lambda i, f: (li, lj, 0, f)),
            pl.BlockSpec((None, None, d, tf), lambda i, f: (li, lj, 0, f + nf)),
            pl.BlockSpec((None, None, tf, d), lambda i, f: (li, lj, jnp.maximum(f - 1, 0), 0)),
            pl.BlockSpec((None, None, tf, d), lambda i, f: (li, lj, nf - 1, 0), pipeline_mode=pl.Buffered(1)),
        ],
        out_specs=pl.BlockSpec((tm, d), lambda i, f: (i, 0)),
        out_shape=jax.ShapeDtypeStruct((n, d), F32),
        scratch_shapes=[pltpu.VMEM((tm, d), BF16), pltpu.VMEM((tm, tf), BF16)],
        compiler_params=_params(("parallel", "arbitrary")),
        name="ffn",
    )(x, gain, wgu, wgu, wd, wd)


def _rope_store(y, cos, sin, o_ref, scale):
    for g in range(y.shape[1] // HEAD):
        yg = y[:, g * HEAD:(g + 1) * HEAD]
        r = yg * cos + pltpu.roll(yg, HEAD // 2, axis=1) * sin
        if scale != 1.0:
            r = r * scale
        o_ref[:, g * HEAD:(g + 1) * HEAD] = r.astype(o_ref.dtype)


def _norm_mm_body(x_ref, g_ref, w_ref, o_ref, xn_ref):
    @pl.when(pl.program_id(1) == 0)
    def _():
        xn_ref[...] = _rms(x_ref[...], g_ref[...]).astype(BF16)

    o_ref[...] = jnp.dot(xn_ref[...], w_ref[...], preferred_element_type=F32).astype(o_ref.dtype)


def _norm_mm_rope_body(x_ref, g_ref, w_ref, cos_ref, sin_ref, o_ref, xn_ref, *, rope_tiles, scale):
    j = pl.program_id(1)

    @pl.when(j == 0)
    def _():
        xn_ref[...] = _rms(x_ref[...], g_ref[...]).astype(BF16)

    y = jnp.dot(xn_ref[...], w_ref[...], preferred_element_type=F32)

    @pl.when(j < rope_tiles)
    def _():
        _rope_store(y, cos_ref[...], sin_ref[...], o_ref, scale)

    @pl.when(j >= rope_tiles)
    def _():
        o_ref[...] = y.astype(o_ref.dtype)


def _norm_mm(x, gain, w, w_index, *, tm, tn, out_dtype, rope=None, seq=None, rope_cols=0, scale=1.0,
             ntiles=None, src_tile=lambda j: j):
    n, d = x.shape
    nout = (w.shape[-1] // tn if ntiles is None else ntiles) * tn
    assert n % tm == 0 and w.shape[-1] % tn == 0
    nlead = w.ndim - 2
    w_mode = dict(pipeline_mode=pl.Buffered(1)) if nout == tn else {}
    w_spec = pl.BlockSpec((None,) * nlead + (d, tn), lambda i, j: tuple(w_index) + (0, src_tile(j)), **w_mode)
    in_specs = [pl.BlockSpec((tm, d), lambda i, j: (i, 0)),
                pl.BlockSpec((1, d), lambda i, j: (0, 0)),
                w_spec]
    args = [x, gain, w]
    if rope is None:
        body = _norm_mm_body
    else:
        assert seq % tm == 0 and rope_cols % tn == 0
        nseq = seq // tm
        tab_spec = pl.BlockSpec((tm, HEAD), lambda i, j: (i % nseq, 0))
        in_specs += [tab_spec, tab_spec]
        args += list(rope)
        body = functools.partial(_norm_mm_rope_body, rope_tiles=rope_cols // tn, scale=scale)
    return pl.pallas_call(
        body,
        grid=(n // tm, nout // tn),
        in_specs=in_specs,
        out_specs=pl.BlockSpec((tm, tn), lambda i, j: (i, j)),
        out_shape=jax.ShapeDtypeStruct((n, nout), out_dtype),
        scratch_shapes=[pltpu.VMEM((tm, d), BF16)],
        compiler_params=_params(("parallel", "arbitrary")),
        name="norm_mm",
    )(*args)


def _norm_mm_rows_body(*refs, nw, tn, rope_tiles, scale):
    x_ref, g_ref = refs[:2]
    w_refs = refs[2:2 + nw]
    o_ref = refs[-1]
    xn = _rms(x_ref[...], g_ref[...]).astype(BF16)
    if rope_tiles:
        cos, sin = refs[2 + nw][...], refs[3 + nw][...]
    t = 0
    for w_ref in w_refs:
        for c in range(w_ref.shape[1] // tn):
            y = jnp.dot(xn, w_ref[:, c * tn:(c + 1) * tn], preferred_element_type=F32)
            cols = slice(t * tn, (t + 1) * tn)
            if t < rope_tiles:
                _rope_store(y, cos, sin, o_ref.at[:, cols], scale)
            else:
                o_ref[:, cols] = y.astype(o_ref.dtype)
            t += 1


def _norm_mm_rows(x, gain, w, w_index, col_blocks, *, tm, tn, rope=None, seq=None, rope_cols=0, scale=1.0):
    n, d = x.shape
    nout = sum(width for _, width in col_blocks)
    assert n % tm == 0 and all(width % tn == 0 for _, width in col_blocks) and rope_cols % tn == 0
    nlead = w.ndim - 2
    in_specs = [pl.BlockSpec((tm, d), lambda i: (i, 0)), pl.BlockSpec((1, d), lambda i: (0, 0))]
    in_specs += [pl.BlockSpec((None,) * nlead + (d, width), lambda i, blk=blk: tuple(w_index) + (0, blk),
                              pipeline_mode=pl.Buffered(1)) for blk, width in col_blocks]
    args = [x, gain] + [w] * len(col_blocks)
    if rope is not None:
        assert seq % tm == 0
        nseq = seq // tm
        in_specs += [pl.BlockSpec((tm, HEAD), lambda i: (i % nseq, 0))] * 2
        args += list(rope)
    return pl.pallas_call(
        functools.partial(_norm_mm_rows_body, nw=len(col_blocks), tn=tn, rope_tiles=rope_cols // tn, scale=scale),
        grid=(n // tm,),
        in_specs=in_specs,
        out_specs=pl.BlockSpec((tm, nout), lambda i: (i, 0)),
        out_shape=jax.ShapeDtypeStruct((n, nout), BF16),
        compiler_params=_params(("parallel",)),
        name="norm_mm_rows",
    )(*args)


def _mm_resid_body(x_ref, w_ref, r_ref, o_ref):
    o_ref[...] = r_ref[...] + jnp.dot(x_ref[...], w_ref[...], preferred_element_type=F32)


def _mm_resid(x, w, w_index, resid, *, tm, tn):
    n, k = x.shape
    nout = w.shape[-1]
    assert n % tm == 0 and nout % tn == 0
    nlead = w.ndim - 2
    w_mode = dict(pipeline_mode=pl.Buffered(1)) if nout == tn else {}
    return pl.pallas_call(
        _mm_resid_body,
        grid=(n // tm, nout // tn),
        in_specs=[pl.BlockSpec((tm, k), lambda i, j: (i, 0)),
                  pl.BlockSpec((None,) * nlead + (k, tn), lambda i, j: tuple(w_index) + (0, j), **w_mode),
                  pl.BlockSpec((tm, tn), lambda i, j: (i, j))],
        out_specs=pl.BlockSpec((tm, tn), lambda i, j: (i, j)),
        out_shape=jax.ShapeDtypeStruct((n, nout), F32),
        compiler_params=_params(("parallel", "arbitrary")),
        name="mm_resid",
    )(x, w, resid)


def _ple_body(x_ref, g_ref, wg_ref, p_ref, wp_ref, fg_ref, o_ref, *, tn, final):
    x = x_ref[...]
    xn = _rms(x, g_ref[...]).astype(BF16)
    pb = p_ref[...].astype(BF16)
    d = x.shape[1]
    for c in range(d // tn):
        sl = slice(c * tn, (c + 1) * tn)
        gate = jnp.dot(xn, wg_ref[:, sl], preferred_element_type=F32)
        proj = jnp.dot(pb, wp_ref[:, sl], preferred_element_type=F32)
        o_ref[:, sl] = x_ref[:, sl] + _sigmoid(gate) * proj
    if final:
        o_ref[...] = _rms(o_ref[...], fg_ref[...])


def _ple(x, gain, wg, p, wp, layer, final_gain, *, tm, tn, final):
    n, d = x.shape
    pd = p.shape[-1]
    rows = n // tm
    assert n % tm == 0 and d % tn == 0
    return pl.pallas_call(
        functools.partial(_ple_body, tn=tn, final=final),
        grid=(rows,),
        in_specs=[pl.BlockSpec((tm, d), lambda i: (i, 0)),
                  pl.BlockSpec((1, d), lambda i: (0, 0)),
                  pl.BlockSpec((None, d, d), lambda i: (layer, 0, 0), pipeline_mode=pl.Buffered(1)),
                  pl.BlockSpec((None, tm, pd), lambda i: (layer, i, 0)),
                  pl.BlockSpec((None, pd, d), lambda i: (layer, 0, 0), pipeline_mode=pl.Buffered(1)),
                  pl.BlockSpec((1, d), lambda i: (0, 0))],
        out_specs=pl.BlockSpec((tm, d), lambda i: (i, 0)),
        out_shape=jax.ShapeDtypeStruct((n, d), F32),
        compiler_params=_params(("parallel",)),
        name="ple",
    )(x, gain, wg, p, wp, final_gain)


def _rope_table_body(cos_ref, sin_ref):
    shape = cos_ref.shape
    pos = lax.broadcasted_iota(jnp.int32, shape, 0).astype(F32)
    lane = lax.broadcasted_iota(jnp.int32, shape, 1)
    half = HEAD // 2
    k = jnp.bitwise_and(lane, half - 1).astype(F32)
    inv_freq = jnp.exp(k * (-math.log(ROPE_THETA) / half))
    ang = pos * inv_freq
    cos_ref[...] = jnp.cos(ang)
    s = jnp.sin(ang)
    sin_ref[...] = jnp.where(lane < half, -s, s)


def _rope_tables(seq):
    return pl.pallas_call(
        _rope_table_body,
        out_shape=(jax.ShapeDtypeStruct((seq, HEAD), F32), jax.ShapeDtypeStruct((seq, HEAD), F32)),
        name="rope_tables",
    )()


def _split3(x):
    hi = x.astype(BF16)
    r1 = x - hi.astype(F32)
    mid = r1.astype(BF16)
    lo = (r1 - mid.astype(F32)).astype(BF16)
    return hi, mid, lo


def _hgrn_consts():
    r = np.arange(HG_SUB)
    chunk, blk = r // HG_CHUNK, r // SUBLANES
    nch = HG_SUB // HG_CHUNK
    same_chunk = chunk[:, None] == chunk[None, :]
    tril = same_chunk & (r[None, :] <= r[:, None])
    off = same_chunk & (blk[:, None] > blk[None, :])
    diag = (blk[:, None] == blk[None, :]) & (r[None, :] <= r[:, None])
    lane_blk = np.arange(SUBLANES * HEAD) // HEAD
    spread = lane_blk[:, None] == (r % SUBLANES)[None, :]
    bpc = HG_CHUNK // SUBLANES
    kmask = np.stack([np.broadcast_to((((r % HG_CHUNK) // SUBLANES) == jb)[:, None], (HG_SUB, HEAD))
                      for jb in range(bpc - 1)])
    cmask = np.stack([np.broadcast_to((chunk == c)[None, :], (HEAD, HG_SUB)) for c in range(nch)])
    eye = np.eye(HEAD)
    return (jnp.asarray(tril, BF16), jnp.asarray(off, BF16), jnp.asarray(diag, BF16),
            jnp.asarray(spread, BF16), jnp.asarray(kmask, BF16), jnp.asarray(cmask, BF16), jnp.asarray(eye, BF16))


def _hgrn_front(q, z, v, lb, tril, kmask):
    rows = q.shape[0]
    nch = rows // HG_CHUNK
    nblk = rows // SUBLANES
    bpc = HG_CHUNK // SUBLANES

    log2e = math.log2(math.e)
    zl = z * log2e
    soft = jnp.log2(1.0 + jnp.exp2(-jnp.abs(zl)))
    lg_lb = jnp.log2(lb)
    lg_1mlb = jnp.log2(1.0 - lb)
    lg_sig = lg_1mlb + (jnp.minimum(zl, 0.0) - soft)
    lg_k = lg_1mlb - (jnp.maximum(zl, 0.0) + soft)
    logf = jnp.maximum(lg_lb, lg_sig) + jnp.log2(1.0 + jnp.exp2(-jnp.abs(lg_lb - lg_sig)))
    qs = q * (HEAD ** -0.5)

    hi, mid, lo = _split3(logf)
    parts = jnp.dot(tril, jnp.concatenate([hi, mid, lo], axis=1), preferred_element_type=F32)
    b = parts[:, :HEAD] + parts[:, HEAD:2 * HEAD] + parts[:, 2 * HEAD:]

    vb = v.astype(BF16)
    c = b - lg_k
    b3 = b.reshape(nch, HG_CHUNK, HEAD)
    b_last = b3[:, HG_CHUNK - 1:HG_CHUNK, :]
    q_in = (qs * jnp.exp2(b)).astype(BF16)
    k_end = jnp.exp2(b_last - c.reshape(nch, HG_CHUNK, HEAD)).reshape(rows, HEAD).astype(BF16)
    decay_cols = jnp.exp2(b_last).reshape(nch, HEAD).T

    qs3 = qs.reshape(nch, HG_CHUNK, HEAD)
    b8 = b.reshape(nblk, SUBLANES, HEAD)
    c8 = c.reshape(nblk, SUBLANES, HEAD)
    k_blk = jnp.exp2(b8[:, SUBLANES - 1:SUBLANES, :] - c8).reshape(rows, HEAD).astype(BF16)
    q_groups, k_groups = [], []
    for jb in range(bpc - 1):
        ref = b3[:, jb * SUBLANES + SUBLANES - 1:jb * SUBLANES + SUBLANES, :]
        qg = qs3 * jnp.exp2(jnp.minimum(b3 - ref, 0.0))
        q_groups.append(qg.reshape(rows, HEAD).astype(BF16))
        k_groups.append(k_blk * kmask[jb])
    q_cat = jnp.concatenate(q_groups, axis=1)
    k_cat = jnp.concatenate(k_groups, axis=1)

    q8 = qs.reshape(nblk, SUBLANES, HEAD)
    prods = []
    for i in range(SUBLANES):
        e = jnp.exp2(jnp.minimum(b8 - c8[:, i:i + 1, :], 0.0))
        prods.append((e * q8).reshape(rows, HEAD).astype(BF16))
    p_cat = jnp.concatenate(prods, axis=1)
    return q_cat, k_cat, p_cat, q_in, k_end, decay_cols, vb


def _hgrn_back(front, state, off, diag, spread, cmask, eye):
    q_cat, k_cat, p_cat, q_in, k_end, decay_cols, vb = front
    nch = decay_cols.shape[1]
    a_off = lax.dot_general(q_cat, k_cat, (((1,), (1,)), ((), ())), preferred_element_type=F32)
    a_diag = jnp.dot(p_cat, spread, preferred_element_type=F32)
    a = a_off.astype(BF16) * off + a_diag.astype(BF16) * diag
    o = jnp.dot(a, vb, preferred_element_type=F32)

    k_end_t = lax.dot_general(eye, k_end, (((1,), (1,)), ((), ())), preferred_element_type=F32)
    k_end_tb = k_end_t.astype(BF16)
    k_stack = jnp.concatenate([k_end_tb * cmask[c] for c in range(nch)], axis=0)
    upd = jnp.dot(k_stack, vb, preferred_element_type=F32)

    outs = []
    for c in range(nch):
        outs.append(jnp.dot(q_in[c * HG_CHUNK:(c + 1) * HG_CHUNK], state.astype(BF16), preferred_element_type=F32))
        state = state * decay_cols[:, c:c + 1] + upd[c * HEAD:(c + 1) * HEAD]
    o = o + jnp.concatenate(outs, axis=0)
    return o, state


def _hgrn_body(q_ref, z_ref, v_ref, g_ref, lbp_ref, gain_ref, tril_ref, off_ref, diag_ref, spread_ref, kmask_ref,
               cmask_ref, eye_ref, o_ref, state_ref, *, layer):
    lbp = lbp_ref[...]
    e = jnp.exp(lbp - jnp.max(lbp, axis=0, keepdims=True))
    lb = jnp.sum(e[:layer + 1], axis=0, keepdims=True) / jnp.sum(e, axis=0, keepdims=True)

    state_ref[...] = jnp.zeros_like(state_ref)
    consts = (tril_ref[...], off_ref[...], diag_ref[...], spread_ref[...], kmask_ref[...], cmask_ref[...],
              eye_ref[...])

    def slabs(s, carry):
        state = state_ref[...]
        sls = [pl.ds(pl.multiple_of((s * HG_UNROLL + u) * HG_SUB, HG_SUB), HG_SUB) for u in range(HG_UNROLL)]
        fronts = [_hgrn_front(q_ref[sl, :].astype(F32), z_ref[sl, :], v_ref[sl, :].astype(F32), lb, consts[0],
                              consts[4]) for sl in sls]
        for sl, front in zip(sls, fronts):
            o, state = _hgrn_back(front, state, consts[1], consts[2], consts[3], consts[5], consts[6])
            on = _rms(o, gain_ref[...])
            g = g_ref[sl, :].astype(F32)
            o_ref[sl, :] = (on * (g * _sigmoid(g))).astype(o_ref.dtype)
        state_ref[...] = state
        return carry

    lax.fori_loop(0, q_ref.shape[0] // (HG_SUB * HG_UNROLL), slabs, 0)


def _hgrn(qvg, z, lbp, gain, layer, *, batch, seq, heads):
    n = z.shape[0]
    assert seq % (HG_SUB * HG_UNROLL) == 0
    consts = _hgrn_consts()
    sec = lambda k: pl.BlockSpec((seq, HEAD), lambda b, h: (b, k * heads + h))
    whole = lambda a: pl.BlockSpec(a.shape, lambda b, h: (0,) * a.ndim)
    return pl.pallas_call(
        functools.partial(_hgrn_body, layer=layer),
        grid=(batch, heads),
        in_specs=[sec(0), sec(0), sec(1), sec(2),
                  pl.BlockSpec((lbp.shape[0], HEAD), lambda b, h: (0, h)),
                  pl.BlockSpec((1, HEAD), lambda b, h: (0, 0))] + [whole(a) for a in consts],
        out_specs=pl.BlockSpec((seq, HEAD), lambda b, h: (b, h)),
        out_shape=jax.ShapeDtypeStruct((n, heads * HEAD), BF16),
        scratch_shapes=[pltpu.VMEM((HEAD, HEAD), F32)],
        compiler_params=_params(("parallel", "parallel")),
        name="hgrn",
    )(qvg, z, qvg, qvg, lbp, gain, *consts)


def _attn_body(lam_ref, sub_ref, q_ref, k_ref, v_ref, o_ref, acc_ref, m_ref, l_ref, s0_ref, s1_ref, *, tq, tk,
               lambda_init):
    nrep = tk // LANES
    lam = lam_ref[...]
    lam_full = (jnp.exp(jnp.sum(lam[0:1] * lam[1:2], axis=-1, keepdims=True))
                - jnp.exp(jnp.sum(lam[2:3] * lam[3:4], axis=-1, keepdims=True)) + lambda_init)
    bufs = (s0_ref, s1_ref)

    def scores(rows, j, s_ref):
        kt = k_ref[j * tk:(j + 1) * tk, :]
        q = q_ref[rows, :]
        for c in range(2):
            cs = slice(c * HEAD, (c + 1) * HEAD)
            s_ref[c] = lax.dot_general(q[:, cs], kt[:, cs], (((1,), (1,)), ((), ())), preferred_element_type=F32)

    def consume(j, s_ref, first, masked):
        vt = v_ref[j * tk:(j + 1) * tk, :]
        ps, alphas = [], []
        for c in range(2):
            s = s_ref[c]
            if masked:
                ri = lax.broadcasted_iota(jnp.int32, s.shape, 0)
                ci = lax.broadcasted_iota(jnp.int32, s.shape, 1)
                s = jnp.where(ci <= ri, s, NEG)
            m_tile = jnp.max(s, axis=-1, keepdims=True)
            if first:
                m_new = jnp.broadcast_to(m_tile, (tq, LANES))
            else:
                m_old = m_ref[c]
                m_new = jnp.maximum(m_old, m_tile)
                alphas.append(jnp.exp2(m_old - m_new))
            p = jnp.exp2(s - jnp.concatenate([m_new] * nrep, axis=1))
            l_tile = jnp.sum(p, axis=-1, keepdims=True)
            l_ref[c] = jnp.broadcast_to(l_tile, (tq, LANES)) if first else alphas[c] * l_ref[c] + l_tile
            m_ref[c] = m_new
            ps.append(p.astype(BF16))
        pv = jnp.dot(jnp.concatenate(ps, axis=0), vt, preferred_element_type=F32)
        for c in range(2):
            part = pv[c * tq:(c + 1) * tq]
            acc_ref[c] = part if first else jnp.concatenate([alphas[c]] * 2, axis=1) * acc_ref[c] + part

    for qi in range(q_ref.shape[0] // tq):
        rows = slice(qi * tq, (qi + 1) * tq)
        scores(rows, 0, bufs[0])
        for j in range(qi + 1):
            if j < qi:
                scores(rows, j + 1, bufs[(j + 1) % 2])
            consume(j, bufs[j % 2], first=(j == 0), masked=(j == qi))
        inv0 = jnp.concatenate([1.0 / l_ref[0]] * 2, axis=1)
        inv1 = jnp.concatenate([1.0 / l_ref[1]] * 2, axis=1)
        o = acc_ref[0] * inv0 - lam_full * (acc_ref[1] * inv1)
        o_ref[rows, :] = (_rms(o, sub_ref[...]) * (1.0 - lambda_init)).astype(o_ref.dtype)


def _attention(q, kv, lam, subln, *, batch, seq, heads, tq, lambda_init):
    n = q.shape[0]
    vd = 2 * HEAD
    assert seq % tq == 0
    return pl.pallas_call(
        functools.partial(_attn_body, tq=tq, tk=tq, lambda_init=lambda_init),
        grid=(batch, heads),
        in_specs=[pl.BlockSpec((4, HEAD), lambda b, h: (0, 0)),
                  pl.BlockSpec((1, vd), lambda b, h: (0, 0)),
                  pl.BlockSpec((seq, vd), lambda b, h: (b, h)),
                  pl.BlockSpec((seq, vd), lambda b, h: (b, h)),
                  pl.BlockSpec((seq, vd), lambda b, h: (b, heads + h))],
        out_specs=pl.BlockSpec((seq, vd), lambda b, h: (b, h)),
        out_shape=jax.ShapeDtypeStruct((n, heads * vd), BF16),
        scratch_shapes=[pltpu.VMEM((2, tq, vd), F32), pltpu.VMEM((2, tq, LANES), F32),
                        pltpu.VMEM((2, tq, LANES), F32), pltpu.VMEM((2, tq, tq), F32), pltpu.VMEM((2, tq, tq), F32)],
        compiler_params=_params(("parallel", "parallel")),
        name="diff_attn",
    )(lam, subln, q, kv, kv)


def kernel(x, p, ffn_norm, ffn_w_gate_up, ffn_w_down, mix_norm, hgrn_w_in, hgrn_lower_bounds, hgrn_out_norm, hgrn_w_out, kv_norm, w_kv, diff_w_q, diff_lambda, diff_subln, diff_w_out, ple_norm, ple_w_gate, ple_w_proj, final_norm):
    batch, seq, d = x.shape
    depth = p.shape[0]
    n_a = hgrn_w_in.shape[0]
    n = batch * seq
    hg_heads = d // HEAD
    da_heads = d // (2 * HEAD)
    k_cols = da_heads * 2 * HEAD

    tm = min(1024, seq)
    tq = min(512, seq)
    wide = hg_heads * HEAD

    wgu = ffn_w_gate_up.astype(BF16)
    wdn = ffn_w_down.astype(BF16)
    w_in = hgrn_w_in.astype(BF16)
    w_ho = hgrn_w_out.astype(BF16)
    w_kvb = w_kv.astype(BF16)
    w_q = diff_w_q.astype(BF16)
    w_do = diff_w_out.astype(BF16)
    w_pg = ple_w_gate.astype(BF16)
    w_pp = ple_w_proj.astype(BF16)
    p2 = p.reshape(depth, n, p.shape[-1])

    rope = _rope_tables(seq)
    h = x.reshape(n, d)
    kv = None
    for i in range(depth):
        h = _ffn(h, ffn_norm[i, 0].reshape(1, d), wgu, wdn, i, 0, tm=tm, tf=512)
        if i < n_a:
            gain = mix_norm[i].reshape(1, d)
            z = _norm_mm(h, gain, w_in, (i,), tm=tm, tn=wide, out_dtype=F32, ntiles=1, src_tile=lambda j: j + 1)
            qvg = _norm_mm_rows(h, gain, w_in, (i,), ((0, wide), (1, 2 * wide)), tm=min(512, seq), tn=1024)
            og = _hgrn(qvg, z, hgrn_lower_bounds, hgrn_out_norm[i].reshape(1, HEAD), i,
                       batch=batch, seq=seq, heads=hg_heads)
            h = _mm_resid(og, w_ho, (i,), h, tm=tm, tn=d)
        else:
            j = i - n_a
            lambda_init = 0.8 - 0.6 * math.exp(-0.3 * i)
            q = _norm_mm(h, mix_norm[i].reshape(1, d), w_q, (j,), tm=tm, tn=k_cols, out_dtype=BF16,
                         rope=rope, seq=seq, rope_cols=k_cols, scale=HEAD ** -0.5 * math.log2(math.e))
            oa = _attention(q, kv, diff_lambda[j], diff_subln[j].reshape(1, 2 * HEAD),
                            batch=batch, seq=seq, heads=da_heads, tq=tq, lambda_init=lambda_init)
            h = _mm_resid(oa, w_do, (j,), h, tm=tm, tn=d)
        h = _ffn(h, ffn_norm[i, 1].reshape(1, d), wgu, wdn, i, 1, tm=tm, tf=512)
        last = i == depth - 1
        h = _ple(h, ple_norm[i].reshape(1, d), w_pg, p2, w_pp, i, final_norm.reshape(1, d),
                 tm=min(512, seq), tn=256, final=last)
        if i == n_a - 1:
            kv = _norm_mm_rows(h, kv_norm.reshape(1, d), w_kvb, (), ((0, w_kvb.shape[-1]),), tm=tm, tn=512,
                               rope=rope, seq=seq, rope_cols=k_cols)
    return h.reshape(batch, seq, d)
```
